```python
import math
import jax, jax.numpy as jnp
from jax import lax
import numpy as np

D_MODEL = 2048
BATCH = 2
SEQ = 4096
DEPTH = 4

HEAD_DIM = 128
BLOCK = 128
EPS = 1e-6
A_PATTERNS = ((128, 1), (512, 4), (2048, 16))
A_GROUPS = 3
A_HEADS_PER_GROUP = 4
A_HEADS = A_GROUPS * A_HEADS_PER_GROUP
A_OUT = A_HEADS_PER_GROUP * HEAD_DIM
B_Q_HEADS = 8
B_KV_HEADS = 2
B_WINDOW = 128
B_OUT = B_Q_HEADS * HEAD_DIM
C_HEADS = 4
C_OUT = C_HEADS * 2 * HEAD_DIM
N_MEM = 256
M_HEADS = 4
M_OUT = M_HEADS * HEAD_DIM
N_BRANCH = 4
BRANCH_WIDTHS = (A_OUT, B_OUT, C_OUT, M_OUT)
BRANCH_TOTAL = A_OUT + B_OUT + C_OUT + M_OUT
QK_A_Q, QK_A_K, QK_B_Q, QK_B_K, QK_C_Q, QK_C_K, QK_M_Q, QK_M_K = 0, 1, 2, 3, 4, 5, 6, 7
N_QK_NORMS = 8
N_LAMBDA_VECS = 4
IN_SIZES = (A_HEADS * HEAD_DIM, A_HEADS * HEAD_DIM, A_HEADS * HEAD_DIM,
            B_Q_HEADS * HEAD_DIM, B_KV_HEADS * HEAD_DIM, B_KV_HEADS * HEAD_DIM,
            C_HEADS * 2 * HEAD_DIM, C_HEADS * 2 * HEAD_DIM, C_OUT,
            M_HEADS * HEAD_DIM,
            A_OUT, B_OUT, C_OUT, M_OUT,
            N_BRANCH * D_MODEL)
D_IN = sum(IN_SIZES)

kernel_name = "hybrid_dilated_swa_diff_memory_gated"


def rms_norm(x, g):
    xf = x.astype(jnp.float32)
    y = xf * lax.rsqrt(jnp.mean(xf * xf, axis=-1, keepdims=True) + EPS)
    return (y * g.astype(jnp.float32)).astype(x.dtype)


def alibi_slopes(n):
    return 2.0 ** (-8.0 * jnp.arange(1, n + 1, dtype=jnp.float32) / n)


def strided(x, d):
    b, s = x.shape[:2]
    x = x.reshape(b, s // d, d, *x.shape[2:])
    x = jnp.moveaxis(x, 2, 1)
    return x.reshape(b * d, s // d, *x.shape[3:])


def unstrided(x, b, d):
    x = x.reshape(b, d, x.shape[1], *x.shape[2:])
    x = jnp.moveaxis(x, 1, 2)
    return x.reshape(b, -1, *x.shape[3:])


def banded_attention(q, k, v, slopes, max_dist, dist_scale, sinks):
    n, L, hkv, g, hd = q.shape
    nb = -(-L // BLOCK)
    lp = nb * BLOCK
    pad = lp - L
    q = jnp.pad(q, ((0, 0), (0, pad), (0, 0), (0, 0), (0, 0)))
    k = jnp.pad(k, ((0, 0), (BLOCK, pad), (0, 0), (0, 0)))
    v = jnp.pad(v, ((0, 0), (BLOCK, pad), (0, 0), (0, 0)))
    qb = q.reshape(n, nb, BLOCK, hkv, g, hd)
    kb = k.reshape(n, nb + 1, BLOCK, hkv, hd)
    vb = v.reshape(n, nb + 1, BLOCK, hkv, hd)
    kw = jnp.concatenate([kb[:, :-1], kb[:, 1:]], axis=2)
    vw = jnp.concatenate([vb[:, :-1], vb[:, 1:]], axis=2)
    s = jnp.einsum('nbqhgd,nbkhd->nbhgqk', qb, kw).astype(jnp.float32) * (hd ** -0.5)
    dist = jnp.arange(BLOCK)[:, None] + BLOCK - jnp.arange(2 * BLOCK)[None, :]
    key_idx = (jnp.arange(nb)[:, None] - 1) * BLOCK + jnp.arange(2 * BLOCK)[None, :]
    valid = (dist >= 0)[None] & (dist <= max_dist)[None] & (key_idx >= 0)[:, None, :]
    bias = -slopes.astype(jnp.float32)[:, :, None, None] * (dist * dist_scale).astype(jnp.float32)
    s = jnp.where(valid[None, :, None, None], s + bias[None, None], -jnp.inf)
    m = jnp.max(s, axis=-1, keepdims=True)
    if sinks is not None:
        sink = sinks.astype(jnp.float32)[None, None, :, :, None, None]
        m = jnp.maximum(m, sink)
    e = jnp.exp(s - m)
    denom = jnp.sum(e, axis=-1, keepdims=True)
    if sinks is not None:
        denom = denom + jnp.exp(sink - m)
    p = e / denom
    lse = (m + jnp.log(denom))[..., 0]
    o = jnp.einsum('nbhgqk,nbkhd->nbqhgd', p.astype(v.dtype), vw)
    o = o.reshape(n, lp, hkv, g, hd)[:, :L]
    lse = jnp.transpose(lse, (0, 1, 4, 2, 3)).reshape(n, lp, hkv, g)[:, :L]
    return o, lse


def diff_attention(q, k, v, slopes, lam):
    b, s_len, h, _, hd = q.shape
    nb = s_len // BLOCK
    qb = jnp.moveaxis(q.reshape(b, nb, BLOCK, h, 2, hd), 1, 0)
    kpos = jnp.arange(s_len)
    sl = slopes.astype(jnp.float32)[None, :, None, None, None]

    def one_block(args):
        qi, i = args
        sc = jnp.einsum('bqhcd,bkhcd->bhcqk', qi, k).astype(jnp.float32) * (hd ** -0.5)
        dist = (i * BLOCK + jnp.arange(BLOCK))[:, None] - kpos[None, :]
        sc = jnp.where(dist >= 0, sc - sl * dist.astype(jnp.float32), -jnp.inf)
        p = jax.nn.softmax(sc, axis=-1)
        a = p[:, :, 0] - lam * p[:, :, 1]
        return jnp.einsum('bhqk,bkhe->bqhe', a.astype(v.dtype), v)

    o = lax.map(one_block, (qb, jnp.arange(nb)))
    return jnp.moveaxis(o, 0, 1).reshape(b, s_len, h, 2 * hd)


def setup_inputs(seed: int = 0) -> dict:
    key = jax.random.key(seed)
    ks = jax.random.split(key, 13)
    f = jnp.float32
    nrm = jax.random.normal
    x = nrm(ks[0], (BATCH, SEQ, D_MODEL), f)
    mem = nrm(ks[1], (BATCH, N_MEM, D_MODEL), f)
    norm_g = 1.0 + 0.02 * nrm(ks[2], (DEPTH, D_MODEL), f)
    w_in = nrm(ks[3], (DEPTH, D_MODEL, D_IN), f) * (D_MODEL ** -0.5)
    b_gate = 0.1 * nrm(ks[4], (DEPTH, N_BRANCH, D_MODEL), f)
    qk_gain = 1.0 + 0.02 * nrm(ks[5], (DEPTH, N_QK_NORMS, HEAD_DIM), f)
    sinks = 0.5 * nrm(ks[6], (DEPTH, B_Q_HEADS), f)
    lam = 0.1 * nrm(ks[7], (DEPTH, N_LAMBDA_VECS, HEAD_DIM), f)
    subln_g = 1.0 + 0.02 * nrm(ks[8], (DEPTH, 2 * HEAD_DIM), f)
    mem_norm_g = 1.0 + 0.02 * nrm(ks[9], (DEPTH, D_MODEL), f)
    w_mem_kv = nrm(ks[10], (DEPTH, D_MODEL, 2 * M_HEADS * HEAD_DIM), f) * (D_MODEL ** -0.5)
    row_scale = jnp.concatenate([jnp.full((w,), w ** -0.5, f) for w in BRANCH_WIDTHS])
    w_branch = nrm(ks[11], (DEPTH, BRANCH_TOTAL, D_MODEL), f) * row_scale[None, :, None]
    w_out = nrm(ks[12], (DEPTH, D_MODEL, D_MODEL), f) * (D_MODEL ** -0.5)
    return {"x": x, "mem": mem, "norm_g": norm_g, "w_in": w_in, "b_gate": b_gate,
            "qk_gain": qk_gain, "sinks": sinks, "lam": lam, "subln_g": subln_g,
            "mem_norm_g": mem_norm_g, "w_mem_kv": w_mem_kv, "w_branch": w_branch, "w_out": w_out}


def reference(x, mem, norm_g, w_in, b_gate, qk_gain, sinks, lam, subln_g, mem_norm_g, w_mem_kv, w_branch, w_out):
    bsz, s_len, _ = x.shape
    split_at = [int(c) for c in np.cumsum(IN_SIZES)[:-1]]
    branch_at = [int(c) for c in np.cumsum(BRANCH_WIDTHS)[:-1]]
    slopes_a = alibi_slopes(A_HEADS_PER_GROUP)[:, None]
    slopes_b = alibi_slopes(B_Q_HEADS).reshape(B_KV_HEADS, -1)
    slopes_c = alibi_slopes(C_HEADS)
    m_scale = HEAD_DIM ** -0.5
    for l in range(DEPTH):
        h = rms_norm(x, norm_g[l])
        proj = jnp.einsum('bsd,de->bse', h, w_in[l])
        (aq, ak, av, bq, bk, bv, cq, ck, cv, mq,
         za, zb, zc, zm, gates) = jnp.split(proj, split_at, axis=-1)
        gq = qk_gain[l]

        aq = rms_norm(aq.reshape(bsz, s_len, A_GROUPS, A_HEADS_PER_GROUP, HEAD_DIM), gq[QK_A_Q])
        ak = rms_norm(ak.reshape(bsz, s_len, A_GROUPS, A_HEADS_PER_GROUP, HEAD_DIM), gq[QK_A_K])
        av = av.reshape(bsz, s_len, A_GROUPS, A_HEADS_PER_GROUP, HEAD_DIM)
        outs, lses = [], []
        for g, (win, dil) in enumerate(A_PATTERNS):
            o, lse = banded_attention(strided(aq[:, :, g], dil)[:, :, :, None], strided(ak[:, :, g], dil),
                                      strided(av[:, :, g], dil), slopes_a, win // dil, dil, None)
            outs.append(unstrided(o[:, :, :, 0], bsz, dil))
            lses.append(unstrided(lse[..., 0], bsz, dil))
        alpha = jax.nn.softmax(jnp.stack(lses), axis=0)
        ya = jnp.sum(alpha[..., None].astype(x.dtype) * jnp.stack(outs), axis=0).reshape(bsz, s_len, A_OUT)

        bq = rms_norm(bq.reshape(bsz, s_len, B_KV_HEADS, B_Q_HEADS // B_KV_HEADS, HEAD_DIM), gq[QK_B_Q])
        bk = rms_norm(bk.reshape(bsz, s_len, B_KV_HEADS, HEAD_DIM), gq[QK_B_K])
        bv = bv.reshape(bsz, s_len, B_KV_HEADS, HEAD_DIM)
        yb, _ = banded_attention(bq, bk, bv, slopes_b, B_WINDOW - 1, 1,
                                 sinks[l].reshape(B_KV_HEADS, -1))
        yb = yb.reshape(bsz, s_len, B_OUT)

        lam_init = 0.8 - 0.6 * math.exp(-0.3 * l)
        lp = lam[l].astype(jnp.float32)
        lam_full = jnp.exp(jnp.sum(lp[0] * lp[1])) - jnp.exp(jnp.sum(lp[2] * lp[3])) + lam_init
        cq = rms_norm(cq.reshape(bsz, s_len, C_HEADS, 2, HEAD_DIM), gq[QK_C_Q])
        ck = rms_norm(ck.reshape(bsz, s_len, C_HEADS, 2, HEAD_DIM), gq[QK_C_K])
        cv = cv.reshape(bsz, s_len, C_HEADS, 2 * HEAD_DIM)
        yc = diff_attention(cq, ck, cv, slopes_c, lam_full)
        yc = (rms_norm(yc, subln_g[l]) * (1.0 - lam_init)).reshape(bsz, s_len, C_OUT)

        mn = rms_norm(mem, mem_norm_g[l])
        mk, mv = jnp.split(jnp.einsum('bmd,de->bme', mn, w_mem_kv[l]), 2, axis=-1)
        n_mem = mem.shape[1]
        mk = rms_norm(mk.reshape(bsz, n_mem, M_HEADS, HEAD_DIM), gq[QK_M_K])
        mv = mv.reshape(bsz, n_mem, M_HEADS, HEAD_DIM)
        mq = rms_norm(mq.reshape(bsz, s_len, M_HEADS, HEAD_DIM), gq[QK_M_Q])
        pm = jax.nn.softmax(jnp.einsum('bshd,bmhd->bhsm', mq, mk).astype(jnp.float32) * m_scale, axis=-1)
        ym = jnp.einsum('bhsm,bmhd->bshd', pm.astype(mv.dtype), mv).reshape(bsz, s_len, M_OUT)

        gate = jax.nn.sigmoid((gates + b_gate[l].reshape(-1)).astype(jnp.float32)).astype(x.dtype)
        gate = gate.reshape(bsz, s_len, N_BRANCH, D_MODEL)
        w_parts = jnp.split(w_branch[l], branch_at, axis=0)
        branches = (ya * jax.nn.silu(za), yb * jax.nn.silu(zb), yc * jax.nn.silu(zc), ym * jax.nn.silu(zm))
        merged = gate[:, :, 0] * jnp.einsum('bse,ed->bsd', branches[0], w_parts[0])
        for i in range(1, N_BRANCH):
            merged = merged + gate[:, :, i] * jnp.einsum('bse,ed->bsd', branches[i], w_parts[i])
        x = x + jnp.einsum('bsd,de->bse', merged, w_out[l])
    return x
```

```python
import functools
import math

import jax
import jax.numpy as jnp
from jax import lax
from jax.experimental import pallas as pl
from jax.experimental.pallas import tpu as pltpu

D_MODEL = 2048
HEAD_DIM = 128
BLOCK = 128
EPS = 1e-6
A_PATTERNS = ((128, 1), (512, 4), (2048, 16))
A_GROUPS = 3
A_HEADS_PER_GROUP = 4
A_HEADS = A_GROUPS * A_HEADS_PER_GROUP
A_OUT = A_HEADS_PER_GROUP * HEAD_DIM
B_Q_HEADS = 8
B_KV_HEADS = 2
B_WINDOW = 128
B_OUT = B_Q_HEADS * HEAD_DIM
C_HEADS = 4
C_OUT = C_HEADS * 2 * HEAD_DIM
N_MEM = 256
M_HEADS = 4
M_OUT = M_HEADS * HEAD_DIM
N_BRANCH = 4
QK_A_Q, QK_A_K, QK_B_Q, QK_B_K, QK_C_Q, QK_C_K, QK_M_Q, QK_M_K = range(8)

_IN_SIZES = (A_HEADS * HEAD_DIM, A_HEADS * HEAD_DIM, A_HEADS * HEAD_DIM,
             B_Q_HEADS * HEAD_DIM, B_KV_HEADS * HEAD_DIM, B_KV_HEADS * HEAD_DIM,
             C_HEADS * 2 * HEAD_DIM, C_HEADS * 2 * HEAD_DIM, C_OUT,
             M_HEADS * HEAD_DIM, A_OUT, B_OUT, C_OUT, M_OUT, N_BRANCH * D_MODEL)
_OFF = [0]
for _s in _IN_SIZES:
    _OFF.append(_OFF[-1] + _s)
(O_AQ, O_AK, O_AV, O_BQ, O_BK, O_BV, O_CQ, O_CK, O_CV, O_MQ,
 O_ZA, O_ZB, O_ZC, O_ZM, O_GATE, O_END) = _OFF

NEG = -1e30
SCALE = HEAD_DIM ** -0.5
VMEM_LIMIT = 48 * 1024 * 1024
BF16 = jnp.bfloat16
F32 = jnp.float32


def _params(n_axes):
    return pltpu.CompilerParams(dimension_semantics=("arbitrary",) * n_axes,
                                vmem_limit_bytes=VMEM_LIMIT)


def _alibi_slopes(n):
    return [2.0 ** (-8.0 * i / n) for i in range(1, n + 1)]


def _rmsnorm_kernel(x_ref, g_ref, o_ref):
    x = x_ref[...]
    ms = jnp.mean(x * x, axis=-1, keepdims=True)
    o_ref[...] = (x * lax.rsqrt(ms + EPS) * g_ref[...]).astype(o_ref.dtype)


def rmsnorm_bf16(x, g, tm):
    t, d = x.shape
    return pl.pallas_call(
        _rmsnorm_kernel,
        out_shape=jax.ShapeDtypeStruct((t, d), BF16),
        grid=(t // tm,),
        in_specs=[pl.BlockSpec((tm, d), lambda i: (i, 0)),
                  pl.BlockSpec((1, d), lambda i: (0, 0))],
        out_specs=pl.BlockSpec((tm, d), lambda i: (i, 0)),
        compiler_params=_params(1),
        name="rmsnorm",
    )(x, g.reshape(1, d))


def _mode_ranges(tile_modes):
    ranges, start = [], 0
    for j in range(1, len(tile_modes) + 1):
        if j == len(tile_modes) or tile_modes[j] != tile_modes[start]:
            ranges.append((tile_modes[start], start, j))
            start = j
    return ranges


def _proj_kernel(h_ref, w_ref, gain_ref, o_ref, *, tile_modes, tn):
    j = pl.program_id(1)
    acc = jnp.dot(h_ref[...], w_ref[...], preferred_element_type=F32)
    for mode, lo, hi in _mode_ranges(tile_modes):
        @pl.when((j >= lo) & (j < hi))
        def _(mode=mode):
            if mode == 'P':
                o_ref[...] = acc.astype(o_ref.dtype)
            elif mode == 'S':
                o_ref[...] = (acc * jax.nn.sigmoid(acc)).astype(o_ref.dtype)
            else:
                for c in range(tn // HEAD_DIM):
                    sl = slice(c * HEAD_DIM, (c + 1) * HEAD_DIM)
                    y = acc[:, sl]
                    ms = jnp.mean(y * y, axis=-1, keepdims=True)
                    o_ref[:, sl] = (y * lax.rsqrt(ms + EPS) * gain_ref[:, sl]).astype(o_ref.dtype)


def proj(h, w, gain, tile_modes, tm, tn, out_dtype, name):
    t, k = h.shape
    n = w.shape[1]
    assert n == tn * len(tile_modes) and t % tm == 0
    return pl.pallas_call(
        functools.partial(_proj_kernel, tile_modes=tuple(tile_modes), tn=tn),
        out_shape=jax.ShapeDtypeStruct((t, n), out_dtype),
        grid=(t // tm, n // tn),
        in_specs=[pl.BlockSpec((tm, k), lambda i, j: (i, 0)),
                  pl.BlockSpec((k, tn), lambda i, j: (0, j)),
                  pl.BlockSpec((1, tn), lambda i, j: (0, j))],
        out_specs=pl.BlockSpec((tm, tn), lambda i, j: (i, j)),
        compiler_params=_params(2),
        name=name,
    )(h, w, gain)


def _banded_kernel(*refs, n_q, n_kv, slopes, max_dist, dist_scale, tq,
                   has_sink, mul_z, write_lse):
    refs = list(refs)
    q_ref, kc_ref, kp_ref, vc_ref, vp_ref = refs[:5]
    pos = 5
    sink_ref = z_ref = lse_ref = None
    if has_sink:
        sink_ref = refs[pos]; pos += 1
    if mul_z:
        z_ref = refs[pos]; pos += 1
    o_ref = refs[pos]; pos += 1
    if write_lse:
        lse_ref = refs[pos]; pos += 1

    first_tile = pl.program_id(2) == 0
    r = lax.broadcasted_iota(jnp.int32, (BLOCK, BLOCK), 0)
    c = lax.broadcasted_iota(jnp.int32, (BLOCK, BLOCK), 1)
    d_cur = r - c
    d_prev = r + BLOCK - c
    valid_cur = d_cur >= 0
    valid_prev = d_prev <= max_dist
    dc = (d_cur * dist_scale).astype(F32)
    dp = (d_prev * dist_scale).astype(F32)
    group = n_q // n_kv
    contract_last = (((1,), (1,)), ((), ()))

    for h in range(n_q):
        kvh = h // group
        hs = slice(h * HEAD_DIM, (h + 1) * HEAD_DIM)
        ks = slice(kvh * HEAD_DIM, (kvh + 1) * HEAD_DIM)
        bias_c = jnp.where(valid_cur, -slopes[h] * dc, NEG)
        bias_p = jnp.where(valid_prev, -slopes[h] * dp, NEG)
        for j in range(tq // BLOCK):
            rows = slice(j * BLOCK, (j + 1) * BLOCK)
            q = q_ref[rows, hs]
            kc = kc_ref[rows, ks]
            vc = vc_ref[rows, ks]
            if j == 0:
                kp = kp_ref[:, ks]
                vp = vp_ref[:, ks]
            else:
                prows = slice((j - 1) * BLOCK, j * BLOCK)
                kp = kc_ref[prows, ks]
                vp = vc_ref[prows, ks]
            s_c = lax.dot_general(q, kc, contract_last, preferred_element_type=F32) * SCALE + bias_c
            s_p = lax.dot_general(q, kp, contract_last, preferred_element_type=F32) * SCALE + bias_p
            if j == 0:
                s_p = jnp.where(first_tile, NEG, s_p)
            m = jnp.maximum(jnp.max(s_c, axis=-1, keepdims=True),
                            jnp.max(s_p, axis=-1, keepdims=True))
            if has_sink:
                m = jnp.maximum(m, sink_ref[h])
            e_c = jnp.exp(s_c - m)
            e_p = jnp.exp(s_p - m)
            denom = jnp.sum(e_c, axis=-1, keepdims=True) + jnp.sum(e_p, axis=-1, keepdims=True)
            if has_sink:
                denom = denom + jnp.exp(sink_ref[h] - m)
            pv = (jnp.dot(e_c.astype(BF16), vc, preferred_element_type=F32)
                  + jnp.dot(e_p.astype(BF16), vp, preferred_element_type=F32))
            o = pv * (1.0 / denom)
            if mul_z:
                o = o * z_ref[rows, hs]
            o_ref[rows, hs] = o.astype(o_ref.dtype)
            if write_lse:
                lse_ref[rows, hs] = jnp.broadcast_to(m + jnp.log(denom), (BLOCK, HEAD_DIM))


def banded_attention(qkv, *, rows_per_seq, n_phase, qcol, kcol, vcol, n_q, n_kv,
                     slopes, max_dist, dist_scale, tq, out_cols_total, out_dtype,
                     sinks=None, z=None, zcol=0, write_lse=False, name="banded"):
    rows_total, width = qkv.shape
    n_batch = rows_total // rows_per_seq
    cw = width // n_phase
    nrt = rows_per_seq // tq
    qw, kw = n_q * HEAD_DIM, n_kv * HEAD_DIM
    assert qcol % qw == 0 and kcol % kw == 0 and vcol % kw == 0
    assert n_phase == 1 or (cw % qw == 0 and cw % kw == 0)
    sub = tq // BLOCK

    def cur_rows(b, p, i):
        return b * nrt + i

    def prev_rows(b, p, i):
        return jnp.maximum((b * nrt + i) * sub - 1, 0)

    in_specs = [
        pl.BlockSpec((tq, qw), lambda b, p, i: (cur_rows(b, p, i), (p * cw + qcol) // qw)),
        pl.BlockSpec((tq, kw), lambda b, p, i: (cur_rows(b, p, i), (p * cw + kcol) // kw)),
        pl.BlockSpec((BLOCK, kw), lambda b, p, i: (prev_rows(b, p, i), (p * cw + kcol) // kw)),
        pl.BlockSpec((tq, kw), lambda b, p, i: (cur_rows(b, p, i), (p * cw + vcol) // kw)),
        pl.BlockSpec((BLOCK, kw), lambda b, p, i: (prev_rows(b, p, i), (p * cw + vcol) // kw)),
    ]
    args = [qkv, qkv, qkv, qkv, qkv]
    if sinks is not None:
        in_specs.append(pl.BlockSpec(memory_space=pltpu.SMEM))
        args.append(sinks)
    if z is not None:
        assert n_phase == 1 and zcol % qw == 0
        in_specs.append(pl.BlockSpec((tq, qw), lambda b, p, i: (cur_rows(b, p, i), zcol // qw)))
        args.append(z)
    out_block = pl.BlockSpec((tq, qw), lambda b, p, i: (cur_rows(b, p, i), p))
    out_sds = jax.ShapeDtypeStruct((rows_total, n_phase * qw), out_dtype)
    out_shape, out_specs = out_sds, out_block
    if write_lse:
        out_shape = (out_sds, jax.ShapeDtypeStruct((rows_total, n_phase * qw), F32))
        out_specs = (out_block, out_block)
    kern = functools.partial(
        _banded_kernel, n_q=n_q, n_kv=n_kv, slopes=tuple(slopes), max_dist=max_dist,
        dist_scale=dist_scale, tq=tq, has_sink=sinks is not None, mul_z=z is not None,
        write_lse=write_lse)
    return pl.pallas_call(
        kern, out_shape=out_shape, grid=(n_batch, n_phase, nrt),
        in_specs=in_specs, out_specs=out_specs,
        compiler_params=_params(3), name=name,
    )(*args)


def _combine_kernel(o0, o1, o2, l0, l1, l2, z_ref, u_ref):
    a0, a1, a2 = l0[...], l1[...], l2[...]
    m = jnp.maximum(jnp.maximum(a0, a1), a2)
    e0, e1, e2 = jnp.exp(a0 - m), jnp.exp(a1 - m), jnp.exp(a2 - m)
    inv = 1.0 / (e0 + e1 + e2)
    ya = (e0 * inv) * o0[...] + (e1 * inv) * o1[...] + (e2 * inv) * o2[...]
    u_ref[...] = (ya * z_ref[...]).astype(u_ref.dtype)


def combine_a(outs, lses, z, zcol, tm):
    t, w = outs[0].shape
    blk = pl.BlockSpec((tm, w), lambda i: (i, 0))
    return pl.pallas_call(
        _combine_kernel,
        out_shape=jax.ShapeDtypeStruct((t, w), BF16),
        grid=(t // tm,),
        in_specs=[blk] * 6 + [pl.BlockSpec((tm, w), lambda i: (i, zcol // w))],
        out_specs=blk,
        compiler_params=_params(1),
        name="combine_a",
    )(*outs, *lses, z)


def _diff_kernel(slope_ref, q_ref, k_ref, v_ref, lam_ref, g_ref, z_ref, o_ref,
                 m_sc, l_sc, acc_sc, *, tq, lam_init):
    h = pl.program_id(1)
    qi = pl.program_id(2)
    slope = slope_ref[h]
    r = lax.broadcasted_iota(jnp.int32, (tq, tq), 0)
    c = lax.broadcasted_iota(jnp.int32, (tq, tq), 1)
    rc = (r - c).astype(F32)
    contract_last = (((1,), (1,)), ((), ()))

    m_sc[...] = jnp.full(m_sc.shape, NEG, F32)
    l_sc[...] = jnp.zeros(l_sc.shape, F32)
    acc_sc[...] = jnp.zeros(acc_sc.shape, F32)

    def step(kj, diagonal):
        start = pl.multiple_of(kj * tq, tq)
        v = v_ref[pl.ds(start, tq), :]
        off = ((qi - kj) * tq).astype(F32)
        for comp in range(2):
            cs = slice(comp * HEAD_DIM, (comp + 1) * HEAD_DIM)
            q = q_ref[:, cs]
            k = k_ref[pl.ds(start, tq), cs]
            s = lax.dot_general(q, k, contract_last, preferred_element_type=F32) * SCALE
            s = s - slope * (rc + off)
            if diagonal:
                s = jnp.where(r >= c, s, NEG)
            m_old = m_sc[comp]
            m_new = jnp.maximum(m_old, jnp.max(s, axis=-1, keepdims=True))
            alpha = jnp.exp(m_old - m_new)
            e = jnp.exp(s - m_new)
            l_sc[comp] = alpha * l_sc[comp] + jnp.sum(e, axis=-1, keepdims=True)
            acc_sc[comp] = alpha * acc_sc[comp] + jnp.dot(e.astype(BF16), v, preferred_element_type=F32)
            m_sc[comp] = m_new

    def body(kj, carry):
        step(kj, False)
        return carry

    lax.fori_loop(0, qi, body, 0)
    step(qi, True)

    lp = lam_ref[...]
    lam_full = (jnp.exp(jnp.sum(lp[0:1] * lp[1:2], axis=-1, keepdims=True))
                - jnp.exp(jnp.sum(lp[2:3] * lp[3:4], axis=-1, keepdims=True)) + lam_init)
    y = acc_sc[0] * (1.0 / l_sc[0]) - lam_full * (acc_sc[1] * (1.0 / l_sc[1]))
    ms = jnp.mean(y * y, axis=-1, keepdims=True)
    y = y * lax.rsqrt(ms + EPS) * g_ref[...] * (1.0 - lam_init)
    o_ref[...] = (y * z_ref[...]).astype(o_ref.dtype)


def diff_attention(cp, slopes, lam_l, subln_g, z, *, seq, qcol, kcol, vcol, zcol, tq, lam_init):
    t = cp.shape[0]
    n_batch = t // seq
    nq = seq // tq
    hw = 2 * HEAD_DIM
    kern = functools.partial(_diff_kernel, tq=tq, lam_init=lam_init)
    return pl.pallas_call(
        kern,
        out_shape=jax.ShapeDtypeStruct((t, C_OUT), BF16),
        grid=(n_batch, C_HEADS, nq),
        in_specs=[
            pl.BlockSpec(memory_space=pltpu.SMEM),
            pl.BlockSpec((tq, hw), lambda b, h, i: (b * nq + i, qcol // hw + h)),
            pl.BlockSpec((seq, hw), lambda b, h, i: (b, kcol // hw + h)),
            pl.BlockSpec((seq, hw), lambda b, h, i: (b, vcol // hw + h)),
            pl.BlockSpec((4, HEAD_DIM), lambda b, h, i: (0, 0)),
            pl.BlockSpec((1, hw), lambda b, h, i: (0, 0)),
            pl.BlockSpec((tq, hw), lambda b, h, i: (b * nq + i, zcol // hw + h)),
        ],
        out_specs=pl.BlockSpec((tq, hw), lambda b, h, i: (b * nq + i, h)),
        scratch_shapes=[pltpu.VMEM((2, tq, 1), F32), pltpu.VMEM((2, tq, 1), F32),
                        pltpu.VMEM((2, tq, hw), F32)],
        compiler_params=_params(3),
        name="diff_attention",
    )(slopes, cp, cp, cp, lam_l, subln_g.reshape(1, hw), z)


def _mem_kernel(q_ref, k_ref, v_ref, z_ref, o_ref):
    contract_last = (((1,), (1,)), ((), ()))
    for h in range(M_HEADS):
        hs = slice(h * HEAD_DIM, (h + 1) * HEAD_DIM)
        s = lax.dot_general(q_ref[:, hs], k_ref[:, hs], contract_last,
                            preferred_element_type=F32) * SCALE
        m = jnp.max(s, axis=-1, keepdims=True)
        e = jnp.exp(s - m)
        denom = jnp.sum(e, axis=-1, keepdims=True)
        o = jnp.dot(e.astype(BF16), v_ref[:, hs], preferred_element_type=F32) * (1.0 / denom)
        o_ref[:, hs] = (o * z_ref[:, hs]).astype(o_ref.dtype)


def mem_attention(cp, mkv, z, *, seq, qcol, zcol, tq):
    t = cp.shape[0]
    nq = seq // tq
    return pl.pallas_call(
        _mem_kernel,
        out_shape=jax.ShapeDtypeStruct((t, M_OUT), BF16),
        grid=(t // tq,),
        in_specs=[
            pl.BlockSpec((tq, M_OUT), lambda i: (i, qcol // M_OUT)),
            pl.BlockSpec((N_MEM, M_OUT), lambda i: (i // nq, 0)),
            pl.BlockSpec((N_MEM, M_OUT), lambda i: (i // nq, 1)),
            pl.BlockSpec((tq, M_OUT), lambda i: (i, zcol // M_OUT)),
        ],
        out_specs=pl.BlockSpec((tq, M_OUT), lambda i: (i, 0)),
        compiler_params=_params(1),
        name="mem_attention",
    )(cp, mkv, mkv, z)


def _merge_kernel(h_ref, ua, ub, uc, um, wg0, wg1, wg2, wg3, bg0, bg1, bg2, bg3,
                  wb0, wb1, wb2, wb3, o_ref):
    h = h_ref[...]
    acc = None
    for u, wg, bg, wb in ((ua, wg0, bg0, wb0), (ub, wg1, bg1, wb1),
                          (uc, wg2, bg2, wb2), (um, wg3, bg3, wb3)):
        gate = jax.nn.sigmoid(jnp.dot(h, wg[...], preferred_element_type=F32) + bg[...])
        term = gate * jnp.dot(u[...], wb[...], preferred_element_type=F32)
        acc = term if acc is None else acc + term
    o_ref[...] = acc.astype(o_ref.dtype)


def gated_merge(h, us, wg, bg, wbs, tm, tn):
    t, d = h.shape
    nj = d // tn
    in_specs = [pl.BlockSpec((tm, d), lambda i, j: (i, 0))]
    in_specs += [pl.BlockSpec((tm, u.shape[1]), lambda i, j: (i, 0)) for u in us]
    in_specs += [pl.BlockSpec((d, tn), functools.partial(lambda i, j, b: (0, b * nj + j), b=b))
                 for b in range(N_BRANCH)]
    in_specs += [pl.BlockSpec((1, tn), functools.partial(lambda i, j, b: (0, b * nj + j), b=b))
                 for b in range(N_BRANCH)]
    in_specs += [pl.BlockSpec((wb.shape[0], tn), lambda i, j: (0, j)) for wb in wbs]
    return pl.pallas_call(
        _merge_kernel,
        out_shape=jax.ShapeDtypeStruct((t, d), BF16),
        grid=(t // tm, nj),
        in_specs=in_specs,
        out_specs=pl.BlockSpec((tm, tn), lambda i, j: (i, j)),
        compiler_params=_params(2),
        name="gated_merge",
    )(h, *us, wg, wg, wg, wg, bg, bg, bg, bg, *wbs)


def _out_kernel(x_ref, m_ref, w_ref, o_ref):
    o_ref[...] = x_ref[...] + jnp.dot(m_ref[...], w_ref[...], preferred_element_type=F32)


def out_proj(x, merged, w, tm, tn):
    t, d = x.shape
    return pl.pallas_call(
        _out_kernel,
        out_shape=jax.ShapeDtypeStruct((t, d), F32),
        grid=(t // tm, d // tn),
        in_specs=[pl.BlockSpec((tm, tn), lambda i, j: (i, j)),
                  pl.BlockSpec((tm, d), lambda i, j: (i, 0)),
                  pl.BlockSpec((d, tn), lambda i, j: (0, j))],
        out_specs=pl.BlockSpec((tm, tn), lambda i, j: (i, j)),
        compiler_params=_params(2),
        name="out_proj",
    )(x, merged, w)


def _tile_gain(g, n):
    return jnp.tile(g, n)


def kernel(x, mem, norm_g, w_in, b_gate, qk_gain, sinks, lam, subln_g, mem_norm_g,
           w_mem_kv, w_branch, w_out):
    bsz, s_len, d = x.shape
    depth = w_in.shape[0]
    t = bsz * s_len
    xf = x.reshape(t, d)
    memf = mem.reshape(bsz * N_MEM, d)

    slopes_a = _alibi_slopes(A_HEADS_PER_GROUP)
    slopes_b = _alibi_slopes(B_Q_HEADS)
    slopes_c = jnp.asarray(_alibi_slopes(C_HEADS), F32)
    ones = lambda n: jnp.ones((n,), F32)

    z_zb, z_zc, z_za, z_zm = 0, B_OUT, B_OUT + C_OUT, B_OUT + C_OUT + A_OUT
    a_w = 3 * A_HEADS * HEAD_DIM
    c_q, c_k, c_v, c_mq = 0, C_OUT, 2 * C_OUT, 3 * C_OUT

    for l in range(depth):
        w = w_in[l]
        gq = qk_gain[l]
        w_a = w[:, O_AQ:O_BQ].astype(BF16)
        w_b = w[:, O_BQ:O_CQ].astype(BF16)
        w_c = w[:, O_CQ:O_ZA].astype(BF16)
        w_z = jnp.concatenate([w[:, O_ZB:O_ZM], w[:, O_ZA:O_ZB], w[:, O_ZM:O_GATE]], axis=1).astype(BF16)
        w_g = w[:, O_GATE:].astype(BF16)
        gain_a = jnp.concatenate([_tile_gain(gq[QK_A_Q], A_HEADS), _tile_gain(gq[QK_A_K], A_HEADS),
                                  ones(A_HEADS * HEAD_DIM)]).reshape(1, -1)
        gain_b = jnp.concatenate([_tile_gain(gq[QK_B_Q], B_Q_HEADS), _tile_gain(gq[QK_B_K], B_KV_HEADS),
                                  ones(B_KV_HEADS * HEAD_DIM)]).reshape(1, -1)
        gain_c = jnp.concatenate([_tile_gain(gq[QK_C_Q], 2 * C_HEADS), _tile_gain(gq[QK_C_K], 2 * C_HEADS),
                                  ones(C_OUT), _tile_gain(gq[QK_M_Q], M_HEADS)]).reshape(1, -1)
        gain_z = ones(w_z.shape[1]).reshape(1, -1)

        h = rmsnorm_bf16(xf, norm_g[l], 512)
        ap = proj(h, w_a, gain_a, "NNNNPP", 1024, 768, BF16, "proj_a")
        bp = proj(h, w_b, gain_b, "NNNNNP", 1024, 256, BF16, "proj_b")
        cp = proj(h, w_c, gain_c, "NNNNPPN", 1024, 512, BF16, "proj_c")
        sz = proj(h, w_z, gain_z, "SSSS", 1024, 768, F32, "proj_z")

        outs, lses = [], []
        for g, (win, dil) in enumerate(A_PATTERNS):
            rows = s_len // dil
            o_g, lse_g = banded_attention(
                ap.reshape(t // dil, dil * a_w), rows_per_seq=rows, n_phase=dil,
                qcol=g * A_OUT, kcol=A_HEADS * HEAD_DIM + g * A_OUT,
                vcol=2 * A_HEADS * HEAD_DIM + g * A_OUT,
                n_q=A_HEADS_PER_GROUP, n_kv=A_HEADS_PER_GROUP, slopes=slopes_a,
                max_dist=win // dil, dist_scale=dil, tq=min(512, rows),
                out_cols_total=A_OUT, out_dtype=F32, write_lse=True, name=f"attn_a{g}")
            outs.append(o_g.reshape(t, A_OUT))
            lses.append(lse_g.reshape(t, A_OUT))
        u_a = combine_a(outs, lses, sz, z_za, 1024)

        u_b = banded_attention(
            bp, rows_per_seq=s_len, n_phase=1, qcol=0, kcol=B_OUT,
            vcol=B_OUT + B_KV_HEADS * HEAD_DIM, n_q=B_Q_HEADS, n_kv=B_KV_HEADS,
            slopes=slopes_b, max_dist=B_WINDOW - 1, dist_scale=1, tq=512,
            out_cols_total=B_OUT, out_dtype=BF16, sinks=sinks[l], z=sz, zcol=z_zb,
            name="attn_b")

        lam_init = 0.8 - 0.6 * math.exp(-0.3 * l)
        u_c = diff_attention(cp, slopes_c, lam[l], subln_g[l], sz, seq=s_len,
                             qcol=c_q, kcol=c_k, vcol=c_v, zcol=z_zc, tq=256, lam_init=lam_init)

        mn = rmsnorm_bf16(memf, mem_norm_g[l], 256)
        gain_m = jnp.concatenate([_tile_gain(gq[QK_M_K], M_HEADS), ones(M_OUT)]).reshape(1, -1)
        mkv = proj(mn, w_mem_kv[l].astype(BF16), gain_m, "NP", bsz * N_MEM, M_OUT, BF16, "proj_mem")
        u_m = mem_attention(cp, mkv, sz, seq=s_len, qcol=c_mq, zcol=z_zm, tq=512)

        wb = w_branch[l].astype(BF16)
        wbs = [wb[0:A_OUT], wb[A_OUT:A_OUT + B_OUT], wb[A_OUT + B_OUT:A_OUT + B_OUT + C_OUT],
               wb[A_OUT + B_OUT + C_OUT:]]
        merged = gated_merge(h, [u_a, u_b, u_c, u_m], w_g, b_gate[l].reshape(1, -1), wbs, 1024, 256)
        xf = out_proj(xf, merged, w_out[l].astype(BF16), 1024, 1024)

    return xf.reshape(bsz, s_len, d)
```

```python
import functools
import math

import jax
import jax.numpy as jnp
from jax import lax
from jax.experimental import pallas as pl
from jax.experimental.pallas import tpu as pltpu

D_MODEL = 2048
HEAD_DIM = 128
BLOCK = 128
EPS = 1e-6
A_PATTERNS = ((128, 1), (512, 4), (2048, 16))
A_GROUPS = 3
A_HEADS_PER_GROUP = 4
A_HEADS = A_GROUPS * A_HEADS_PER_GROUP
A_OUT = A_HEADS_PER_GROUP * HEAD_DIM
B_Q_HEADS = 8
B_KV_HEADS = 2
B_WINDOW = 128
B_OUT = B_Q_HEADS * HEAD_DIM
C_HEADS = 4
C_OUT = C_HEADS * 2 * HEAD_DIM
N_MEM = 256
M_HEADS = 4
M_OUT = M_HEADS * HEAD_DIM
N_BRANCH = 4
QK_A_Q, QK_A_K, QK_B_Q, QK_B_K, QK_C_Q, QK_C_K, QK_M_Q, QK_M_K = range(8)

_IN_SIZES = (A_HEADS * HEAD_DIM, A_HEADS * HEAD_DIM, A_HEADS * HEAD_DIM,
             B_Q_HEADS * HEAD_DIM, B_KV_HEADS * HEAD_DIM, B_KV_HEADS * HEAD_DIM,
             C_HEADS * 2 * HEAD_DIM, C_HEADS * 2 * HEAD_DIM, C_OUT,
             M_HEADS * HEAD_DIM, A_OUT, B_OUT, C_OUT, M_OUT, N_BRANCH * D_MODEL)
_OFF = [0]
for _s in _IN_SIZES:
    _OFF.append(_OFF[-1] + _s)
(O_AQ, O_AK, O_AV, O_BQ, O_BK, O_BV, O_CQ, O_CK, O_CV, O_MQ,
 O_ZA, O_ZB, O_ZC, O_ZM, O_GATE, O_END) = _OFF

NEG = -1e30
SCALE = HEAD_DIM ** -0.5
VMEM_LIMIT = 52 * 1024 * 1024
PROJ_TN = 512
DIFF_UNROLL = (4, 2, 1)
LOG2E = math.log2(math.e)
BF16 = jnp.bfloat16
F32 = jnp.float32
CONTRACT_LAST = (((1,), (1,)), ((), ()))
CONTRACT_FIRST = (((0,), (0,)), ((), ()))


def _params(n_axes):
    return pltpu.CompilerParams(dimension_semantics=("arbitrary",) * n_axes,
                                vmem_limit_bytes=VMEM_LIMIT)


def _alibi_slopes(n):
    return [2.0 ** (-8.0 * i / n) for i in range(1, n + 1)]


def _rmsnorm_kernel(x_ref, g_ref, o_ref, *perm_refs, dils, tm):
    x = x_ref[...]
    ms = jnp.mean(x * x, axis=-1, keepdims=True)
    h = (x * lax.rsqrt(ms + EPS) * g_ref[...]).astype(BF16)
    o_ref[...] = h
    row = lax.broadcasted_iota(jnp.int32, (tm, tm), 0)
    col = lax.broadcasted_iota(jnp.int32, (tm, tm), 1)
    for d, p_ref in zip(dils, perm_refs):
        n = tm // d
        src = (row % n) * d + row // n
        perm = jnp.where(col == src, 1.0, 0.0).astype(BF16)
        hp = jnp.dot(perm, h, preferred_element_type=F32).astype(BF16)
        for p in range(d):
            p_ref[p] = hp[p * n:(p + 1) * n, :]


def rmsnorm_bf16(x, g, tm, dils=()):
    t, d = x.shape
    out_shape = [jax.ShapeDtypeStruct((t, d), BF16)]
    out_specs = [pl.BlockSpec((tm, d), lambda i: (i, 0))]
    for dil in dils:
        out_shape.append(jax.ShapeDtypeStruct((dil, t // dil, d), BF16))
        out_specs.append(pl.BlockSpec((dil, tm // dil, d), lambda i: (0, i, 0)))
    outs = pl.pallas_call(
        functools.partial(_rmsnorm_kernel, dils=tuple(dils), tm=tm),
        out_shape=out_shape,
        grid=(t // tm,),
        in_specs=[pl.BlockSpec((tm, d), lambda i: (i, 0)),
                  pl.BlockSpec((1, d), lambda i: (0, 0))],
        out_specs=out_specs,
        compiler_params=_params(1),
        name="rmsnorm",
    )(x, g.reshape(1, d))
    return outs


def _mode_ranges(tile_modes):
    ranges, start = [], 0
    for j in range(1, len(tile_modes) + 1):
        if j == len(tile_modes) or tile_modes[j] != tile_modes[start]:
            ranges.append((tile_modes[start], start, j))
            start = j
    return ranges


def _proj_kernel(h_ref, w_ref, gain_ref, o_ref, w_sc, *, tile_modes):
    j = pl.program_id(0)

    @pl.when(pl.program_id(1) == 0)
    def _():
        w_sc[...] = w_ref[...].astype(BF16)

    acc = jnp.dot(h_ref[...], w_sc[...], preferred_element_type=F32)
    for modes, lo, hi in _mode_ranges(tile_modes):
        @pl.when((j >= lo) & (j < hi))
        def _(modes=modes):
            for c, mode in enumerate(modes):
                sl = slice(c * HEAD_DIM, (c + 1) * HEAD_DIM)
                y = acc[:, sl]
                if mode == 'S':
                    y = y * jax.nn.sigmoid(y)
                elif mode == 'N':
                    ms = jnp.mean(y * y, axis=-1, keepdims=True)
                    y = y * lax.rsqrt(ms + EPS) * gain_ref[:, sl]
                o_ref[:, sl] = y.astype(o_ref.dtype)


def proj(h, w3, layer, colmap, gain, tile_modes, tm, out_dtype, name):
    t, k = h.shape
    tn = PROJ_TN
    n = tn * len(tile_modes)
    assert t % tm == 0 and all(len(m) == tn // HEAD_DIM for m in tile_modes)
    return pl.pallas_call(
        functools.partial(_proj_kernel, tile_modes=tuple(tile_modes)),
        out_shape=jax.ShapeDtypeStruct((t, n), out_dtype),
        grid=(n // tn, t // tm),
        in_specs=[pl.BlockSpec((tm, k), lambda j, i: (i, 0)),
                  pl.BlockSpec((None, k, tn), lambda j, i: (layer, 0, colmap(j))),
                  pl.BlockSpec((1, tn), lambda j, i: (0, j))],
        out_specs=pl.BlockSpec((tm, tn), lambda j, i: (i, j)),
        scratch_shapes=[pltpu.VMEM((k, tn), BF16)],
        compiler_params=_params(2),
        name=name,
    )(h, w3, gain)


def _banded_kernel(*refs, n_q, n_kv, slopes, max_dist, dist_scale, tq,
                   has_sink, mul_z, write_lse):
    refs = list(refs)
    q_ref, kc_ref, kp_ref, vc_ref, vp_ref = refs[:5]
    pos = 5
    sink_ref = z_ref = lse_ref = None
    if has_sink:
        sink_ref = refs[pos]; pos += 1
    if mul_z:
        z_ref = refs[pos]; pos += 1
    o_ref = refs[pos]; pos += 1
    if write_lse:
        lse_ref = refs[pos]; pos += 1

    shared_kv = n_kv < n_q
    unit = 4
    assert n_q % unit == 0 and (not shared_kv or n_q // n_kv == unit)
    kk = lax.broadcasted_iota(jnp.int32, (2 * BLOCK, BLOCK), 0)
    r = lax.broadcasted_iota(jnp.int32, (2 * BLOCK, BLOCK), 1)
    dist = r + BLOCK - kk
    distf = (dist * dist_scale).astype(F32)
    valid = (dist >= 0) & (dist <= max_dist)
    has_prev = jnp.minimum(pl.program_id(1), 1) * BLOCK
    valid0 = valid & (kk + has_prev >= BLOCK)

    for u in range(n_q // unit):
        heads = list(range(u * unit, (u + 1) * unit))
        bias = jnp.concatenate([jnp.where(valid, -slopes[h] * distf, NEG) for h in heads], axis=1)
        bias0 = jnp.concatenate([jnp.where(valid0, -slopes[h] * distf, NEG) for h in heads], axis=1)
        if has_sink:
            sink = jnp.concatenate([jnp.full((1, BLOCK), sink_ref[h], F32) for h in heads], axis=1)
        for j in range(tq // BLOCK):
            rows = slice(j * BLOCK, (j + 1) * BLOCK)
            prows = slice((j - 1) * BLOCK, j * BLOCK)

            def window(cur_ref, prev_ref, kv):
                ks = slice(kv * HEAD_DIM, (kv + 1) * HEAD_DIM)
                prev = prev_ref[:, ks] if j == 0 else cur_ref[prows, ks]
                return jnp.concatenate([prev, cur_ref[rows, ks]], axis=0)

            qs = [q_ref[rows, h * HEAD_DIM:(h + 1) * HEAD_DIM] for h in heads]
            if shared_kv:
                kwin = window(kc_ref, kp_ref, u)
                s_t = lax.dot_general(kwin, jnp.concatenate(qs, axis=0), CONTRACT_LAST,
                                      preferred_element_type=F32)
            else:
                s_t = jnp.concatenate(
                    [lax.dot_general(window(kc_ref, kp_ref, h), q, CONTRACT_LAST,
                                     preferred_element_type=F32) for h, q in zip(heads, qs)], axis=1)
            s_t = s_t * SCALE + (bias0 if j == 0 else bias)
            m = jnp.max(s_t, axis=0, keepdims=True)
            if has_sink:
                m = jnp.maximum(m, sink)
            e = jnp.exp(s_t - m)
            denom = jnp.sum(e, axis=0, keepdims=True)
            if has_sink:
                denom = denom + jnp.exp(sink - m)
            p_t = (e * (1.0 / denom)).astype(BF16)
            if shared_kv:
                o_all = lax.dot_general(p_t, window(vc_ref, vp_ref, u), CONTRACT_FIRST,
                                        preferred_element_type=F32)
            if write_lse:
                lse = m + jnp.log(denom)
            for g, h in enumerate(heads):
                hs = slice(h * HEAD_DIM, (h + 1) * HEAD_DIM)
                gs = slice(g * BLOCK, (g + 1) * BLOCK)
                if shared_kv:
                    o = o_all[gs, :]
                else:
                    o = lax.dot_general(p_t[:, gs], window(vc_ref, vp_ref, h), CONTRACT_FIRST,
                                        preferred_element_type=F32)
                if mul_z:
                    o = o * z_ref[rows, hs]
                o_ref[rows, hs] = o.astype(o_ref.dtype)
                if write_lse:
                    lse_ref[rows, hs] = jnp.broadcast_to(lse[:, gs], (BLOCK, BLOCK)).T


def banded_attention(qkv, *, rows_per_seq, qcol, kcol, vcol, n_q, n_kv, slopes, max_dist,
                     dist_scale, tq, out_dtype, sinks=None, z=None, zcol=0, write_lse=False,
                     name="banded"):
    rows_total, _ = qkv.shape
    n_seq = rows_total // rows_per_seq
    nrt = rows_per_seq // tq
    qw, kw = n_q * HEAD_DIM, n_kv * HEAD_DIM
    assert qcol % qw == 0 and kcol % kw == 0 and vcol % kw == 0
    sub = tq // BLOCK

    def cur(col, w):
        return lambda s, i: (s * nrt + i, col // w)

    def prev(col, w):
        return lambda s, i: (jnp.maximum((s * nrt + i) * sub - 1, 0), col // w)

    in_specs = [
        pl.BlockSpec((tq, qw), cur(qcol, qw)),
        pl.BlockSpec((tq, kw), cur(kcol, kw)),
        pl.BlockSpec((BLOCK, kw), prev(kcol, kw)),
        pl.BlockSpec((tq, kw), cur(vcol, kw)),
        pl.BlockSpec((BLOCK, kw), prev(vcol, kw)),
    ]
    args = [qkv, qkv, qkv, qkv, qkv]
    if sinks is not None:
        in_specs.append(pl.BlockSpec(memory_space=pltpu.SMEM))
        args.append(sinks)
    if z is not None:
        assert zcol % qw == 0
        in_specs.append(pl.BlockSpec((tq, qw), cur(zcol, qw)))
        args.append(z)
    out_block = pl.BlockSpec((tq, qw), cur(0, qw))
    out_sds = jax.ShapeDtypeStruct((rows_total, qw), out_dtype)
    out_shape, out_specs = out_sds, out_block
    if write_lse:
        out_shape = (out_sds, jax.ShapeDtypeStruct((rows_total, qw), F32))
        out_specs = (out_block, out_block)
    kern = functools.partial(
        _banded_kernel, n_q=n_q, n_kv=n_kv, slopes=tuple(slopes), max_dist=max_dist,
        dist_scale=dist_scale, tq=tq, has_sink=sinks is not None, mul_z=z is not None,
        write_lse=write_lse)
    return pl.pallas_call(
        kern, out_shape=out_shape, grid=(n_seq, nrt),
        in_specs=in_specs, out_specs=out_specs,
        compiler_params=_params(2), name=name,
    )(*args)


def _combine_kernel(o0, o1, o2, l0, l1, l2, z_ref, u_ref, o1_sc, o2_sc, l1_sc, l2_sc, *, tm, dils):
    n_chunk = o0.shape[1] // HEAD_DIM
    for src, dst, d in ((o1, o1_sc, dils[0]), (l1, l1_sc, dils[0]),
                        (o2, o2_sc, dils[1]), (l2, l2_sc, dils[1])):
        n = tm // d
        for p in range(d):
            for c in range(n_chunk):
                dst[c, pl.ds(p, n, stride=d), :] = src[p, :, c * HEAD_DIM:(c + 1) * HEAD_DIM]
    for c in range(n_chunk):
        cs = slice(c * HEAD_DIM, (c + 1) * HEAD_DIM)
        a0, a1, a2 = l0[:, cs], l1_sc[c], l2_sc[c]
        m = jnp.maximum(jnp.maximum(a0, a1), a2)
        e0, e1, e2 = jnp.exp(a0 - m), jnp.exp(a1 - m), jnp.exp(a2 - m)
        inv = 1.0 / (e0 + e1 + e2)
        ya = (e0 * inv) * o0[:, cs] + (e1 * inv) * o1_sc[c] + (e2 * inv) * o2_sc[c]
        u_ref[:, cs] = (ya * z_ref[:, cs]).astype(u_ref.dtype)


def combine_a(outs, lses, z, zcol, tm, dils):
    t, w = outs[0].shape
    blk = pl.BlockSpec((tm, w), lambda i: (i, 0))

    def pblk(d):
        return pl.BlockSpec((d, tm // d, w), lambda i: (0, i, 0))

    def p3(a, d):
        return a.reshape(d, t // d, w)

    d1, d2 = dils
    return pl.pallas_call(
        functools.partial(_combine_kernel, tm=tm, dils=dils),
        out_shape=jax.ShapeDtypeStruct((t, w), BF16),
        grid=(t // tm,),
        in_specs=[blk, pblk(d1), pblk(d2), blk, pblk(d1), pblk(d2),
                  pl.BlockSpec((tm, w), lambda i: (i, zcol // w))],
        out_specs=blk,
        scratch_shapes=[pltpu.VMEM((w // HEAD_DIM, tm, HEAD_DIM), F32)] * 4,
        compiler_params=_params(1),
        name="combine_a",
    )(outs[0], p3(outs[1], d1), p3(outs[2], d2), lses[0], p3(lses[1], d1), p3(lses[2], d2), z)


def _diff_kernel(slope_ref, q_ref, k_ref, v_ref, lam_ref, g_ref, z_ref, o_ref,
                 vt_sc, acc_sc, bias_sc, s_sc, *, tq, seq, lam_init):
    h = pl.program_id(1)
    qi = pl.program_id(2)
    slope = slope_ref[h]

    @pl.when(qi == 0)
    def _():
        for cidx in range(seq // tq):
            rows = slice(cidx * tq, (cidx + 1) * tq)
            vt_sc[:, rows] = v_ref[rows, :].astype(F32).T.astype(BF16)

    kk = lax.broadcasted_iota(jnp.int32, (tq, tq), 0)
    r = lax.broadcasted_iota(jnp.int32, (tq, tq), 1)
    bias = (-slope / SCALE) * (r - kk).astype(F32)
    bias_sc[0] = bias
    bias_sc[1] = jnp.where(r >= kk, bias, NEG)
    qs = [q_ref[:, c * HEAD_DIM:(c + 1) * HEAD_DIM] for c in range(2)]

    def shift_of(kj):
        return slope * ((qi - kj) * tq).astype(F32)

    def bias_of(kj):
        return bias_sc[(kj == qi).astype(jnp.int32)]

    def max_step(kj, ms):
        start = pl.multiple_of(kj * tq, tq)
        out = []
        for comp in range(2):
            k = k_ref[pl.ds(start, tq), comp * HEAD_DIM:(comp + 1) * HEAD_DIM]
            a = lax.dot_general(k, qs[comp], CONTRACT_LAST, preferred_element_type=F32) + bias_of(kj)
            s_sc[comp, pl.ds(start, tq), :] = a
            out.append(jnp.maximum(ms[comp], jnp.max(a, axis=0, keepdims=True) * SCALE - shift_of(kj)))
        return tuple(out)

    def exp_step(kj, ls):
        start = pl.multiple_of(kj * tq, tq)
        vt = vt_sc[:, pl.ds(start, tq)]
        out = []
        for comp in range(2):
            rowv = (ms[comp] + shift_of(kj)) * (-LOG2E)
            e = jnp.exp2(s_sc[comp, pl.ds(start, tq), :] * (SCALE * LOG2E) + rowv)
            acc_sc[comp] += jnp.dot(vt, e.astype(BF16), preferred_element_type=F32)
            out.append(ls[comp] + jnp.sum(e, axis=0, keepdims=True))
        return tuple(out)

    def over_blocks(step_fn, carry):
        done = 0
        for u in DIFF_UNROLL:
            n_it = (qi + 1 - done) // u

            def body(t, c, u=u, done=done):
                for uu in range(u):
                    c = step_fn(done + t * u + uu, c)
                return c

            carry = lax.fori_loop(0, n_it, body, carry)
            done = done + n_it * u
        return carry

    neg = jnp.full((1, tq), NEG, F32)
    ms = over_blocks(max_step, (neg, neg))
    acc_sc[...] = jnp.zeros(acc_sc.shape, F32)
    zero = jnp.zeros((1, tq), F32)
    l1, l2 = over_blocks(exp_step, (zero, zero))

    lp = lam_ref[...]
    lam_full = (jnp.exp(jnp.sum(lp[0:1] * lp[1:2], axis=-1, keepdims=True))
                - jnp.exp(jnp.sum(lp[2:3] * lp[3:4], axis=-1, keepdims=True)) + lam_init)
    y_t = acc_sc[0] * (1.0 / l1) - lam_full * (acc_sc[1] * (1.0 / l2))
    ms2 = jnp.mean(y_t * y_t, axis=0, keepdims=True)
    y = (y_t * lax.rsqrt(ms2 + EPS)).T
    y = y * g_ref[...] * (1.0 - lam_init)
    o_ref[...] = (y * z_ref[...]).astype(o_ref.dtype)


def diff_attention(cp, slopes, lam_l, subln_g, z, *, seq, qcol, kcol, vcol, zcol, tq, lam_init):
    t = cp.shape[0]
    n_batch = t // seq
    nq = seq // tq
    hw = 2 * HEAD_DIM
    kern = functools.partial(_diff_kernel, tq=tq, seq=seq, lam_init=lam_init)
    return pl.pallas_call(
        kern,
        out_shape=jax.ShapeDtypeStruct((t, C_OUT), BF16),
        grid=(n_batch, C_HEADS, nq),
        in_specs=[
            pl.BlockSpec(memory_space=pltpu.SMEM),
            pl.BlockSpec((tq, hw), lambda b, h, i: (b * nq + i, qcol // hw + h)),
            pl.BlockSpec((seq, hw), lambda b, h, i: (b, kcol // hw + h)),
            pl.BlockSpec((seq, hw), lambda b, h, i: (b, vcol // hw + h)),
            pl.BlockSpec((4, HEAD_DIM), lambda b, h, i: (0, 0)),
            pl.BlockSpec((1, hw), lambda b, h, i: (0, 0)),
            pl.BlockSpec((tq, hw), lambda b, h, i: (b * nq + i, zcol // hw + h)),
        ],
        out_specs=pl.BlockSpec((tq, hw), lambda b, h, i: (b * nq + i, h)),
        scratch_shapes=[pltpu.VMEM((hw, seq), BF16), pltpu.VMEM((2, hw, tq), F32),
                        pltpu.VMEM((2, tq, tq), F32), pltpu.VMEM((2, seq, tq), F32)],
        compiler_params=_params(3),
        name="diff_attention",
    )(slopes, cp, cp, cp, lam_l, subln_g.reshape(1, hw), z)


def _mem_kernel(q_ref, k_ref, v_ref, z_ref, o_ref):
    for h in range(M_HEADS):
        hs = slice(h * HEAD_DIM, (h + 1) * HEAD_DIM)
        s_t = lax.dot_general(k_ref[:, hs], q_ref[:, hs], CONTRACT_LAST,
                              preferred_element_type=F32) * SCALE
        m = jnp.max(s_t, axis=0, keepdims=True)
        e = jnp.exp(s_t - m)
        p_t = (e * (1.0 / jnp.sum(e, axis=0, keepdims=True))).astype(BF16)
        o = lax.dot_general(p_t, v_ref[:, hs], CONTRACT_FIRST, preferred_element_type=F32)
        o_ref[:, hs] = (o * z_ref[:, hs]).astype(o_ref.dtype)


def mem_attention(cp, mkv, z, *, seq, qcol, zcol, tq):
    t = cp.shape[0]
    nq = seq // tq
    return pl.pallas_call(
        _mem_kernel,
        out_shape=jax.ShapeDtypeStruct((t, M_OUT), BF16),
        grid=(t // tq,),
        in_specs=[
            pl.BlockSpec((tq, M_OUT), lambda i: (i, qcol // M_OUT)),
            pl.BlockSpec((N_MEM, M_OUT), lambda i: (i // nq, 0)),
            pl.BlockSpec((N_MEM, M_OUT), lambda i: (i // nq, 1)),
            pl.BlockSpec((tq, M_OUT), lambda i: (i, zcol // M_OUT)),
        ],
        out_specs=pl.BlockSpec((tq, M_OUT), lambda i: (i, 0)),
        compiler_params=_params(1),
        name="mem_attention",
    )(cp, mkv, mkv, z)


_BRANCH_ROWS = ((0, A_OUT), (A_OUT, A_OUT + B_OUT), (A_OUT + B_OUT, A_OUT + B_OUT + C_OUT),
                (A_OUT + B_OUT + C_OUT, A_OUT + B_OUT + C_OUT + M_OUT))
_WB_CHUNK = 512


def _merge_kernel(*refs):
    h_ref, us = refs[0], refs[1:5]
    wgs, bgs = refs[5:9], refs[9:13]
    n_chunk = (A_OUT + B_OUT + C_OUT + M_OUT) // _WB_CHUNK
    wbs = refs[13:13 + n_chunk]
    o_ref, wg_sc, wb_sc = refs[13 + n_chunk:]

    @pl.when(pl.program_id(1) == 0)
    def _():
        for b in range(N_BRANCH):
            wg_sc[b] = wgs[b][...].astype(BF16)
        for c in range(n_chunk):
            wb_sc[c * _WB_CHUNK:(c + 1) * _WB_CHUNK, :] = wbs[c][...].astype(BF16)

    h = h_ref[...]
    acc = None
    for b in range(N_BRANCH):
        lo, hi = _BRANCH_ROWS[b]
        gate = jax.nn.sigmoid(jnp.dot(h, wg_sc[b], preferred_element_type=F32) + bgs[b][...])
        term = gate * jnp.dot(us[b][...], wb_sc[lo:hi, :], preferred_element_type=F32)
        acc = term if acc is None else acc + term
    o_ref[...] = acc.astype(o_ref.dtype)


def gated_merge(h, us, w_in, b_gate3, w_branch, layer, tm, tn):
    t, d = h.shape
    nj = d // tn
    gate0 = O_GATE // tn
    n_chunk = w_branch.shape[1] // _WB_CHUNK
    in_specs = [pl.BlockSpec((tm, d), lambda j, i: (i, 0))]
    in_specs += [pl.BlockSpec((tm, u.shape[1]), lambda j, i: (i, 0)) for u in us]
    in_specs += [pl.BlockSpec((None, d, tn), functools.partial(
        lambda j, i, b: (layer, 0, gate0 + b * nj + j), b=b)) for b in range(N_BRANCH)]
    in_specs += [pl.BlockSpec((None, 1, tn), functools.partial(
        lambda j, i, b: (layer, 0, b * nj + j), b=b)) for b in range(N_BRANCH)]
    in_specs += [pl.BlockSpec((None, _WB_CHUNK, tn), functools.partial(
        lambda j, i, c: (layer, c, j), c=c)) for c in range(n_chunk)]
    return pl.pallas_call(
        _merge_kernel,
        out_shape=jax.ShapeDtypeStruct((t, d), BF16),
        grid=(nj, t // tm),
        in_specs=in_specs,
        out_specs=pl.BlockSpec((tm, tn), lambda j, i: (i, j)),
        scratch_shapes=[pltpu.VMEM((N_BRANCH, d, tn), BF16),
                        pltpu.VMEM((w_branch.shape[1], tn), BF16)],
        compiler_params=_params(2),
        name="gated_merge",
    )(h, *us, *([w_in] * N_BRANCH), *([b_gate3] * N_BRANCH), *([w_branch] * n_chunk))


def _out_kernel(x_ref, m_ref, w_ref, o_ref, w_sc):
    @pl.when(pl.program_id(1) == 0)
    def _():
        w_sc[...] = w_ref[...].astype(BF16)

    o_ref[...] = x_ref[...] + jnp.dot(m_ref[...], w_sc[...], preferred_element_type=F32)


def out_proj(x, merged, w_out, layer, tm, tn):
    t, d = x.shape
    return pl.pallas_call(
        _out_kernel,
        out_shape=jax.ShapeDtypeStruct((t, d), F32),
        grid=(d // tn, t // tm),
        in_specs=[pl.BlockSpec((tm, tn), lambda j, i: (i, j)),
                  pl.BlockSpec((tm, d), lambda j, i: (i, 0)),
                  pl.BlockSpec((None, d, tn), lambda j, i: (layer, 0, j))],
        out_specs=pl.BlockSpec((tm, tn), lambda j, i: (i, j)),
        scratch_shapes=[pltpu.VMEM((d, tn), BF16)],
        compiler_params=_params(2),
        name="out_proj",
    )(x, merged, w_out)


def kernel(x, mem, norm_g, w_in, b_gate, qk_gain, sinks, lam, subln_g, mem_norm_g,
           w_mem_kv, w_branch, w_out):
    bsz, s_len, d = x.shape
    depth = w_in.shape[0]
    t = bsz * s_len
    xf = x.reshape(t, d)
    memf = mem.reshape(bsz * N_MEM, d)
    b_gate3 = b_gate.reshape(depth, 1, N_BRANCH * d)

    slopes_a = _alibi_slopes(A_HEADS_PER_GROUP)
    slopes_b = _alibi_slopes(B_Q_HEADS)
    slopes_c = jnp.asarray(_alibi_slopes(C_HEADS), F32)
    ones = lambda n: jnp.ones((n,), F32)
    dils = tuple(dil for _, dil in A_PATTERNS)

    tn = PROJ_TN
    z_zb, z_zc, z_za, z_zm = 0, B_OUT, B_OUT + C_OUT, B_OUT + C_OUT + A_OUT
    z_base = O_ZA // tn

    def z_colmap(j):
        return z_base + jnp.where(j < 4, j + 1, jnp.where(j == 4, 0, 5))

    c_q, c_k, c_v, c_mq = 0, C_OUT, 2 * C_OUT, 3 * C_OUT

    for l in range(depth):
        gq = qk_gain[l]
        gain_a = jnp.concatenate([jnp.tile(gq[QK_A_Q], 4), jnp.tile(gq[QK_A_K], 4), ones(A_OUT)]).reshape(1, -1)
        gain_b = jnp.concatenate([jnp.tile(gq[QK_B_Q], B_Q_HEADS), jnp.tile(gq[QK_B_K], B_KV_HEADS),
                                  ones(B_KV_HEADS * HEAD_DIM)]).reshape(1, -1)
        gain_c = jnp.concatenate([jnp.tile(gq[QK_C_Q], 2 * C_HEADS), jnp.tile(gq[QK_C_K], 2 * C_HEADS),
                                  ones(C_OUT), jnp.tile(gq[QK_M_Q], M_HEADS)]).reshape(1, -1)
        gain_z = ones(O_GATE - O_ZA).reshape(1, -1)
        gain_m = jnp.concatenate([jnp.tile(gq[QK_M_K], M_HEADS), ones(M_OUT)]).reshape(1, -1)

        hs = rmsnorm_bf16(xf, norm_g[l], 256, dils[1:])
        h = hs[0]
        h_perm = [h] + [hp.reshape(t, d) for hp in hs[1:]]

        outs, lses = [], []
        for g, (win, dil) in enumerate(A_PATTERNS):
            apg = proj(h_perm[g], w_in, l, functools.partial(lambda j, g: g + A_GROUPS * j, g=g),
                       gain_a, ("NNNN", "NNNN", "PPPP"), 1024, BF16, f"proj_a{g}")
            rows = s_len // dil
            o_g, lse_g = banded_attention(
                apg, rows_per_seq=rows, qcol=0, kcol=A_OUT, vcol=2 * A_OUT,
                n_q=A_HEADS_PER_GROUP, n_kv=A_HEADS_PER_GROUP, slopes=slopes_a,
                max_dist=win // dil, dist_scale=dil, tq=min(512, rows),
                out_dtype=F32, write_lse=True, name=f"attn_a{g}")
            outs.append(o_g)
            lses.append(lse_g)

        bp = proj(h, w_in, l, lambda j: O_BQ // tn + j, gain_b, ("NNNN", "NNNN", "NNPP"), 1024, BF16, "proj_b")
        cp = proj(h, w_in, l, lambda j: O_CQ // tn + j, gain_c,
                  ("NNNN",) * 4 + ("PPPP",) * 2 + ("NNNN",), 1024, BF16, "proj_c")
        sz = proj(h, w_in, l, z_colmap, gain_z, ("SSSS",) * 6, 1024, F32, "proj_z")

        u_a = combine_a(outs, lses, sz, z_za, 512, dils[1:])

        u_b = banded_attention(
            bp, rows_per_seq=s_len, qcol=0, kcol=B_OUT, vcol=B_OUT + B_KV_HEADS * HEAD_DIM,
            n_q=B_Q_HEADS, n_kv=B_KV_HEADS, slopes=slopes_b, max_dist=B_WINDOW - 1, dist_scale=1,
            tq=512, out_dtype=BF16, sinks=sinks[l], z=sz, zcol=z_zb, name="attn_b")

        lam_init = 0.8 - 0.6 * math.exp(-0.3 * l)
        u_c = diff_attention(cp, slopes_c, lam[l], subln_g[l], sz, seq=s_len,
                             qcol=c_q, kcol=c_k, vcol=c_v, zcol=z_zc, tq=256, lam_init=lam_init)

        mn = rmsnorm_bf16(memf, mem_norm_g[l], 256)[0]
        mkv = proj(mn, w_mem_kv, l, lambda j: j, gain_m, ("NNNN", "PPPP"), bsz * N_MEM, BF16, "proj_mem")
        u_m = mem_attention(cp, mkv, sz, seq=s_len, qcol=c_mq, zcol=z_zm, tq=512)

        merged = gated_merge(h, [u_a, u_b, u_c, u_m], w_in, b_gate3, w_branch, l, 512, 256)
        xf = out_proj(xf, merged, w_out, l, 1024, 512)

    return xf.reshape(bsz, s_len, d)
```

```python
import functools
import math

import jax
import jax.numpy as jnp
from jax import lax
from jax.experimental import pallas as pl
from jax.experimental.pallas import tpu as pltpu

D_MODEL = 2048
HEAD_DIM = 128
BLOCK = 128
EPS = 1e-6
A_PATTERNS = ((128, 1), (512, 4), (2048, 16))
A_GROUPS = 3
A_HEADS_PER_GROUP = 4
A_HEADS = A_GROUPS * A_HEADS_PER_GROUP
A_OUT = A_HEADS_PER_GROUP * HEAD_DIM
B_Q_HEADS = 8
B_KV_HEADS = 2
B_WINDOW = 128
B_OUT = B_Q_HEADS * HEAD_DIM
C_HEADS = 4
C_OUT = C_HEADS * 2 * HEAD_DIM
N_MEM = 256
M_HEADS = 4
M_OUT = M_HEADS * HEAD_DIM
N_BRANCH = 4
QK_A_Q, QK_A_K, QK_B_Q, QK_B_K, QK_C_Q, QK_C_K, QK_M_Q, QK_M_K = range(8)

_IN_SIZES = (A_HEADS * HEAD_DIM, A_HEADS * HEAD_DIM, A_HEADS * HEAD_DIM,
             B_Q_HEADS * HEAD_DIM, B_KV_HEADS * HEAD_DIM, B_KV_HEADS * HEAD_DIM,
             C_HEADS * 2 * HEAD_DIM, C_HEADS * 2 * HEAD_DIM, C_OUT,
             M_HEADS * HEAD_DIM, A_OUT, B_OUT, C_OUT, M_OUT, N_BRANCH * D_MODEL)
_OFF = [0]
for _s in _IN_SIZES:
    _OFF.append(_OFF[-1] + _s)
(O_AQ, O_AK, O_AV, O_BQ, O_BK, O_BV, O_CQ, O_CK, O_CV, O_MQ,
 O_ZA, O_ZB, O_ZC, O_ZM, O_GATE, O_END) = _OFF

NEG = -1e30
SCALE = HEAD_DIM ** -0.5
VMEM_LIMIT = 58 * 1024 * 1024
PROJ_TN = 512
DIFF_UNROLL = (8, 4, 2, 1)
LOG2E = math.log2(math.e)
BF16 = jnp.bfloat16
F32 = jnp.float32
CONTRACT_LAST = (((1,), (1,)), ((), ()))
CONTRACT_FIRST = (((0,), (0,)), ((), ()))


def _params(n_axes):
    return pltpu.CompilerParams(dimension_semantics=("arbitrary",) * n_axes,
                                vmem_limit_bytes=VMEM_LIMIT)


def _alibi_slopes(n):
    return [2.0 ** (-8.0 * i / n) for i in range(1, n + 1)]


def _rmsnorm_kernel(x_ref, g_ref, o_ref, *perm_refs, dils, tm):
    x = x_ref[...]
    ms = jnp.mean(x * x, axis=-1, keepdims=True)
    h = (x * lax.rsqrt(ms + EPS) * g_ref[...]).astype(BF16)
    o_ref[...] = h
    row = lax.broadcasted_iota(jnp.int32, (tm, tm), 0)
    col = lax.broadcasted_iota(jnp.int32, (tm, tm), 1)
    for d, p_ref in zip(dils, perm_refs):
        n = tm // d
        src = (row % n) * d + row // n
        perm = jnp.where(col == src, 1.0, 0.0).astype(BF16)
        hp = jnp.dot(perm, h, preferred_element_type=F32).astype(BF16)
        for p in range(d):
            p_ref[p] = hp[p * n:(p + 1) * n, :]


def rmsnorm_bf16(x, g, tm, dils=()):
    t, d = x.shape
    out_shape = [jax.ShapeDtypeStruct((t, d), BF16)]
    out_specs = [pl.BlockSpec((tm, d), lambda i: (i, 0))]
    for dil in dils:
        out_shape.append(jax.ShapeDtypeStruct((dil, t // dil, d), BF16))
        out_specs.append(pl.BlockSpec((dil, tm // dil, d), lambda i: (0, i, 0)))
    outs = pl.pallas_call(
        functools.partial(_rmsnorm_kernel, dils=tuple(dils), tm=tm),
        out_shape=out_shape,
        grid=(t // tm,),
        in_specs=[pl.BlockSpec((tm, d), lambda i: (i, 0)),
                  pl.BlockSpec((1, d), lambda i: (0, 0))],
        out_specs=out_specs,
        compiler_params=_params(1),
        name="rmsnorm",
    )(x, g.reshape(1, d))
    return outs


def _mode_ranges(tile_modes):
    ranges, start = [], 0
    for j in range(1, len(tile_modes) + 1):
        if j == len(tile_modes) or tile_modes[j] != tile_modes[start]:
            ranges.append((tile_modes[start], start, j))
            start = j
    return ranges


def _proj_kernel(h_ref, w_ref, gain_ref, o_ref, w_sc, *, tile_modes):
    j = pl.program_id(0)

    @pl.when(pl.program_id(1) == 0)
    def _():
        w_sc[...] = w_ref[...].astype(BF16)

    acc = jnp.dot(h_ref[...], w_sc[...], preferred_element_type=F32)
    for c in range(len(tile_modes[0])):
        sl = slice(c * HEAD_DIM, (c + 1) * HEAD_DIM)
        modes = [m[c] for m in tile_modes]
        y = acc[:, sl]
        if 'S' in modes:
            assert set(modes) == {'S'}
            y = y * jax.nn.sigmoid(y)
        elif 'N' in modes:
            ms = jnp.mean(y * y, axis=-1, keepdims=True)
            normed = y * lax.rsqrt(ms + EPS) * gain_ref[:, sl]
            if set(modes) == {'N'}:
                y = normed
            else:
                is_norm = functools.reduce(jnp.logical_or, [j == jj for jj, m in enumerate(modes) if m == 'N'])
                y = jnp.where(is_norm, normed, y)
        o_ref[:, sl] = y.astype(o_ref.dtype)


def proj(h, w3, layer, colmap, gain, tile_modes, tm, out_dtype, name):
    t, k = h.shape
    tn = PROJ_TN
    n = tn * len(tile_modes)
    assert t % tm == 0 and all(len(m) == tn // HEAD_DIM for m in tile_modes)
    return pl.pallas_call(
        functools.partial(_proj_kernel, tile_modes=tuple(tile_modes)),
        out_shape=jax.ShapeDtypeStruct((t, n), out_dtype),
        grid=(n // tn, t // tm),
        in_specs=[pl.BlockSpec((tm, k), lambda j, i: (i, 0)),
                  pl.BlockSpec((None, k, tn), lambda j, i: (layer, 0, colmap(j))),
                  pl.BlockSpec((1, tn), lambda j, i: (0, j))],
        out_specs=pl.BlockSpec((tm, tn), lambda j, i: (i, j)),
        scratch_shapes=[pltpu.VMEM((k, tn), BF16)],
        compiler_params=_params(2),
        name=name,
    )(h, w3, gain)


def _banded_kernel(*refs, n_q, n_kv, slopes, max_dist, dist_scale, tq,
                   has_sink, mul_z, write_lse):
    refs = list(refs)
    q_ref, kc_ref, kp_ref, vc_ref, vp_ref = refs[:5]
    pos = 5
    sink_ref = z_ref = lse_ref = None
    if has_sink:
        sink_ref = refs[pos]; pos += 1
    if mul_z:
        z_ref = refs[pos]; pos += 1
    o_ref = refs[pos]; pos += 1
    if write_lse:
        lse_ref = refs[pos]; pos += 1

    shared_kv = n_kv < n_q
    unit = 4
    assert n_q % unit == 0 and (not shared_kv or n_q // n_kv == unit)
    kk = lax.broadcasted_iota(jnp.int32, (2 * BLOCK, BLOCK), 0)
    r = lax.broadcasted_iota(jnp.int32, (2 * BLOCK, BLOCK), 1)
    dist = r + BLOCK - kk
    distf = (dist * dist_scale).astype(F32)
    valid = (dist >= 0) & (dist <= max_dist)
    has_prev = jnp.minimum(pl.program_id(1), 1) * BLOCK
    valid0 = valid & (kk + has_prev >= BLOCK)

    for u in range(n_q // unit):
        heads = list(range(u * unit, (u + 1) * unit))
        bias = jnp.concatenate([jnp.where(valid, -slopes[h] * distf, NEG) for h in heads], axis=1)
        bias0 = jnp.concatenate([jnp.where(valid0, -slopes[h] * distf, NEG) for h in heads], axis=1)
        if has_sink:
            sink = jnp.concatenate([jnp.full((1, BLOCK), sink_ref[h], F32) for h in heads], axis=1)
        for j in range(tq // BLOCK):
            rows = slice(j * BLOCK, (j + 1) * BLOCK)
            prows = slice((j - 1) * BLOCK, j * BLOCK)

            def window(cur_ref, prev_ref, kv):
                ks = slice(kv * HEAD_DIM, (kv + 1) * HEAD_DIM)
                prev = prev_ref[:, ks] if j == 0 else cur_ref[prows, ks]
                return jnp.concatenate([prev, cur_ref[rows, ks]], axis=0)

            qs = [q_ref[rows, h * HEAD_DIM:(h + 1) * HEAD_DIM] for h in heads]
            if shared_kv:
                kwin = window(kc_ref, kp_ref, u)
                s_t = lax.dot_general(kwin, jnp.concatenate(qs, axis=0), CONTRACT_LAST,
                                      preferred_element_type=F32)
            else:
                s_t = jnp.concatenate(
                    [lax.dot_general(window(kc_ref, kp_ref, h), q, CONTRACT_LAST,
                                     preferred_element_type=F32) for h, q in zip(heads, qs)], axis=1)
            s_t = s_t * SCALE + (bias0 if j == 0 else bias)
            m = jnp.max(s_t, axis=0, keepdims=True)
            if has_sink:
                m = jnp.maximum(m, sink)
            e = jnp.exp(s_t - m)
            denom = jnp.sum(e, axis=0, keepdims=True)
            if has_sink:
                denom = denom + jnp.exp(sink - m)
            p_t = (e * (1.0 / denom)).astype(BF16)
            if shared_kv:
                o_all = lax.dot_general(p_t, window(vc_ref, vp_ref, u), CONTRACT_FIRST,
                                        preferred_element_type=F32)
            if write_lse:
                lse = m + jnp.log(denom)
            for g, h in enumerate(heads):
                hs = slice(h * HEAD_DIM, (h + 1) * HEAD_DIM)
                gs = slice(g * BLOCK, (g + 1) * BLOCK)
                if shared_kv:
                    o = o_all[gs, :]
                else:
                    o = lax.dot_general(p_t[:, gs], window(vc_ref, vp_ref, h), CONTRACT_FIRST,
                                        preferred_element_type=F32)
                if mul_z:
                    o = o * z_ref[rows, hs]
                o_ref[rows, hs] = o.astype(o_ref.dtype)
                if write_lse:
                    lse_ref[rows, hs] = jnp.broadcast_to(lse[:, gs], (BLOCK, BLOCK)).T


def banded_attention(qkv, *, rows_per_seq, qcol, kcol, vcol, n_q, n_kv, slopes, max_dist,
                     dist_scale, tq, out_dtype, sinks=None, z=None, zcol=0, write_lse=False,
                     name="banded"):
    rows_total, _ = qkv.shape
    n_seq = rows_total // rows_per_seq
    nrt = rows_per_seq // tq
    qw, kw = n_q * HEAD_DIM, n_kv * HEAD_DIM
    assert qcol % qw == 0 and kcol % kw == 0 and vcol % kw == 0
    sub = tq // BLOCK

    def cur(col, w):
        return lambda s, i: (s * nrt + i, col // w)

    def prev(col, w):
        return lambda s, i: (jnp.maximum((s * nrt + i) * sub - 1, 0), col // w)

    in_specs = [
        pl.BlockSpec((tq, qw), cur(qcol, qw)),
        pl.BlockSpec((tq, kw), cur(kcol, kw)),
        pl.BlockSpec((BLOCK, kw), prev(kcol, kw)),
        pl.BlockSpec((tq, kw), cur(vcol, kw)),
        pl.BlockSpec((BLOCK, kw), prev(vcol, kw)),
    ]
    args = [qkv, qkv, qkv, qkv, qkv]
    if sinks is not None:
        in_specs.append(pl.BlockSpec(memory_space=pltpu.SMEM))
        args.append(sinks)
    if z is not None:
        assert zcol % qw == 0
        in_specs.append(pl.BlockSpec((tq, qw), cur(zcol, qw)))
        args.append(z)
    out_block = pl.BlockSpec((tq, qw), cur(0, qw))
    out_sds = jax.ShapeDtypeStruct((rows_total, qw), out_dtype)
    out_shape, out_specs = out_sds, out_block
    if write_lse:
        out_shape = (out_sds, jax.ShapeDtypeStruct((rows_total, qw), F32))
        out_specs = (out_block, out_block)
    kern = functools.partial(
        _banded_kernel, n_q=n_q, n_kv=n_kv, slopes=tuple(slopes), max_dist=max_dist,
        dist_scale=dist_scale, tq=tq, has_sink=sinks is not None, mul_z=z is not None,
        write_lse=write_lse)
    return pl.pallas_call(
        kern, out_shape=out_shape, grid=(n_seq, nrt),
        in_specs=in_specs, out_specs=out_specs,
        compiler_params=_params(2), name=name,
    )(*args)


def _combine_kernel(o0, o1, o2, l0, l1, l2, z_ref, u_ref, o1_sc, o2_sc, l1_sc, l2_sc, *, tm, dils):
    n_chunk = o0.shape[1] // HEAD_DIM
    for src, dst, d in ((o1, o1_sc, dils[0]), (l1, l1_sc, dils[0]),
                        (o2, o2_sc, dils[1]), (l2, l2_sc, dils[1])):
        n = tm // d
        for p in range(d):
            for c in range(n_chunk):
                dst[c, pl.ds(p, n, stride=d), :] = src[p, :, c * HEAD_DIM:(c + 1) * HEAD_DIM]
    for c in range(n_chunk):
        cs = slice(c * HEAD_DIM, (c + 1) * HEAD_DIM)
        a0, a1, a2 = l0[:, cs], l1_sc[c], l2_sc[c]
        m = jnp.maximum(jnp.maximum(a0, a1), a2)
        e0, e1, e2 = jnp.exp(a0 - m), jnp.exp(a1 - m), jnp.exp(a2 - m)
        inv = 1.0 / (e0 + e1 + e2)
        ya = (e0 * inv) * o0[:, cs] + (e1 * inv) * o1_sc[c] + (e2 * inv) * o2_sc[c]
        u_ref[:, cs] = (ya * z_ref[:, cs]).astype(u_ref.dtype)


def combine_a(outs, lses, z, zcol, tm, dils):
    t, w = outs[0].shape
    blk = pl.BlockSpec((tm, w), lambda i: (i, 0))

    def pblk(d):
        return pl.BlockSpec((d, tm // d, w), lambda i: (0, i, 0))

    def p3(a, d):
        return a.reshape(d, t // d, w)

    d1, d2 = dils
    return pl.pallas_call(
        functools.partial(_combine_kernel, tm=tm, dils=dils),
        out_shape=jax.ShapeDtypeStruct((t, w), BF16),
        grid=(t // tm,),
        in_specs=[blk, pblk(d1), pblk(d2), blk, pblk(d1), pblk(d2),
                  pl.BlockSpec((tm, w), lambda i: (i, zcol // w))],
        out_specs=blk,
        scratch_shapes=[pltpu.VMEM((w // HEAD_DIM, tm, HEAD_DIM), F32)] * 4,
        compiler_params=_params(1),
        name="combine_a",
    )(outs[0], p3(outs[1], d1), p3(outs[2], d2), lses[0], p3(lses[1], d1), p3(lses[2], d2), z)


def _diff_kernel(slope_ref, q_ref, k_ref, v_ref, lam_ref, g_ref, z_ref, o_ref,
                 vt_sc, acc_sc, bias_sc, s_sc, *, tq, seq, lam_init):
    h = pl.program_id(1)
    qi = pl.program_id(2)
    slope = slope_ref[h]

    @pl.when(qi == 0)
    def _():
        for cidx in range(seq // tq):
            rows = slice(cidx * tq, (cidx + 1) * tq)
            vt_sc[:, rows] = v_ref[rows, :].astype(F32).T.astype(BF16)

    kk = lax.broadcasted_iota(jnp.int32, (tq, tq), 0)
    r = lax.broadcasted_iota(jnp.int32, (tq, tq), 1)
    bias = (-slope / SCALE) * (r - kk).astype(F32)
    bias_sc[0] = bias
    bias_sc[1] = jnp.where(r >= kk, bias, NEG)
    qs = [q_ref[:, c * HEAD_DIM:(c + 1) * HEAD_DIM] for c in range(2)]
    to_log2 = SCALE * LOG2E

    def shift_of(kj):
        return (slope * LOG2E) * ((qi - kj) * tq).astype(F32)

    def max_chunk(kj0, n, ms):
        ms = list(ms)
        for u in range(n):
            kj = kj0 + u
            start = pl.multiple_of(kj * tq, tq)
            b = bias_sc[(kj == qi).astype(jnp.int32)]
            for comp in range(2):
                k = k_ref[pl.ds(start, tq), comp * HEAD_DIM:(comp + 1) * HEAD_DIM]
                a = (lax.dot_general(k, qs[comp], CONTRACT_LAST, preferred_element_type=F32) + b) * to_log2
                s_sc[comp, pl.ds(start, tq), :] = a
                ms[comp] = jnp.maximum(ms[comp], jnp.max(a, axis=0, keepdims=True) - shift_of(kj))
        return tuple(ms)

    def exp_chunk(kj0, n, ls):
        ls = list(ls)
        start0 = pl.multiple_of(kj0 * tq, tq)
        vt = vt_sc[:, pl.ds(start0, n * tq)]
        for comp in range(2):
            es = []
            for u in range(n):
                start = pl.multiple_of((kj0 + u) * tq, tq)
                e = jnp.exp2(s_sc[comp, pl.ds(start, tq), :] - (ms[comp] + shift_of(kj0 + u)))
                ls[comp] = ls[comp] + jnp.sum(e, axis=0, keepdims=True)
                es.append(e.astype(BF16))
            e_cat = es[0] if n == 1 else jnp.concatenate(es, axis=0)
            acc_sc[comp] += jnp.dot(vt, e_cat, preferred_element_type=F32)
        return tuple(ls)

    def over_blocks(chunk_fn, carry):
        done = 0
        for n in DIFF_UNROLL:
            n_it = (qi + 1 - done) // n
            carry = lax.fori_loop(
                0, n_it, lambda t, c, n=n, done=done: chunk_fn(done + t * n, n, c), carry)
            done = done + n_it * n
        return carry

    neg = jnp.full((1, tq), NEG, F32)
    ms = over_blocks(max_chunk, (neg, neg))
    acc_sc[...] = jnp.zeros(acc_sc.shape, F32)
    zero = jnp.zeros((1, tq), F32)
    l1, l2 = over_blocks(exp_chunk, (zero, zero))

    lp = lam_ref[...]
    lam_full = (jnp.exp(jnp.sum(lp[0:1] * lp[1:2], axis=-1, keepdims=True))
                - jnp.exp(jnp.sum(lp[2:3] * lp[3:4], axis=-1, keepdims=True)) + lam_init)
    y_t = acc_sc[0] * (1.0 / l1) - lam_full * (acc_sc[1] * (1.0 / l2))
    ms2 = jnp.mean(y_t * y_t, axis=0, keepdims=True)
    y = (y_t * lax.rsqrt(ms2 + EPS)).T
    y = y * g_ref[...] * (1.0 - lam_init)
    o_ref[...] = (y * z_ref[...]).astype(o_ref.dtype)


def diff_attention(cp, slopes, lam_l, subln_g, z, *, seq, qcol, kcol, vcol, zcol, tq, lam_init):
    t = cp.shape[0]
    n_batch = t // seq
    nq = seq // tq
    hw = 2 * HEAD_DIM
    kern = functools.partial(_diff_kernel, tq=tq, seq=seq, lam_init=lam_init)
    return pl.pallas_call(
        kern,
        out_shape=jax.ShapeDtypeStruct((t, C_OUT), BF16),
        grid=(n_batch, C_HEADS, nq),
        in_specs=[
            pl.BlockSpec(memory_space=pltpu.SMEM),
            pl.BlockSpec((tq, hw), lambda b, h, i: (b * nq + i, qcol // hw + h)),
            pl.BlockSpec((seq, hw), lambda b, h, i: (b, kcol // hw + h)),
            pl.BlockSpec((seq, hw), lambda b, h, i: (b, vcol // hw + h)),
            pl.BlockSpec((4, HEAD_DIM), lambda b, h, i: (0, 0)),
            pl.BlockSpec((1, hw), lambda b, h, i: (0, 0)),
            pl.BlockSpec((tq, hw), lambda b, h, i: (b * nq + i, zcol // hw + h)),
        ],
        out_specs=pl.BlockSpec((tq, hw), lambda b, h, i: (b * nq + i, h)),
        scratch_shapes=[pltpu.VMEM((hw, seq), BF16), pltpu.VMEM((2, hw, tq), F32),
                        pltpu.VMEM((2, tq, tq), F32), pltpu.VMEM((2, seq, tq), F32)],
        compiler_params=_params(3),
        name="diff_attention",
    )(slopes, cp, cp, cp, lam_l, subln_g.reshape(1, hw), z)


def _mem_kernel(q_ref, k_ref, v_ref, z_ref, o_ref):
    for h in range(M_HEADS):
        hs = slice(h * HEAD_DIM, (h + 1) * HEAD_DIM)
        s_t = lax.dot_general(k_ref[:, hs], q_ref[:, hs], CONTRACT_LAST,
                              preferred_element_type=F32) * SCALE
        m = jnp.max(s_t, axis=0, keepdims=True)
        e = jnp.exp(s_t - m)
        p_t = (e * (1.0 / jnp.sum(e, axis=0, keepdims=True))).astype(BF16)
        o = lax.dot_general(p_t, v_ref[:, hs], CONTRACT_FIRST, preferred_element_type=F32)
        o_ref[:, hs] = (o * z_ref[:, hs]).astype(o_ref.dtype)


def mem_attention(cp, mkv, z, *, seq, qcol, zcol, tq):
    t = cp.shape[0]
    nq = seq // tq
    return pl.pallas_call(
        _mem_kernel,
        out_shape=jax.ShapeDtypeStruct((t, M_OUT), BF16),
        grid=(t // tq,),
        in_specs=[
            pl.BlockSpec((tq, M_OUT), lambda i: (i, qcol // M_OUT)),
            pl.BlockSpec((N_MEM, M_OUT), lambda i: (i // nq, 0)),
            pl.BlockSpec((N_MEM, M_OUT), lambda i: (i // nq, 1)),
            pl.BlockSpec((tq, M_OUT), lambda i: (i, zcol // M_OUT)),
        ],
        out_specs=pl.BlockSpec((tq, M_OUT), lambda i: (i, 0)),
        compiler_params=_params(1),
        name="mem_attention",
    )(cp, mkv, mkv, z)


_BRANCH_ROWS = ((0, A_OUT), (A_OUT, A_OUT + B_OUT), (A_OUT + B_OUT, A_OUT + B_OUT + C_OUT),
                (A_OUT + B_OUT + C_OUT, A_OUT + B_OUT + C_OUT + M_OUT))
_WB_CHUNK = 512


def _merge_kernel(*refs):
    h_ref, us = refs[0], refs[1:5]
    wgs, bgs = refs[5:9], refs[9:13]
    n_chunk = (A_OUT + B_OUT + C_OUT + M_OUT) // _WB_CHUNK
    wbs = refs[13:13 + n_chunk]
    o_ref, wg_sc, wb_sc = refs[13 + n_chunk:]

    @pl.when(pl.program_id(1) == 0)
    def _():
        for b in range(N_BRANCH):
            wg_sc[b] = wgs[b][...].astype(BF16)
        for c in range(n_chunk):
            wb_sc[c * _WB_CHUNK:(c + 1) * _WB_CHUNK, :] = wbs[c][...].astype(BF16)

    h = h_ref[...]
    acc = None
    for b in range(N_BRANCH):
        lo, hi = _BRANCH_ROWS[b]
        gate = jax.nn.sigmoid(jnp.dot(h, wg_sc[b], preferred_element_type=F32) + bgs[b][...])
        term = gate * jnp.dot(us[b][...], wb_sc[lo:hi, :], preferred_element_type=F32)
        acc = term if acc is None else acc + term
    o_ref[...] = acc.astype(o_ref.dtype)


def gated_merge(h, us, w_in, b_gate3, w_branch, layer, tm, tn):
    t, d = h.shape
    nj = d // tn
    gate0 = O_GATE // tn
    n_chunk = w_branch.shape[1] // _WB_CHUNK
    in_specs = [pl.BlockSpec((tm, d), lambda j, i: (i, 0))]
    in_specs += [pl.BlockSpec((tm, u.shape[1]), lambda j, i: (i, 0)) for u in us]
    in_specs += [pl.BlockSpec((None, d, tn), functools.partial(
        lambda j, i, b: (layer, 0, gate0 + b * nj + j), b=b)) for b in range(N_BRANCH)]
    in_specs += [pl.BlockSpec((None, 1, tn), functools.partial(
        lambda j, i, b: (layer, 0, b * nj + j), b=b)) for b in range(N_BRANCH)]
    in_specs += [pl.BlockSpec((None, _WB_CHUNK, tn), functools.partial(
        lambda j, i, c: (layer, c, j), c=c)) for c in range(n_chunk)]
    return pl.pallas_call(
        _merge_kernel,
        out_shape=jax.ShapeDtypeStruct((t, d), BF16),
        grid=(nj, t // tm),
        in_specs=in_specs,
        out_specs=pl.BlockSpec((tm, tn), lambda j, i: (i, j)),
        scratch_shapes=[pltpu.VMEM((N_BRANCH, d, tn), BF16),
                        pltpu.VMEM((w_branch.shape[1], tn), BF16)],
        compiler_params=_params(2),
        name="gated_merge",
    )(h, *us, *([w_in] * N_BRANCH), *([b_gate3] * N_BRANCH), *([w_branch] * n_chunk))


def _out_kernel(x_ref, m_ref, w_ref, o_ref, w_sc):
    @pl.when(pl.program_id(1) == 0)
    def _():
        w_sc[...] = w_ref[...].astype(BF16)

    o_ref[...] = x_ref[...] + jnp.dot(m_ref[...], w_sc[...], preferred_element_type=F32)


def out_proj(x, merged, w_out, layer, tm, tn):
    t, d = x.shape
    return pl.pallas_call(
        _out_kernel,
        out_shape=jax.ShapeDtypeStruct((t, d), F32),
        grid=(d // tn, t // tm),
        in_specs=[pl.BlockSpec((tm, tn), lambda j, i: (i, j)),
                  pl.BlockSpec((tm, d), lambda j, i: (i, 0)),
                  pl.BlockSpec((None, d, tn), lambda j, i: (layer, 0, j))],
        out_specs=pl.BlockSpec((tm, tn), lambda j, i: (i, j)),
        scratch_shapes=[pltpu.VMEM((d, tn), BF16)],
        compiler_params=_params(2),
        name="out_proj",
    )(x, merged, w_out)


def kernel(x, mem, norm_g, w_in, b_gate, qk_gain, sinks, lam, subln_g, mem_norm_g,
           w_mem_kv, w_branch, w_out):
    bsz, s_len, d = x.shape
    depth = w_in.shape[0]
    t = bsz * s_len
    xf = x.reshape(t, d)
    memf = mem.reshape(bsz * N_MEM, d)
    b_gate3 = b_gate.reshape(depth, 1, N_BRANCH * d)

    slopes_a = _alibi_slopes(A_HEADS_PER_GROUP)
    slopes_b = _alibi_slopes(B_Q_HEADS)
    slopes_c = jnp.asarray(_alibi_slopes(C_HEADS), F32)
    ones = lambda n: jnp.ones((n,), F32)
    dils = tuple(dil for _, dil in A_PATTERNS)

    tn = PROJ_TN
    z_zb, z_zc, z_za, z_zm = 0, B_OUT, B_OUT + C_OUT, B_OUT + C_OUT + A_OUT
    z_base = O_ZA // tn

    def z_colmap(j):
        return z_base + jnp.where(j < 4, j + 1, jnp.where(j == 4, 0, 5))

    c_q, c_k, c_v, c_mq = 0, C_OUT, 2 * C_OUT, 3 * C_OUT

    for l in range(depth):
        gq = qk_gain[l]
        gain_a = jnp.concatenate([jnp.tile(gq[QK_A_Q], 4), jnp.tile(gq[QK_A_K], 4), ones(A_OUT)]).reshape(1, -1)
        gain_b = jnp.concatenate([jnp.tile(gq[QK_B_Q], B_Q_HEADS), jnp.tile(gq[QK_B_K], B_KV_HEADS),
                                  ones(B_KV_HEADS * HEAD_DIM)]).reshape(1, -1)
        gain_c = jnp.concatenate([jnp.tile(gq[QK_C_Q], 2 * C_HEADS), jnp.tile(gq[QK_C_K], 2 * C_HEADS),
                                  ones(C_OUT), jnp.tile(gq[QK_M_Q], M_HEADS)]).reshape(1, -1)
        gain_z = ones(O_GATE - O_ZA).reshape(1, -1)
        gain_m = jnp.concatenate([jnp.tile(gq[QK_M_K], M_HEADS), ones(M_OUT)]).reshape(1, -1)

        hs = rmsnorm_bf16(xf, norm_g[l], 256, dils[1:])
        h = hs[0]
        h_perm = [h] + [hp.reshape(t, d) for hp in hs[1:]]

        outs, lses = [], []
        for g, (win, dil) in enumerate(A_PATTERNS):
            apg = proj(h_perm[g], w_in, l, functools.partial(lambda j, g: g + A_GROUPS * j, g=g),
                       gain_a, ("NNNN", "NNNN", "PPPP"), 1024, BF16, f"proj_a{g}")
            rows = s_len // dil
            o_g, lse_g = banded_attention(
                apg, rows_per_seq=rows, qcol=0, kcol=A_OUT, vcol=2 * A_OUT,
                n_q=A_HEADS_PER_GROUP, n_kv=A_HEADS_PER_GROUP, slopes=slopes_a,
                max_dist=win // dil, dist_scale=dil, tq=min(512, rows),
                out_dtype=F32, write_lse=True, name=f"attn_a{g}")
            outs.append(o_g)
            lses.append(lse_g)

        bp = proj(h, w_in, l, lambda j: O_BQ // tn + j, gain_b, ("NNNN", "NNNN", "NNPP"), 1024, BF16, "proj_b")
        cp = proj(h, w_in, l, lambda j: O_CQ // tn + j, gain_c,
                  ("NNNN",) * 4 + ("PPPP",) * 2 + ("NNNN",), 1024, BF16, "proj_c")
        sz = proj(h, w_in, l, z_colmap, gain_z, ("SSSS",) * 6, 1024, F32, "proj_z")

        u_a = combine_a(outs, lses, sz, z_za, 512, dils[1:])

        u_b = banded_attention(
            bp, rows_per_seq=s_len, qcol=0, kcol=B_OUT, vcol=B_OUT + B_KV_HEADS * HEAD_DIM,
            n_q=B_Q_HEADS, n_kv=B_KV_HEADS, slopes=slopes_b, max_dist=B_WINDOW - 1, dist_scale=1,
            tq=512, out_dtype=BF16, sinks=sinks[l], z=sz, zcol=z_zb, name="attn_b")

        lam_init = 0.8 - 0.6 * math.exp(-0.3 * l)
        u_c = diff_attention(cp, slopes_c, lam[l], subln_g[l], sz, seq=s_len,
                             qcol=c_q, kcol=c_k, vcol=c_v, zcol=z_zc, tq=256, lam_init=lam_init)

        mn = rmsnorm_bf16(memf, mem_norm_g[l], 256)[0]
        mkv = proj(mn, w_mem_kv, l, lambda j: j, gain_m, ("NNNN", "PPPP"), bsz * N_MEM, BF16, "proj_mem")
        u_m = mem_attention(cp, mkv, sz, seq=s_len, qcol=c_mq, zcol=z_zm, tq=512)

        merged = gated_merge(h, [u_a, u_b, u_c, u_m], w_in, b_gate3, w_branch, l, 1024, 256)
        xf = out_proj(xf, merged, w_out, l, 1024, 1024)

    return xf.reshape(bsz, s_len, d)
```

```python
import functools
import math

import jax
import jax.numpy as jnp
from jax import lax
from jax.experimental import pallas as pl
from jax.experimental.pallas import tpu as pltpu

D_MODEL = 2048
HEAD_DIM = 128
BLOCK = 128
EPS = 1e-6
A_PATTERNS = ((128, 1), (512, 4), (2048, 16))
A_GROUPS = 3
A_HEADS_PER_GROUP = 4
A_HEADS = A_GROUPS * A_HEADS_PER_GROUP
A_OUT = A_HEADS_PER_GROUP * HEAD_DIM
B_Q_HEADS = 8
B_KV_HEADS = 2
B_WINDOW = 128
B_OUT = B_Q_HEADS * HEAD_DIM
C_HEADS = 4
C_OUT = C_HEADS * 2 * HEAD_DIM
N_MEM = 256
M_HEADS = 4
M_OUT = M_HEADS * HEAD_DIM
N_BRANCH = 4
QK_A_Q, QK_A_K, QK_B_Q, QK_B_K, QK_C_Q, QK_C_K, QK_M_Q, QK_M_K = range(8)

_IN_SIZES = (A_HEADS * HEAD_DIM, A_HEADS * HEAD_DIM, A_HEADS * HEAD_DIM,
             B_Q_HEADS * HEAD_DIM, B_KV_HEADS * HEAD_DIM, B_KV_HEADS * HEAD_DIM,
             C_HEADS * 2 * HEAD_DIM, C_HEADS * 2 * HEAD_DIM, C_OUT,
             M_HEADS * HEAD_DIM, A_OUT, B_OUT, C_OUT, M_OUT, N_BRANCH * D_MODEL)
_OFF = [0]
for _s in _IN_SIZES:
    _OFF.append(_OFF[-1] + _s)
(O_AQ, O_AK, O_AV, O_BQ, O_BK, O_BV, O_CQ, O_CK, O_CV, O_MQ,
 O_ZA, O_ZB, O_ZC, O_ZM, O_GATE, O_END) = _OFF

NEG = -1e30
SCALE = HEAD_DIM ** -0.5
VMEM_LIMIT = 58 * 1024 * 1024
PROJ_PIECE = 512
DIFF_UNROLL = (8, 4, 2, 1)
LOG2E = math.log2(math.e)
BF16 = jnp.bfloat16
F32 = jnp.float32
CONTRACT_LAST = (((1,), (1,)), ((), ()))
CONTRACT_FIRST = (((0,), (0,)), ((), ()))


def _params(n_axes):
    return pltpu.CompilerParams(dimension_semantics=("arbitrary",) * n_axes,
                                vmem_limit_bytes=VMEM_LIMIT)


def _alibi_slopes(n):
    return [2.0 ** (-8.0 * i / n) for i in range(1, n + 1)]


def _rmsnorm_kernel(x_ref, g_ref, o_ref, *perm_refs, dils, tm):
    x = x_ref[...]
    ms = jnp.mean(x * x, axis=-1, keepdims=True)
    h = (x * lax.rsqrt(ms + EPS) * g_ref[...]).astype(BF16)
    o_ref[...] = h
    row = lax.broadcasted_iota(jnp.int32, (tm, tm), 0)
    col = lax.broadcasted_iota(jnp.int32, (tm, tm), 1)
    for d, p_ref in zip(dils, perm_refs):
        n = tm // d
        src = (row % n) * d + row // n
        perm = jnp.where(col == src, 1.0, 0.0).astype(BF16)
        hp = jnp.dot(perm, h, preferred_element_type=F32).astype(BF16)
        for p in range(d):
            p_ref[p] = hp[p * n:(p + 1) * n, :]


def rmsnorm_bf16(x, g, tm, dils=()):
    t, d = x.shape
    out_shape = [jax.ShapeDtypeStruct((t, d), BF16)]
    out_specs = [pl.BlockSpec((tm, d), lambda i: (i, 0))]
    for dil in dils:
        out_shape.append(jax.ShapeDtypeStruct((dil, t // dil, d), BF16))
        out_specs.append(pl.BlockSpec((dil, tm // dil, d), lambda i: (0, i, 0)))
    outs = pl.pallas_call(
        functools.partial(_rmsnorm_kernel, dils=tuple(dils), tm=tm),
        out_shape=out_shape,
        grid=(t // tm,),
        in_specs=[pl.BlockSpec((tm, d), lambda i: (i, 0)),
                  pl.BlockSpec((1, d), lambda i: (0, 0))],
        out_specs=out_specs,
        compiler_params=_params(1),
        name="rmsnorm",
    )(x, g.reshape(1, d))
    return outs


def _proj_kernel(*refs, tile_modes, n_piece):
    h_ref, w_refs = refs[0], refs[1:1 + n_piece]
    gain_ref, o_ref, w_sc = refs[1 + n_piece:]
    j = pl.program_id(0)

    @pl.when(pl.program_id(1) == 0)
    def _():
        for p in range(n_piece):
            w_sc[:, p * PROJ_PIECE:(p + 1) * PROJ_PIECE] = w_refs[p][...].astype(BF16)

    acc = jnp.dot(h_ref[...], w_sc[...], preferred_element_type=F32)
    for c in range(len(tile_modes[0])):
        sl = slice(c * HEAD_DIM, (c + 1) * HEAD_DIM)
        modes = [m[c] for m in tile_modes]
        y = acc[:, sl]
        if 'S' in modes:
            assert set(modes) == {'S'}
            y = y * jax.nn.sigmoid(y)
        elif 'N' in modes:
            ms = jnp.mean(y * y, axis=-1, keepdims=True)
            normed = y * lax.rsqrt(ms + EPS) * gain_ref[:, sl]
            if set(modes) == {'N'}:
                y = normed
            else:
                is_norm = functools.reduce(jnp.logical_or, [j == jj for jj, m in enumerate(modes) if m == 'N'])
                y = jnp.where(is_norm, normed, y)
        o_ref[:, sl] = y.astype(o_ref.dtype)


def _select(j, values):
    out = values[-1]
    for jj in range(len(values) - 2, -1, -1):
        out = jnp.where(j == jj, values[jj], out)
    return out


def proj(h, w3, layer, piece_cols, gain, tile_modes, tm, out_dtype, name):
    t, k = h.shape
    n_piece = len(piece_cols)
    tn = n_piece * PROJ_PIECE
    n_tiles = len(tile_modes)
    assert t % tm == 0 and all(len(m) == tn // HEAD_DIM for m in tile_modes)
    assert all(len(cols) == n_tiles for cols in piece_cols) and gain.shape == (1, tn * n_tiles)
    w_specs = [pl.BlockSpec((None, k, PROJ_PIECE),
                            functools.partial(lambda j, i, cols: (layer, 0, _select(j, cols)), cols=tuple(cols)))
               for cols in piece_cols]
    return pl.pallas_call(
        functools.partial(_proj_kernel, tile_modes=tuple(tile_modes), n_piece=n_piece),
        out_shape=jax.ShapeDtypeStruct((t, tn * n_tiles), out_dtype),
        grid=(n_tiles, t // tm),
        in_specs=[pl.BlockSpec((tm, k), lambda j, i: (i, 0))] + w_specs
                 + [pl.BlockSpec((1, tn), lambda j, i: (0, j))],
        out_specs=pl.BlockSpec((tm, tn), lambda j, i: (i, j)),
        scratch_shapes=[pltpu.VMEM((k, tn), BF16)],
        compiler_params=_params(2),
        name=name,
    )(h, *([w3] * n_piece), gain)


def _banded_kernel(*refs, n_q, n_kv, slopes, max_dist, dist_scale, tq, unit,
                   has_sink, mul_z, write_lse):
    refs = list(refs)
    q_ref, kc_ref, kp_ref, vc_ref, vp_ref = refs[:5]
    pos = 5
    sink_ref = z_ref = lse_ref = None
    if has_sink:
        sink_ref = refs[pos]; pos += 1
    if mul_z:
        z_ref = refs[pos]; pos += 1
    o_ref = refs[pos]; pos += 1
    if write_lse:
        lse_ref = refs[pos]; pos += 1

    shared_kv = n_kv < n_q and unit > 1
    group = n_q // n_kv
    assert n_q % unit == 0 and (not shared_kv or group == unit)
    kk = lax.broadcasted_iota(jnp.int32, (2 * BLOCK, BLOCK), 0)
    r = lax.broadcasted_iota(jnp.int32, (2 * BLOCK, BLOCK), 1)
    dist = r + BLOCK - kk
    distf = (dist * dist_scale).astype(F32)
    valid = (dist >= 0) & (dist <= max_dist)
    has_prev = jnp.minimum(pl.program_id(1), 1) * BLOCK
    valid0 = valid & (kk + has_prev >= BLOCK)

    for u in range(n_q // unit):
        heads = list(range(u * unit, (u + 1) * unit))
        bias = jnp.concatenate([jnp.where(valid, -slopes[h] * distf, NEG) for h in heads], axis=1)
        bias0 = jnp.concatenate([jnp.where(valid0, -slopes[h] * distf, NEG) for h in heads], axis=1)
        if has_sink:
            sink = jnp.concatenate([jnp.full((1, BLOCK), sink_ref[h], F32) for h in heads], axis=1)
        for j in range(tq // BLOCK):
            rows = slice(j * BLOCK, (j + 1) * BLOCK)
            prows = slice((j - 1) * BLOCK, j * BLOCK)

            def window(cur_ref, prev_ref, kv):
                ks = slice(kv * HEAD_DIM, (kv + 1) * HEAD_DIM)
                prev = prev_ref[:, ks] if j == 0 else cur_ref[prows, ks]
                return jnp.concatenate([prev, cur_ref[rows, ks]], axis=0)

            qs = [q_ref[rows, h * HEAD_DIM:(h + 1) * HEAD_DIM] for h in heads]
            if shared_kv:
                kwin = window(kc_ref, kp_ref, heads[0] // group)
                s_t = lax.dot_general(kwin, jnp.concatenate(qs, axis=0), CONTRACT_LAST,
                                      preferred_element_type=F32)
            else:
                s_t = jnp.concatenate(
                    [lax.dot_general(window(kc_ref, kp_ref, h // group), q, CONTRACT_LAST,
                                     preferred_element_type=F32) for h, q in zip(heads, qs)], axis=1)
            s_t = s_t * SCALE + (bias0 if j == 0 else bias)
            m = jnp.max(s_t, axis=0, keepdims=True)
            if has_sink:
                m = jnp.maximum(m, sink)
            e = jnp.exp(s_t - m)
            denom = jnp.sum(e, axis=0, keepdims=True)
            if has_sink:
                denom = denom + jnp.exp(sink - m)
            p_t = (e * (1.0 / denom)).astype(BF16)
            if shared_kv:
                o_all = lax.dot_general(p_t, window(vc_ref, vp_ref, heads[0] // group), CONTRACT_FIRST,
                                        preferred_element_type=F32)
            if write_lse:
                lse = m + jnp.log(denom)
            for g, h in enumerate(heads):
                hs = slice(h * HEAD_DIM, (h + 1) * HEAD_DIM)
                gs = slice(g * BLOCK, (g + 1) * BLOCK)
                if shared_kv:
                    o = o_all[gs, :]
                else:
                    o = lax.dot_general(p_t[:, gs], window(vc_ref, vp_ref, h // group), CONTRACT_FIRST,
                                        preferred_element_type=F32)
                if mul_z:
                    o = o * z_ref[rows, hs]
                o_ref[rows, hs] = o.astype(o_ref.dtype)
                if write_lse:
                    lse_ref[rows, hs] = jnp.broadcast_to(lse[:, gs], (BLOCK, BLOCK)).T


def banded_attention(qkv, *, rows_per_seq, qcol, kcol, vcol, n_q, n_kv, slopes, max_dist,
                     dist_scale, tq, out_dtype, unit=4, sinks=None, z=None, zcol=0, write_lse=False,
                     name="banded"):
    rows_total, _ = qkv.shape
    n_seq = rows_total // rows_per_seq
    nrt = rows_per_seq // tq
    qw, kw = n_q * HEAD_DIM, n_kv * HEAD_DIM
    assert qcol % qw == 0 and kcol % kw == 0 and vcol % kw == 0
    sub = tq // BLOCK

    def cur(col, w):
        return lambda s, i: (s * nrt + i, col // w)

    def prev(col, w):
        return lambda s, i: (jnp.maximum((s * nrt + i) * sub - 1, 0), col // w)

    in_specs = [
        pl.BlockSpec((tq, qw), cur(qcol, qw)),
        pl.BlockSpec((tq, kw), cur(kcol, kw)),
        pl.BlockSpec((BLOCK, kw), prev(kcol, kw)),
        pl.BlockSpec((tq, kw), cur(vcol, kw)),
        pl.BlockSpec((BLOCK, kw), prev(vcol, kw)),
    ]
    args = [qkv, qkv, qkv, qkv, qkv]
    if sinks is not None:
        in_specs.append(pl.BlockSpec(memory_space=pltpu.SMEM))
        args.append(sinks)
    if z is not None:
        assert zcol % qw == 0
        in_specs.append(pl.BlockSpec((tq, qw), cur(zcol, qw)))
        args.append(z)
    out_block = pl.BlockSpec((tq, qw), cur(0, qw))
    out_sds = jax.ShapeDtypeStruct((rows_total, qw), out_dtype)
    out_shape, out_specs = out_sds, out_block
    if write_lse:
        out_shape = (out_sds, jax.ShapeDtypeStruct((rows_total, qw), F32))
        out_specs = (out_block, out_block)
    kern = functools.partial(
        _banded_kernel, n_q=n_q, n_kv=n_kv, slopes=tuple(slopes), max_dist=max_dist,
        dist_scale=dist_scale, tq=tq, unit=unit, has_sink=sinks is not None, mul_z=z is not None,
        write_lse=write_lse)
    return pl.pallas_call(
        kern, out_shape=out_shape, grid=(n_seq, nrt),
        in_specs=in_specs, out_specs=out_specs,
        compiler_params=_params(2), name=name,
    )(*args)


def _combine_kernel(o0, o1, o2, l0, l1, l2, z_ref, u_ref, o1_sc, o2_sc, l1_sc, l2_sc, *, tm, dils):
    n_chunk = o0.shape[1] // HEAD_DIM
    for src, dst, d in ((o1, o1_sc, dils[0]), (l1, l1_sc, dils[0]),
                        (o2, o2_sc, dils[1]), (l2, l2_sc, dils[1])):
        n = tm // d
        for p in range(d):
            for c in range(n_chunk):
                dst[c, pl.ds(p, n, stride=d), :] = src[p, :, c * HEAD_DIM:(c + 1) * HEAD_DIM]
    for c in range(n_chunk):
        cs = slice(c * HEAD_DIM, (c + 1) * HEAD_DIM)
        a0, a1, a2 = l0[:, cs], l1_sc[c], l2_sc[c]
        m = jnp.maximum(jnp.maximum(a0, a1), a2)
        e0, e1, e2 = jnp.exp(a0 - m), jnp.exp(a1 - m), jnp.exp(a2 - m)
        inv = 1.0 / (e0 + e1 + e2)
        ya = (e0 * inv) * o0[:, cs] + (e1 * inv) * o1_sc[c] + (e2 * inv) * o2_sc[c]
        u_ref[:, cs] = (ya * z_ref[:, cs]).astype(u_ref.dtype)


def combine_a(outs, lses, z, zcol, tm, dils):
    t, w = outs[0].shape
    blk = pl.BlockSpec((tm, w), lambda i: (i, 0))

    def pblk(d):
        return pl.BlockSpec((d, tm // d, w), lambda i: (0, i, 0))

    def p3(a, d):
        return a.reshape(d, t // d, w)

    d1, d2 = dils
    return pl.pallas_call(
        functools.partial(_combine_kernel, tm=tm, dils=dils),
        out_shape=jax.ShapeDtypeStruct((t, w), BF16),
        grid=(t // tm,),
        in_specs=[blk, pblk(d1), pblk(d2), blk, pblk(d1), pblk(d2),
                  pl.BlockSpec((tm, w), lambda i: (i, zcol // w))],
        out_specs=blk,
        scratch_shapes=[pltpu.VMEM((w // HEAD_DIM, tm, HEAD_DIM), F32)] * 4,
        compiler_params=_params(1),
        name="combine_a",
    )(outs[0], p3(outs[1], d1), p3(outs[2], d2), lses[0], p3(lses[1], d1), p3(lses[2], d2), z)


def _diff_kernel(slope_ref, q_ref, k_ref, v_ref, lam_ref, g_ref, z_ref, o_ref,
                 vt_sc, acc_sc, bias_sc, s_sc, *, tq, tk, seq, lam_init):
    h = pl.program_id(1)
    qi = pl.program_id(2)
    slope = slope_ref[h]
    per_q = tq // tk
    base = qi * per_q

    @pl.when(qi == 0)
    def _():
        for cidx in range(seq // tk):
            rows = slice(cidx * tk, (cidx + 1) * tk)
            vt_sc[:, rows] = v_ref[rows, :].astype(F32).T.astype(BF16)

    kk = lax.broadcasted_iota(jnp.int32, (tk, tq), 0)
    r = lax.broadcasted_iota(jnp.int32, (tk, tq), 1)
    bias = (-slope / SCALE) * (r - kk).astype(F32)
    bias_sc[0] = bias
    for dblk in range(per_q):
        bias_sc[1 + dblk] = jnp.where(r >= kk + dblk * tk, bias, NEG)
    qs = [q_ref[:, c * HEAD_DIM:(c + 1) * HEAD_DIM] for c in range(2)]
    to_log2 = SCALE * LOG2E

    def shift_of(kj):
        return (slope * LOG2E) * ((base - kj) * tk).astype(F32)

    def max_chunk(kj0, n, ms):
        ms = list(ms)
        for u in range(n):
            kj = kj0 + u
            start = pl.multiple_of(kj * tk, tk)
            b = bias_sc[jnp.maximum(kj - base + 1, 0)]
            for comp in range(2):
                k = k_ref[pl.ds(start, tk), comp * HEAD_DIM:(comp + 1) * HEAD_DIM]
                a = (lax.dot_general(k, qs[comp], CONTRACT_LAST, preferred_element_type=F32) + b) * to_log2
                s_sc[comp, pl.ds(start, tk), :] = a
                ms[comp] = jnp.maximum(ms[comp], jnp.max(a, axis=0, keepdims=True) - shift_of(kj))
        return tuple(ms)

    def exp_chunk(kj0, n, ls):
        ls = list(ls)
        start0 = pl.multiple_of(kj0 * tk, tk)
        vt = vt_sc[:, pl.ds(start0, n * tk)]
        for comp in range(2):
            es = []
            for u in range(n):
                start = pl.multiple_of((kj0 + u) * tk, tk)
                e = jnp.exp2(s_sc[comp, pl.ds(start, tk), :] - (ms[comp] + shift_of(kj0 + u)))
                ls[comp] = ls[comp] + jnp.sum(e, axis=0, keepdims=True)
                es.append(e.astype(BF16))
            e_cat = es[0] if n == 1 else jnp.concatenate(es, axis=0)
            acc_sc[comp] += jnp.dot(vt, e_cat, preferred_element_type=F32)
        return tuple(ls)

    def over_blocks(chunk_fn, carry):
        done = 0
        for n in DIFF_UNROLL:
            if n % per_q:
                continue
            n_it = (base + per_q - done) // n
            carry = lax.fori_loop(
                0, n_it, lambda t, c, n=n, done=done: chunk_fn(done + t * n, n, c), carry)
            done = done + n_it * n
        return carry

    neg = jnp.full((1, tq), NEG, F32)
    ms = over_blocks(max_chunk, (neg, neg))
    acc_sc[...] = jnp.zeros(acc_sc.shape, F32)
    zero = jnp.zeros((1, tq), F32)
    l1, l2 = over_blocks(exp_chunk, (zero, zero))

    lp = lam_ref[...]
    lam_full = (jnp.exp(jnp.sum(lp[0:1] * lp[1:2], axis=-1, keepdims=True))
                - jnp.exp(jnp.sum(lp[2:3] * lp[3:4], axis=-1, keepdims=True)) + lam_init)
    y_t = acc_sc[0] * (1.0 / l1) - lam_full * (acc_sc[1] * (1.0 / l2))
    ms2 = jnp.mean(y_t * y_t, axis=0, keepdims=True)
    y = (y_t * lax.rsqrt(ms2 + EPS)).T
    y = y * g_ref[...] * (1.0 - lam_init)
    o_ref[...] = (y * z_ref[...]).astype(o_ref.dtype)


def diff_attention(cp, slopes, lam_l, subln_g, z, *, seq, qcol, kcol, vcol, zcol, tq, tk, lam_init):
    t = cp.shape[0]
    n_batch = t // seq
    nq = seq // tq
    hw = 2 * HEAD_DIM
    kern = functools.partial(_diff_kernel, tq=tq, tk=tk, seq=seq, lam_init=lam_init)
    return pl.pallas_call(
        kern,
        out_shape=jax.ShapeDtypeStruct((t, C_OUT), BF16),
        grid=(n_batch, C_HEADS, nq),
        in_specs=[
            pl.BlockSpec(memory_space=pltpu.SMEM),
            pl.BlockSpec((tq, hw), lambda b, h, i: (b * nq + i, qcol // hw + h)),
            pl.BlockSpec((seq, hw), lambda b, h, i: (b, kcol // hw + h)),
            pl.BlockSpec((seq, hw), lambda b, h, i: (b, vcol // hw + h)),
            pl.BlockSpec((4, HEAD_DIM), lambda b, h, i: (0, 0)),
            pl.BlockSpec((1, hw), lambda b, h, i: (0, 0)),
            pl.BlockSpec((tq, hw), lambda b, h, i: (b * nq + i, zcol // hw + h)),
        ],
        out_specs=pl.BlockSpec((tq, hw), lambda b, h, i: (b * nq + i, h)),
        scratch_shapes=[pltpu.VMEM((hw, seq), BF16), pltpu.VMEM((2, hw, tq), F32),
                        pltpu.VMEM((1 + tq // tk, tk, tq), F32), pltpu.VMEM((2, seq, tq), F32)],
        compiler_params=_params(3),
        name="diff_attention",
    )(slopes, cp, cp, cp, lam_l, subln_g.reshape(1, hw), z)


def _mem_kernel(q_ref, k_ref, v_ref, z_ref, o_ref):
    for h in range(M_HEADS):
        hs = slice(h * HEAD_DIM, (h + 1) * HEAD_DIM)
        s_t = lax.dot_general(k_ref[:, hs], q_ref[:, hs], CONTRACT_LAST,
                              preferred_element_type=F32) * SCALE
        m = jnp.max(s_t, axis=0, keepdims=True)
        e = jnp.exp(s_t - m)
        p_t = (e * (1.0 / jnp.sum(e, axis=0, keepdims=True))).astype(BF16)
        o = lax.dot_general(p_t, v_ref[:, hs], CONTRACT_FIRST, preferred_element_type=F32)
        o_ref[:, hs] = (o * z_ref[:, hs]).astype(o_ref.dtype)


def mem_attention(cp, mkv, z, *, seq, qcol, zcol, tq):
    t = cp.shape[0]
    nq = seq // tq
    return pl.pallas_call(
        _mem_kernel,
        out_shape=jax.ShapeDtypeStruct((t, M_OUT), BF16),
        grid=(t // tq,),
        in_specs=[
            pl.BlockSpec((tq, M_OUT), lambda i: (i, qcol // M_OUT)),
            pl.BlockSpec((N_MEM, M_OUT), lambda i: (i // nq, 0)),
            pl.BlockSpec((N_MEM, M_OUT), lambda i: (i // nq, 1)),
            pl.BlockSpec((tq, M_OUT), lambda i: (i, zcol // M_OUT)),
        ],
        out_specs=pl.BlockSpec((tq, M_OUT), lambda i: (i, 0)),
        compiler_params=_params(1),
        name="mem_attention",
    )(cp, mkv, mkv, z)


_BRANCH_ROWS = ((0, A_OUT), (A_OUT, A_OUT + B_OUT), (A_OUT + B_OUT, A_OUT + B_OUT + C_OUT),
                (A_OUT + B_OUT + C_OUT, A_OUT + B_OUT + C_OUT + M_OUT))
_WB_CHUNK = 512


def _merge_kernel(*refs):
    h_ref, us = refs[0], refs[1:5]
    wgs, bgs = refs[5:9], refs[9:13]
    n_chunk = (A_OUT + B_OUT + C_OUT + M_OUT) // _WB_CHUNK
    wbs = refs[13:13 + n_chunk]
    o_ref, wg_sc, wb_sc = refs[13 + n_chunk:]

    @pl.when(pl.program_id(1) == 0)
    def _():
        for b in range(N_BRANCH):
            wg_sc[b] = wgs[b][...].astype(BF16)
        for c in range(n_chunk):
            wb_sc[c * _WB_CHUNK:(c + 1) * _WB_CHUNK, :] = wbs[c][...].astype(BF16)

    h = h_ref[...]
    acc = None
    for b in range(N_BRANCH):
        lo, hi = _BRANCH_ROWS[b]
        gate = jax.nn.sigmoid(jnp.dot(h, wg_sc[b], preferred_element_type=F32) + bgs[b][...])
        term = gate * jnp.dot(us[b][...], wb_sc[lo:hi, :], preferred_element_type=F32)
        acc = term if acc is None else acc + term
    o_ref[...] = acc.astype(o_ref.dtype)


def gated_merge(h, us, w_in, b_gate3, w_branch, layer, tm, tn):
    t, d = h.shape
    nj = d // tn
    gate0 = O_GATE // tn
    n_chunk = w_branch.shape[1] // _WB_CHUNK
    in_specs = [pl.BlockSpec((tm, d), lambda j, i: (i, 0))]
    in_specs += [pl.BlockSpec((tm, u.shape[1]), lambda j, i: (i, 0)) for u in us]
    in_specs += [pl.BlockSpec((None, d, tn), functools.partial(
        lambda j, i, b: (layer, 0, gate0 + b * nj + j), b=b)) for b in range(N_BRANCH)]
    in_specs += [pl.BlockSpec((None, 1, tn), functools.partial(
        lambda j, i, b: (layer, 0, b * nj + j), b=b)) for b in range(N_BRANCH)]
    in_specs += [pl.BlockSpec((None, _WB_CHUNK, tn), functools.partial(
        lambda j, i, c: (layer, c, j), c=c)) for c in range(n_chunk)]
    return pl.pallas_call(
        _merge_kernel,
        out_shape=jax.ShapeDtypeStruct((t, d), BF16),
        grid=(nj, t // tm),
        in_specs=in_specs,
        out_specs=pl.BlockSpec((tm, tn), lambda j, i: (i, j)),
        scratch_shapes=[pltpu.VMEM((N_BRANCH, d, tn), BF16),
                        pltpu.VMEM((w_branch.shape[1], tn), BF16)],
        compiler_params=_params(2),
        name="gated_merge",
    )(h, *us, *([w_in] * N_BRANCH), *([b_gate3] * N_BRANCH), *([w_branch] * n_chunk))


def _out_kernel(x_ref, m_ref, w_ref, o_ref, w_sc):
    @pl.when(pl.program_id(1) == 0)
    def _():
        w_sc[...] = w_ref[...].astype(BF16)

    o_ref[...] = x_ref[...] + jnp.dot(m_ref[...], w_sc[...], preferred_element_type=F32)


def out_proj(x, merged, w_out, layer, tm, tn):
    t, d = x.shape
    return pl.pallas_call(
        _out_kernel,
        out_shape=jax.ShapeDtypeStruct((t, d), F32),
        grid=(d // tn, t // tm),
        in_specs=[pl.BlockSpec((tm, tn), lambda j, i: (i, j)),
                  pl.BlockSpec((tm, d), lambda j, i: (i, 0)),
                  pl.BlockSpec((None, d, tn), lambda j, i: (layer, 0, j))],
        out_specs=pl.BlockSpec((tm, tn), lambda j, i: (i, j)),
        scratch_shapes=[pltpu.VMEM((d, tn), BF16)],
        compiler_params=_params(2),
        name="out_proj",
    )(x, merged, w_out)


def kernel(x, mem, norm_g, w_in, b_gate, qk_gain, sinks, lam, subln_g, mem_norm_g,
           w_mem_kv, w_branch, w_out):
    bsz, s_len, d = x.shape
    depth = w_in.shape[0]
    t = bsz * s_len
    xf = x.reshape(t, d)
    memf = mem.reshape(bsz * N_MEM, d)
    b_gate3 = b_gate.reshape(depth, 1, N_BRANCH * d)

    slopes_a = _alibi_slopes(A_HEADS_PER_GROUP)
    slopes_b = _alibi_slopes(B_Q_HEADS)
    slopes_c = jnp.asarray(_alibi_slopes(C_HEADS), F32)
    ones = lambda n: jnp.ones((n,), F32)
    dils = tuple(dil for _, dil in A_PATTERNS)

    pc = PROJ_PIECE
    z_zb, z_zc, z_za, z_zm = 0, B_OUT, B_OUT + C_OUT, B_OUT + C_OUT + A_OUT
    bc_bq, bc_bk, bc_bv, bc_mq = 0, B_OUT, B_OUT + B_KV_HEADS * HEAD_DIM, B_OUT + 2 * B_KV_HEADS * HEAD_DIM
    bc_cq = bc_mq + M_OUT
    bc_ck, bc_cv = bc_cq + C_OUT, bc_cq + 2 * C_OUT

    for l in range(depth):
        gq = qk_gain[l]
        gain_a = jnp.concatenate([jnp.tile(gq[QK_A_Q], 4), jnp.tile(gq[QK_A_K], 4), ones(A_OUT)]).reshape(1, -1)
        gain_bc = jnp.concatenate([
            jnp.tile(gq[QK_B_Q], B_Q_HEADS), jnp.tile(gq[QK_B_K], B_KV_HEADS), ones(B_KV_HEADS * HEAD_DIM),
            jnp.tile(gq[QK_M_Q], M_HEADS), jnp.tile(gq[QK_C_Q], 2 * C_HEADS),
            jnp.tile(gq[QK_C_K], 2 * C_HEADS), ones(C_OUT)]).reshape(1, -1)
        gain_z = ones(O_GATE - O_ZA).reshape(1, -1)
        gain_m = jnp.concatenate([jnp.tile(gq[QK_M_K], M_HEADS), ones(M_OUT)]).reshape(1, -1)

        hs = rmsnorm_bf16(xf, norm_g[l], 256, dils[1:])
        h = hs[0]
        h_perm = [h] + [hp.reshape(t, d) for hp in hs[1:]]

        outs, lses = [], []
        for g, (win, dil) in enumerate(A_PATTERNS):
            apg = proj(h_perm[g], w_in, l, [(O_AQ // pc + g,), (O_AK // pc + g,), (O_AV // pc + g,)],
                       gain_a, ("NNNNNNNNPPPP",), 1024, BF16, f"proj_a{g}")
            rows = s_len // dil
            o_g, lse_g = banded_attention(
                apg, rows_per_seq=rows, qcol=0, kcol=A_OUT, vcol=2 * A_OUT,
                n_q=A_HEADS_PER_GROUP, n_kv=A_HEADS_PER_GROUP, slopes=slopes_a,
                max_dist=win // dil, dist_scale=dil, tq=min(512, rows),
                out_dtype=F32, unit=1 if rows >= 512 else 4, write_lse=True, name=f"attn_a{g}")
            outs.append(o_g)
            lses.append(lse_g)

        bc = proj(h, w_in, l,
                  [(O_BQ // pc, O_BK // pc, O_CQ // pc, O_CK // pc, O_CV // pc),
                   (O_BQ // pc + 1, O_MQ // pc, O_CQ // pc + 1, O_CK // pc + 1, O_CV // pc + 1)],
                  gain_bc, ("NNNNNNNN", "NNPPNNNN", "NNNNNNNN", "NNNNNNNN", "PPPPPPPP"), 1024, BF16, "proj_bc")
        sz = proj(h, w_in, l, [(O_ZB // pc, O_ZC // pc, O_ZA // pc), (O_ZB // pc + 1, O_ZC // pc + 1, O_ZM // pc)],
                  gain_z, ("SSSSSSSS",) * 3, 1024, F32, "proj_z")

        u_a = combine_a(outs, lses, sz, z_za, 512, dils[1:])

        u_b = banded_attention(
            bc, rows_per_seq=s_len, qcol=bc_bq, kcol=bc_bk, vcol=bc_bv,
            n_q=B_Q_HEADS, n_kv=B_KV_HEADS, slopes=slopes_b, max_dist=B_WINDOW - 1, dist_scale=1,
            tq=512, out_dtype=BF16, sinks=sinks[l], z=sz, zcol=z_zb, name="attn_b")

        lam_init = 0.8 - 0.6 * math.exp(-0.3 * l)
        u_c = diff_attention(bc, slopes_c, lam[l], subln_g[l], sz, seq=s_len,
                             qcol=bc_cq, kcol=bc_ck, vcol=bc_cv, zcol=z_zc, tq=512, tk=256, lam_init=lam_init)

        mn = rmsnorm_bf16(memf, mem_norm_g[l], 256)[0]
        mkv = proj(mn, w_mem_kv, l, [(0,), (1,)], gain_m, ("NNNNPPPP",), bsz * N_MEM, BF16, "proj_mem")
        u_m = mem_attention(bc, mkv, sz, seq=s_len, qcol=bc_mq, zcol=z_zm, tq=512)

        merged = gated_merge(h, [u_a, u_b, u_c, u_m], w_in, b_gate3, w_branch, l, 1024, 256)
        xf = out_proj(xf, merged, w_out, l, 1024, 1024)

    return xf.reshape(bsz, s_len, d)
```

```python
import functools
import math

import jax
import jax.numpy as jnp
from jax import lax
from jax.experimental import pallas as pl
from jax.experimental.pallas import tpu as pltpu

D_MODEL = 2048
HEAD_DIM = 128
BLOCK = 128
EPS = 1e-6
A_PATTERNS = ((128, 1), (512, 4), (2048, 16))
A_GROUPS = 3
A_HEADS_PER_GROUP = 4
A_HEADS = A_GROUPS * A_HEADS_PER_GROUP
A_OUT = A_HEADS_PER_GROUP * HEAD_DIM
B_Q_HEADS = 8
B_KV_HEADS = 2
B_WINDOW = 128
B_OUT = B_Q_HEADS * HEAD_DIM
C_HEADS = 4
C_OUT = C_HEADS * 2 * HEAD_DIM
N_MEM = 256
M_HEADS = 4
M_OUT = M_HEADS * HEAD_DIM
N_BRANCH = 4
QK_A_Q, QK_A_K, QK_B_Q, QK_B_K, QK_C_Q, QK_C_K, QK_M_Q, QK_M_K = range(8)

_IN_SIZES = (A_HEADS * HEAD_DIM, A_HEADS * HEAD_DIM, A_HEADS * HEAD_DIM,
             B_Q_HEADS * HEAD_DIM, B_KV_HEADS * HEAD_DIM, B_KV_HEADS * HEAD_DIM,
             C_HEADS * 2 * HEAD_DIM, C_HEADS * 2 * HEAD_DIM, C_OUT,
             M_HEADS * HEAD_DIM, A_OUT, B_OUT, C_OUT, M_OUT, N_BRANCH * D_MODEL)
_OFF = [0]
for _s in _IN_SIZES:
    _OFF.append(_OFF[-1] + _s)
(O_AQ, O_AK, O_AV, O_BQ, O_BK, O_BV, O_CQ, O_CK, O_CV, O_MQ,
 O_ZA, O_ZB, O_ZC, O_ZM, O_GATE, O_END) = _OFF

NEG = -1e30
SCALE = HEAD_DIM ** -0.5
VMEM_LIMIT = 58 * 1024 * 1024
PROJ_PIECE = 512
DIFF_UNROLL = (8, 4, 2, 1)
LOG2E = math.log2(math.e)
BF16 = jnp.bfloat16
F32 = jnp.float32
CONTRACT_LAST = (((1,), (1,)), ((), ()))
CONTRACT_FIRST = (((0,), (0,)), ((), ()))


def _params(n_axes):
    return pltpu.CompilerParams(dimension_semantics=("arbitrary",) * n_axes,
                                vmem_limit_bytes=VMEM_LIMIT)


def _sigmoid(x):
    return 0.5 * jnp.tanh(0.5 * x) + 0.5


def _alibi_slopes(n):
    return [2.0 ** (-8.0 * i / n) for i in range(1, n + 1)]


def _rmsnorm_kernel(x_ref, g_ref, o_ref, *perm_refs, dils, tm):
    x = x_ref[...]
    ms = jnp.mean(x * x, axis=-1, keepdims=True)
    h = (x * lax.rsqrt(ms + EPS) * g_ref[...]).astype(BF16)
    o_ref[...] = h
    row = lax.broadcasted_iota(jnp.int32, (tm, tm), 0)
    col = lax.broadcasted_iota(jnp.int32, (tm, tm), 1)
    for d, p_ref in zip(dils, perm_refs):
        n = tm // d
        src = (row % n) * d + row // n
        perm = jnp.where(col == src, 1.0, 0.0).astype(BF16)
        hp = jnp.dot(perm, h, preferred_element_type=F32).astype(BF16)
        for p in range(d):
            p_ref[p] = hp[p * n:(p + 1) * n, :]


def rmsnorm_bf16(x, g, tm, dils=()):
    t, d = x.shape
    out_shape = [jax.ShapeDtypeStruct((t, d), BF16)]
    out_specs = [pl.BlockSpec((tm, d), lambda i: (i, 0))]
    for dil in dils:
        out_shape.append(jax.ShapeDtypeStruct((dil, t // dil, d), BF16))
        out_specs.append(pl.BlockSpec((dil, tm // dil, d), lambda i: (0, i, 0)))
    outs = pl.pallas_call(
        functools.partial(_rmsnorm_kernel, dils=tuple(dils), tm=tm),
        out_shape=out_shape,
        grid=(t // tm,),
        in_specs=[pl.BlockSpec((tm, d), lambda i: (i, 0)),
                  pl.BlockSpec((1, d), lambda i: (0, 0))],
        out_specs=out_specs,
        compiler_params=_params(1),
        name="rmsnorm",
    )(x, g.reshape(1, d))
    return outs


def _proj_kernel(*refs, tile_modes, n_piece):
    h_ref, w_refs = refs[0], refs[1:1 + n_piece]
    gain_ref, o_ref, w_sc = refs[1 + n_piece:]
    j = pl.program_id(0)

    @pl.when(pl.program_id(1) == 0)
    def _():
        for p in range(n_piece):
            w_sc[:, p * PROJ_PIECE:(p + 1) * PROJ_PIECE] = w_refs[p][...].astype(BF16)

    acc = jnp.dot(h_ref[...], w_sc[...], preferred_element_type=F32)
    for c in range(len(tile_modes[0])):
        sl = slice(c * HEAD_DIM, (c + 1) * HEAD_DIM)
        modes = [m[c] for m in tile_modes]
        y = acc[:, sl]
        if 'S' in modes:
            assert set(modes) == {'S'}
            y = y * _sigmoid(y)
        elif 'N' in modes:
            ms = jnp.mean(y * y, axis=-1, keepdims=True)
            normed = y * lax.rsqrt(ms + EPS) * gain_ref[:, sl]
            if set(modes) == {'N'}:
                y = normed
            else:
                is_norm = functools.reduce(jnp.logical_or, [j == jj for jj, m in enumerate(modes) if m == 'N'])
                y = jnp.where(is_norm, normed, y)
        o_ref[:, sl] = y.astype(o_ref.dtype)


def _select(j, values):
    out = values[-1]
    for jj in range(len(values) - 2, -1, -1):
        out = jnp.where(j == jj, values[jj], out)
    return out


def proj(h, w3, layer, piece_cols, gain, tile_modes, tm, out_dtype, name):
    t, k = h.shape
    n_piece = len(piece_cols)
    tn = n_piece * PROJ_PIECE
    n_tiles = len(tile_modes)
    assert t % tm == 0 and all(len(m) == tn // HEAD_DIM for m in tile_modes)
    assert all(len(cols) == n_tiles for cols in piece_cols) and gain.shape == (1, tn * n_tiles)
    w_specs = [pl.BlockSpec((None, k, PROJ_PIECE),
                            functools.partial(lambda j, i, cols: (layer, 0, _select(j, cols)), cols=tuple(cols)))
               for cols in piece_cols]
    return pl.pallas_call(
        functools.partial(_proj_kernel, tile_modes=tuple(tile_modes), n_piece=n_piece),
        out_shape=jax.ShapeDtypeStruct((t, tn * n_tiles), out_dtype),
        grid=(n_tiles, t // tm),
        in_specs=[pl.BlockSpec((tm, k), lambda j, i: (i, 0))] + w_specs
                 + [pl.BlockSpec((1, tn), lambda j, i: (0, j))],
        out_specs=pl.BlockSpec((tm, tn), lambda j, i: (i, j)),
        scratch_shapes=[pltpu.VMEM((k, tn), BF16)],
        compiler_params=_params(2),
        name=name,
    )(h, *([w3] * n_piece), gain)


def _banded_kernel(*refs, n_q, n_kv, slopes, max_dist, dist_scale, tq, rows_per_seq, unit, batch,
                   has_sink, mul_z, write_lse):
    refs = list(refs)
    q_ref, kc_ref, kp_ref, vc_ref, vp_ref = refs[:5]
    pos = 5
    sink_ref = z_ref = lse_ref = None
    if has_sink:
        sink_ref = refs[pos]; pos += 1
    if mul_z:
        z_ref = refs[pos]; pos += 1
    o_ref = refs[pos]; pos += 1
    if write_lse:
        lse_ref = refs[pos]; pos += 1

    shared_kv = n_kv < n_q and unit > 1
    group = n_q // n_kv
    assert n_q % unit == 0 and (not shared_kv or group == unit)
    kk = lax.broadcasted_iota(jnp.int32, (2 * BLOCK, BLOCK), 0)
    r = lax.broadcasted_iota(jnp.int32, (2 * BLOCK, BLOCK), 1)
    dist = r + BLOCK - kk
    distf = (dist * dist_scale).astype(F32)
    valid = (dist >= 0) & (dist <= max_dist)
    valid_start = valid & (kk >= BLOCK)
    if rows_per_seq > tq:
        has_prev = jnp.minimum(pl.program_id(1), 1) * BLOCK
        valid0 = valid & (kk + has_prev >= BLOCK)
    else:
        valid0 = valid_start

    def window(cur_ref, prev_ref, kv, j):
        ks = slice(kv * HEAD_DIM, (kv + 1) * HEAD_DIM)
        prev = prev_ref[:, ks] if j == 0 else cur_ref[(j - 1) * BLOCK:j * BLOCK, ks]
        return jnp.concatenate([prev, cur_ref[j * BLOCK:(j + 1) * BLOCK, ks]], axis=0)

    units = [(list(range(u * unit, (u + 1) * unit)), j)
             for u in range(n_q // unit) for j in range(tq // BLOCK)]

    for b0 in range(0, len(units), batch or len(units)):
        batch_units = units[b0:b0 + (batch or len(units))]
        scores = []
        for heads, j in batch_units:
            rows = slice(j * BLOCK, (j + 1) * BLOCK)
            qs = [q_ref[rows, h * HEAD_DIM:(h + 1) * HEAD_DIM] for h in heads]
            if shared_kv:
                s_t = lax.dot_general(window(kc_ref, kp_ref, heads[0] // group, j), jnp.concatenate(qs, axis=0),
                                      CONTRACT_LAST, preferred_element_type=F32)
            else:
                s_t = jnp.concatenate(
                    [lax.dot_general(window(kc_ref, kp_ref, h // group, j), q, CONTRACT_LAST,
                                     preferred_element_type=F32) for h, q in zip(heads, qs)], axis=1)
            scores.append(s_t)

        probs, lses = [], []
        for (heads, j), s_t in zip(batch_units, scores):
            v_mask = valid0 if j == 0 else (valid_start if (j * BLOCK) % rows_per_seq == 0 else valid)
            bias = jnp.concatenate([jnp.where(v_mask, (-slopes[h] * LOG2E) * distf, NEG) for h in heads], axis=1)
            s_t = s_t * (SCALE * LOG2E) + bias
            m = jnp.max(s_t, axis=0, keepdims=True)
            if has_sink:
                sink = jnp.concatenate([jnp.full((1, BLOCK), sink_ref[h] * LOG2E, F32) for h in heads], axis=1)
                m = jnp.maximum(m, sink)
            e = jnp.exp2(s_t - m)
            denom = jnp.sum(e, axis=0, keepdims=True)
            if has_sink:
                denom = denom + jnp.exp2(sink - m)
            probs.append((e * (1.0 / denom)).astype(BF16))
            lses.append(m + jnp.log2(denom) if write_lse else None)

        for (heads, j), p_t, lse in zip(batch_units, probs, lses):
            rows = slice(j * BLOCK, (j + 1) * BLOCK)
            if shared_kv:
                o_all = lax.dot_general(p_t, window(vc_ref, vp_ref, heads[0] // group, j), CONTRACT_FIRST,
                                        preferred_element_type=F32)
            for g, h in enumerate(heads):
                hs = slice(h * HEAD_DIM, (h + 1) * HEAD_DIM)
                gs = slice(g * BLOCK, (g + 1) * BLOCK)
                if shared_kv:
                    o = o_all[gs, :]
                else:
                    o = lax.dot_general(p_t[:, gs], window(vc_ref, vp_ref, h // group, j), CONTRACT_FIRST,
                                        preferred_element_type=F32)
                if mul_z:
                    o = o * z_ref[rows, hs]
                o_ref[rows, hs] = o.astype(o_ref.dtype)
                if write_lse:
                    lse_ref[rows, hs] = jnp.broadcast_to(lse[:, gs], (BLOCK, BLOCK)).T


def banded_attention(qkv, *, rows_per_seq, qcol, kcol, vcol, n_q, n_kv, slopes, max_dist,
                     dist_scale, tq, out_dtype, unit=4, batch=None, sinks=None, z=None, zcol=0, write_lse=False,
                     name="banded"):
    rows_total, _ = qkv.shape
    nrt = max(rows_per_seq // tq, 1)
    assert tq % rows_per_seq == 0 or rows_per_seq % tq == 0
    n_seq = rows_total // (nrt * tq)
    qw, kw = n_q * HEAD_DIM, n_kv * HEAD_DIM
    assert qcol % qw == 0 and kcol % kw == 0 and vcol % kw == 0
    sub = tq // BLOCK

    def cur(col, w):
        return lambda s, i: (s * nrt + i, col // w)

    def prev(col, w):
        return lambda s, i: (jnp.maximum((s * nrt + i) * sub - 1, 0), col // w)

    in_specs = [
        pl.BlockSpec((tq, qw), cur(qcol, qw)),
        pl.BlockSpec((tq, kw), cur(kcol, kw)),
        pl.BlockSpec((BLOCK, kw), prev(kcol, kw)),
        pl.BlockSpec((tq, kw), cur(vcol, kw)),
        pl.BlockSpec((BLOCK, kw), prev(vcol, kw)),
    ]
    args = [qkv, qkv, qkv, qkv, qkv]
    if sinks is not None:
        in_specs.append(pl.BlockSpec(memory_space=pltpu.SMEM))
        args.append(sinks)
    if z is not None:
        assert zcol % qw == 0
        in_specs.append(pl.BlockSpec((tq, qw), cur(zcol, qw)))
        args.append(z)
    out_block = pl.BlockSpec((tq, qw), cur(0, qw))
    out_sds = jax.ShapeDtypeStruct((rows_total, qw), out_dtype)
    out_shape, out_specs = out_sds, out_block
    if write_lse:
        out_shape = (out_sds, jax.ShapeDtypeStruct((rows_total, qw), F32))
        out_specs = (out_block, out_block)
    kern = functools.partial(
        _banded_kernel, n_q=n_q, n_kv=n_kv, slopes=tuple(slopes), max_dist=max_dist,
        dist_scale=dist_scale, tq=tq, rows_per_seq=rows_per_seq, unit=unit, batch=batch, has_sink=sinks is not None, mul_z=z is not None,
        write_lse=write_lse)
    return pl.pallas_call(
        kern, out_shape=out_shape, grid=(n_seq, nrt),
        in_specs=in_specs, out_specs=out_specs,
        compiler_params=_params(2), name=name,
    )(*args)


def _combine_kernel(o0, o1, o2, l0, l1, l2, z_ref, u_ref, o1_sc, o2_sc, l1_sc, l2_sc, *, tm, dils):
    n_chunk = o0.shape[1] // HEAD_DIM
    for src, dst, d in ((o1, o1_sc, dils[0]), (l1, l1_sc, dils[0]),
                        (o2, o2_sc, dils[1]), (l2, l2_sc, dils[1])):
        n = tm // d
        for p in range(d):
            for c in range(n_chunk):
                dst[c, pl.ds(p, n, stride=d), :] = src[p, :, c * HEAD_DIM:(c + 1) * HEAD_DIM]
    for c in range(n_chunk):
        cs = slice(c * HEAD_DIM, (c + 1) * HEAD_DIM)
        a0, a1, a2 = l0[:, cs], l1_sc[c], l2_sc[c]
        m = jnp.maximum(jnp.maximum(a0, a1), a2)
        e0, e1, e2 = jnp.exp2(a0 - m), jnp.exp2(a1 - m), jnp.exp2(a2 - m)
        inv = 1.0 / (e0 + e1 + e2)
        ya = (e0 * inv) * o0[:, cs] + (e1 * inv) * o1_sc[c] + (e2 * inv) * o2_sc[c]
        u_ref[:, cs] = (ya * z_ref[:, cs]).astype(u_ref.dtype)


def combine_a(outs, lses, z, zcol, tm, dils):
    t, w = outs[0].shape
    blk = pl.BlockSpec((tm, w), lambda i: (i, 0))

    def pblk(d):
        return pl.BlockSpec((d, tm // d, w), lambda i: (0, i, 0))

    def p3(a, d):
        return a.reshape(d, t // d, w)

    d1, d2 = dils
    return pl.pallas_call(
        functools.partial(_combine_kernel, tm=tm, dils=dils),
        out_shape=jax.ShapeDtypeStruct((t, w), BF16),
        grid=(t // tm,),
        in_specs=[blk, pblk(d1), pblk(d2), blk, pblk(d1), pblk(d2),
                  pl.BlockSpec((tm, w), lambda i: (i, zcol // w))],
        out_specs=blk,
        scratch_shapes=[pltpu.VMEM((w // HEAD_DIM, tm, HEAD_DIM), F32)] * 4,
        compiler_params=_params(1),
        name="combine_a",
    )(outs[0], p3(outs[1], d1), p3(outs[2], d2), lses[0], p3(lses[1], d1), p3(lses[2], d2), z)


def _diff_kernel(slope_ref, q_ref, k_ref, v_ref, lam_ref, g_ref, z_ref, o_ref,
                 vt_sc, acc_sc, bias_sc, s_sc, *, tq, tk, seq, lam_init):
    h = pl.program_id(1)
    qi = pl.program_id(2)
    slope = slope_ref[h]
    per_q = tq // tk
    base = qi * per_q

    @pl.when(qi == 0)
    def _():
        for cidx in range(seq // tk):
            rows = slice(cidx * tk, (cidx + 1) * tk)
            vt_sc[:, rows] = v_ref[rows, :].astype(F32).T.astype(BF16)
        kk = lax.broadcasted_iota(jnp.int32, (tk, tq), 0)
        r = lax.broadcasted_iota(jnp.int32, (tk, tq), 1)
        bias = (-slope / SCALE) * (r - kk).astype(F32)
        bias_sc[0] = bias
        for dblk in range(per_q):
            bias_sc[1 + dblk] = jnp.where(r >= kk + dblk * tk, bias, NEG)

    qs =[q_ref[:, c * HEAD_DIM:(c + 1) * HEAD_DIM] for c in range(2)]
    to_log2 = SCALE * LOG2E

    def shift_of(kj):
        return (slope * LOG2E) * ((base - kj) * tk).astype(F32)

    def max_chunk(kj0, n, ms):
        ms = list(ms)
        for u in range(n):
            kj = kj0 + u
            start = pl.multiple_of(kj * tk, tk)
            b = bias_sc[jnp.maximum(kj - base + 1, 0)]
            for comp in range(2):
                k = k_ref[pl.ds(start, tk), comp * HEAD_DIM:(comp + 1) * HEAD_DIM]
                a = (lax.dot_general(k, qs[comp], CONTRACT_LAST, preferred_element_type=F32) + b) * to_log2
                s_sc[comp, pl.ds(start, tk), :] = a
                ms[comp] = jnp.maximum(ms[comp], jnp.max(a, axis=0, keepdims=True) - shift_of(kj))
        return tuple(ms)

    def exp_chunk(kj0, n, ls):
        ls = list(ls)
        start0 = pl.multiple_of(kj0 * tk, tk)
        vt = vt_sc[:, pl.ds(start0, n * tk)]
        for comp in range(2):
            es = []
            for u in range(n):
                start = pl.multiple_of((kj0 + u) * tk, tk)
                e = jnp.exp2(s_sc[comp, pl.ds(start, tk), :] - (ms[comp] + shift_of(kj0 + u)))
                ls[comp] = ls[comp] + jnp.sum(e, axis=0, keepdims=True)
                es.append(e.astype(BF16))
            e_cat = es[0] if n == 1 else jnp.concatenate(es, axis=0)
            acc_sc[comp] += jnp.dot(vt, e_cat, preferred_element_type=F32)
        return tuple(ls)

    def over_blocks(chunk_fn, carry):
        done = 0
        for n in DIFF_UNROLL:
            if n % per_q:
                continue
            n_it = (base + per_q - done) // n
            carry = lax.fori_loop(
                0, n_it, lambda t, c, n=n, done=done: chunk_fn(done + t * n, n, c), carry)
            done = done + n_it * n
        return carry

    neg = jnp.full((1, tq), NEG, F32)
    ms = over_blocks(max_chunk, (neg, neg))
    acc_sc[...] = jnp.zeros(acc_sc.shape, F32)
    zero = jnp.zeros((1, tq), F32)
    l1, l2 = over_blocks(exp_chunk, (zero, zero))

    lp = lam_ref[...]
    lam_full = (jnp.exp(jnp.sum(lp[0:1] * lp[1:2], axis=-1, keepdims=True))
                - jnp.exp(jnp.sum(lp[2:3] * lp[3:4], axis=-1, keepdims=True)) + lam_init)
    y_t = acc_sc[0] * (1.0 / l1) - lam_full * (acc_sc[1] * (1.0 / l2))
    ms2 = jnp.mean(y_t * y_t, axis=0, keepdims=True)
    y = (y_t * lax.rsqrt(ms2 + EPS)).T
    y = y * g_ref[...] * (1.0 - lam_init)
    o_ref[...] = (y * z_ref[...]).astype(o_ref.dtype)


def diff_attention(cp, slopes, lam_l, subln_g, z, *, seq, qcol, kcol, vcol, zcol, tq, tk, lam_init):
    t = cp.shape[0]
    n_batch = t // seq
    nq = seq // tq
    hw = 2 * HEAD_DIM
    kern = functools.partial(_diff_kernel, tq=tq, tk=tk, seq=seq, lam_init=lam_init)
    return pl.pallas_call(
        kern,
        out_shape=jax.ShapeDtypeStruct((t, C_OUT), BF16),
        grid=(n_batch, C_HEADS, nq),
        in_specs=[
            pl.BlockSpec(memory_space=pltpu.SMEM),
            pl.BlockSpec((tq, hw), lambda b, h, i: (b * nq + i, qcol // hw + h)),
            pl.BlockSpec((seq, hw), lambda b, h, i: (b, kcol // hw + h)),
            pl.BlockSpec((seq, hw), lambda b, h, i: (b, vcol // hw + h)),
            pl.BlockSpec((4, HEAD_DIM), lambda b, h, i: (0, 0)),
            pl.BlockSpec((1, hw), lambda b, h, i: (0, 0)),
            pl.BlockSpec((tq, hw), lambda b, h, i: (b * nq + i, zcol // hw + h)),
        ],
        out_specs=pl.BlockSpec((tq, hw), lambda b, h, i: (b * nq + i, h)),
        scratch_shapes=[pltpu.VMEM((hw, seq), BF16), pltpu.VMEM((2, hw, tq), F32),
                        pltpu.VMEM((1 + tq // tk, tk, tq), F32), pltpu.VMEM((2, seq, tq), F32)],
        compiler_params=_params(3),
        name="diff_attention",
    )(slopes, cp, cp, cp, lam_l, subln_g.reshape(1, hw), z)


def _mem_kernel(q_ref, k_ref, v_ref, z_ref, o_ref):
    for h in range(M_HEADS):
        hs = slice(h * HEAD_DIM, (h + 1) * HEAD_DIM)
        s_t = lax.dot_general(k_ref[:, hs], q_ref[:, hs], CONTRACT_LAST,
                              preferred_element_type=F32) * (SCALE * LOG2E)
        m = jnp.max(s_t, axis=0, keepdims=True)
        e = jnp.exp2(s_t - m)
        p_t = (e * (1.0 / jnp.sum(e, axis=0, keepdims=True))).astype(BF16)
        o = lax.dot_general(p_t, v_ref[:, hs], CONTRACT_FIRST, preferred_element_type=F32)
        o_ref[:, hs] = (o * z_ref[:, hs]).astype(o_ref.dtype)


def mem_attention(cp, mkv, z, *, seq, qcol, zcol, tq):
    t = cp.shape[0]
    nq = seq // tq
    return pl.pallas_call(
        _mem_kernel,
        out_shape=jax.ShapeDtypeStruct((t, M_OUT), BF16),
        grid=(t // tq,),
        in_specs=[
            pl.BlockSpec((tq, M_OUT), lambda i: (i, qcol // M_OUT)),
            pl.BlockSpec((N_MEM, M_OUT), lambda i: (i // nq, 0)),
            pl.BlockSpec((N_MEM, M_OUT), lambda i: (i // nq, 1)),
            pl.BlockSpec((tq, M_OUT), lambda i: (i, zcol // M_OUT)),
        ],
        out_specs=pl.BlockSpec((tq, M_OUT), lambda i: (i, 0)),
        compiler_params=_params(1),
        name="mem_attention",
    )(cp, mkv, mkv, z)


_BRANCH_ROWS = ((0, A_OUT), (A_OUT, A_OUT + B_OUT), (A_OUT + B_OUT, A_OUT + B_OUT + C_OUT),
                (A_OUT + B_OUT + C_OUT, A_OUT + B_OUT + C_OUT + M_OUT))
_WB_CHUNK = 512


def _merge_kernel(*refs):
    h_ref, us = refs[0], refs[1:5]
    wgs, bgs = refs[5:9], refs[9:13]
    n_chunk = (A_OUT + B_OUT + C_OUT + M_OUT) // _WB_CHUNK
    wbs = refs[13:13 + n_chunk]
    o_ref, wg_sc, wb_sc = refs[13 + n_chunk:]

    @pl.when(pl.program_id(1) == 0)
    def _():
        for b in range(N_BRANCH):
            wg_sc[b] = wgs[b][...].astype(BF16)
        for c in range(n_chunk):
            wb_sc[c * _WB_CHUNK:(c + 1) * _WB_CHUNK, :] = wbs[c][...].astype(BF16)

    h = h_ref[...]
    acc = None
    for b in range(N_BRANCH):
        lo, hi = _BRANCH_ROWS[b]
        gate = _sigmoid(jnp.dot(h, wg_sc[b], preferred_element_type=F32) + bgs[b][...])
        term = gate * jnp.dot(us[b][...], wb_sc[lo:hi, :], preferred_element_type=F32)
        acc = term if acc is None else acc + term
    o_ref[...] = acc.astype(o_ref.dtype)


def gated_merge(h, us, w_in, b_gate3, w_branch, layer, tm, tn):
    t, d = h.shape
    nj = d // tn
    gate0 = O_GATE // tn
    n_chunk = w_branch.shape[1] // _WB_CHUNK
    in_specs = [pl.BlockSpec((tm, d), lambda j, i: (i, 0))]
    in_specs += [pl.BlockSpec((tm, u.shape[1]), lambda j, i: (i, 0)) for u in us]
    in_specs += [pl.BlockSpec((None, d, tn), functools.partial(
        lambda j, i, b: (layer, 0, gate0 + b * nj + j), b=b)) for b in range(N_BRANCH)]
    in_specs += [pl.BlockSpec((None, 1, tn), functools.partial(
        lambda j, i, b: (layer, 0, b * nj + j), b=b)) for b in range(N_BRANCH)]
    in_specs += [pl.BlockSpec((None, _WB_CHUNK, tn), functools.partial(
        lambda j, i, c: (layer, c, j), c=c)) for c in range(n_chunk)]
    return pl.pallas_call(
        _merge_kernel,
        out_shape=jax.ShapeDtypeStruct((t, d), BF16),
        grid=(nj, t // tm),
        in_specs=in_specs,
        out_specs=pl.BlockSpec((tm, tn), lambda j, i: (i, j)),
        scratch_shapes=[pltpu.VMEM((N_BRANCH, d, tn), BF16),
                        pltpu.VMEM((w_branch.shape[1], tn), BF16)],
        compiler_params=_params(2),
        name="gated_merge",
    )(h, *us, *([w_in] * N_BRANCH), *([b_gate3] * N_BRANCH), *([w_branch] * n_chunk))


def _out_kernel(x_ref, m_ref, w_ref, o_ref, w_sc):
    @pl.when(pl.program_id(1) == 0)
    def _():
        w_sc[...] = w_ref[...].astype(BF16)

    o_ref[...] = x_ref[...] + jnp.dot(m_ref[...], w_sc[...], preferred_element_type=F32)


def out_proj(x, merged, w_out, layer, tm, tn):
    t, d = x.shape
    return pl.pallas_call(
        _out_kernel,
        out_shape=jax.ShapeDtypeStruct((t, d), F32),
        grid=(d // tn, t // tm),
        in_specs=[pl.BlockSpec((tm, tn), lambda j, i: (i, j)),
                  pl.BlockSpec((tm, d), lambda j, i: (i, 0)),
                  pl.BlockSpec((None, d, tn), lambda j, i: (layer, 0, j))],
        out_specs=pl.BlockSpec((tm, tn), lambda j, i: (i, j)),
        scratch_shapes=[pltpu.VMEM((d, tn), BF16)],
        compiler_params=_params(2),
        name="out_proj",
    )(x, merged, w_out)


def kernel(x, mem, norm_g, w_in, b_gate, qk_gain, sinks, lam, subln_g, mem_norm_g,
           w_mem_kv, w_branch, w_out):
    bsz, s_len, d = x.shape
    depth = w_in.shape[0]
    t = bsz * s_len
    xf = x.reshape(t, d)
    memf = mem.reshape(bsz * N_MEM, d)
    b_gate3 = b_gate.reshape(depth, 1, N_BRANCH * d)

    slopes_a = _alibi_slopes(A_HEADS_PER_GROUP)
    slopes_b = _alibi_slopes(B_Q_HEADS)
    slopes_c = jnp.asarray(_alibi_slopes(C_HEADS), F32)
    ones = lambda n: jnp.ones((n,), F32)
    dils = tuple(dil for _, dil in A_PATTERNS)

    pc = PROJ_PIECE
    z_zb, z_zc, z_za, z_zm = 0, B_OUT, B_OUT + C_OUT, B_OUT + C_OUT + A_OUT
    bc_bq, bc_bk, bc_bv, bc_mq = 0, B_OUT, B_OUT + B_KV_HEADS * HEAD_DIM, B_OUT + 2 * B_KV_HEADS * HEAD_DIM
    bc_cq = bc_mq + M_OUT
    bc_ck, bc_cv = bc_cq + C_OUT, bc_cq + 2 * C_OUT

    for l in range(depth):
        gq = qk_gain[l]
        gain_a = jnp.concatenate([jnp.tile(gq[QK_A_Q], 4), jnp.tile(gq[QK_A_K], 4), ones(A_OUT)]).reshape(1, -1)
        gain_bc = jnp.concatenate([
            jnp.tile(gq[QK_B_Q], B_Q_HEADS), jnp.tile(gq[QK_B_K], B_KV_HEADS), ones(B_KV_HEADS * HEAD_DIM),
            jnp.tile(gq[QK_M_Q], M_HEADS), jnp.tile(gq[QK_C_Q], 2 * C_HEADS),
            jnp.tile(gq[QK_C_K], 2 * C_HEADS), ones(C_OUT)]).reshape(1, -1)
        gain_z = ones(O_GATE - O_ZA).reshape(1, -1)
        gain_m = jnp.concatenate([jnp.tile(gq[QK_M_K], M_HEADS), ones(M_OUT)]).reshape(1, -1)

        hs = rmsnorm_bf16(xf, norm_g[l], 256, dils[1:])
        h = hs[0]
        h_perm = [h] + [hp.reshape(t, d) for hp in hs[1:]]

        outs, lses = [], []
        for g, (win, dil) in enumerate(A_PATTERNS):
            apg = proj(h_perm[g], w_in, l, [(O_AQ // pc + g,), (O_AK // pc + g,), (O_AV // pc + g,)],
                       gain_a, ("NNNNNNNNPPPP",), 1024, BF16, f"proj_a{g}")
            rows = s_len // dil
            o_g, lse_g = banded_attention(
                apg, rows_per_seq=rows, qcol=0, kcol=A_OUT, vcol=2 * A_OUT,
                n_q=A_HEADS_PER_GROUP, n_kv=A_HEADS_PER_GROUP, slopes=slopes_a,
                max_dist=win // dil, dist_scale=dil, tq=1024,
                out_dtype=F32, unit=1, write_lse=True, name=f"attn_a{g}")
            outs.append(o_g)
            lses.append(lse_g)

        bc = proj(h, w_in, l,
                  [(O_BQ // pc, O_BK // pc, O_CQ // pc, O_CK // pc, O_CV // pc),
                   (O_BQ // pc + 1, O_MQ // pc, O_CQ // pc + 1, O_CK // pc + 1, O_CV // pc + 1)],
                  gain_bc, ("NNNNNNNN", "NNPPNNNN", "NNNNNNNN", "NNNNNNNN", "PPPPPPPP"), 1024, BF16, "proj_bc")
        sz = proj(h, w_in, l, [(O_ZB // pc, O_ZC // pc, O_ZA // pc), (O_ZB // pc + 1, O_ZC // pc + 1, O_ZM // pc)],
                  gain_z, ("SSSSSSSS",) * 3, 1024, F32, "proj_z")

        u_a = combine_a(outs, lses, sz, z_za, 512, dils[1:])

        u_b = banded_attention(
            bc, rows_per_seq=s_len, qcol=bc_bq, kcol=bc_bk, vcol=bc_bv,
            n_q=B_Q_HEADS, n_kv=B_KV_HEADS, slopes=slopes_b, max_dist=B_WINDOW - 1, dist_scale=1,
            tq=1024, out_dtype=BF16, batch=1, sinks=sinks[l], z=sz, zcol=z_zb, name="attn_b")

        lam_init = 0.8 - 0.6 * math.exp(-0.3 * l)
        u_c = diff_attention(bc, slopes_c, lam[l], subln_g[l], sz, seq=s_len,
                             qcol=bc_cq, kcol=bc_ck, vcol=bc_cv, zcol=z_zc, tq=512, tk=256, lam_init=lam_init)

        mn = rmsnorm_bf16(memf, mem_norm_g[l], 256)[0]
        mkv = proj(mn, w_mem_kv, l, [(0,), (1,)], gain_m, ("NNNNPPPP",), bsz * N_MEM, BF16, "proj_mem")
        u_m = mem_attention(bc, mkv, sz, seq=s_len, qcol=bc_mq, zcol=z_zm, tq=512)

        merged = gated_merge(h, [u_a, u_b, u_c, u_m], w_in, b_gate3, w_branch, l, 1024, 256)
        xf = out_proj(xf, merged, w_out, l, 1024, 1024)

    return xf.reshape(bsz, s_len, d)
```

```python
import functools
import math

import jax
import jax.numpy as jnp
from jax import lax
from jax.experimental import pallas as pl
from jax.experimental.pallas import tpu as pltpu

D_MODEL = 2048
HEAD_DIM = 128
BLOCK = 128
EPS = 1e-6
A_PATTERNS = ((128, 1), (512, 4), (2048, 16))
A_GROUPS = 3
A_HEADS_PER_GROUP = 4
A_HEADS = A_GROUPS * A_HEADS_PER_GROUP
A_OUT = A_HEADS_PER_GROUP * HEAD_DIM
B_Q_HEADS = 8
B_KV_HEADS = 2
B_WINDOW = 128
B_OUT = B_Q_HEADS * HEAD_DIM
C_HEADS = 4
C_OUT = C_HEADS * 2 * HEAD_DIM
N_MEM = 256
M_HEADS = 4
M_OUT = M_HEADS * HEAD_DIM
N_BRANCH = 4
QK_A_Q, QK_A_K, QK_B_Q, QK_B_K, QK_C_Q, QK_C_K, QK_M_Q, QK_M_K = range(8)

_IN_SIZES = (A_HEADS * HEAD_DIM, A_HEADS * HEAD_DIM, A_HEADS * HEAD_DIM,
             B_Q_HEADS * HEAD_DIM, B_KV_HEADS * HEAD_DIM, B_KV_HEADS * HEAD_DIM,
             C_HEADS * 2 * HEAD_DIM, C_HEADS * 2 * HEAD_DIM, C_OUT,
             M_HEADS * HEAD_DIM, A_OUT, B_OUT, C_OUT, M_OUT, N_BRANCH * D_MODEL)
_OFF = [0]
for _s in _IN_SIZES:
    _OFF.append(_OFF[-1] + _s)
(O_AQ, O_AK, O_AV, O_BQ, O_BK, O_BV, O_CQ, O_CK, O_CV, O_MQ,
 O_ZA, O_ZB, O_ZC, O_ZM, O_GATE, O_END) = _OFF

NEG = -1e30
SCALE = HEAD_DIM ** -0.5
VMEM_LIMIT = 58 * 1024 * 1024
PROJ_PIECE = 512
DIFF_UNROLL = (8, 6, 4, 2, 1)
LOG2E = math.log2(math.e)
BF16 = jnp.bfloat16
F32 = jnp.float32
CONTRACT_LAST = (((1,), (1,)), ((), ()))
CONTRACT_FIRST = (((0,), (0,)), ((), ()))


def _params(n_axes):
    return pltpu.CompilerParams(dimension_semantics=("arbitrary",) * n_axes,
                                vmem_limit_bytes=VMEM_LIMIT)


def _sigmoid(x):
    return 0.5 * jnp.tanh(0.5 * x) + 0.5


def _alibi_slopes(n):
    return [2.0 ** (-8.0 * i / n) for i in range(1, n + 1)]


def _rmsnorm_kernel(x_ref, g_ref, o_ref, *perm_refs, dils, tm):
    x = x_ref[...]
    ms = jnp.mean(x * x, axis=-1, keepdims=True)
    h = (x * lax.rsqrt(ms + EPS) * g_ref[...]).astype(BF16)
    o_ref[...] = h
    row = lax.broadcasted_iota(jnp.int32, (tm, tm), 0)
    col = lax.broadcasted_iota(jnp.int32, (tm, tm), 1)
    for d, p_ref in zip(dils, perm_refs):
        n = tm // d
        src = (row % n) * d + row // n
        perm = jnp.where(col == src, 1.0, 0.0).astype(BF16)
        hp = jnp.dot(perm, h, preferred_element_type=F32).astype(BF16)
        for p in range(d):
            p_ref[p] = hp[p * n:(p + 1) * n, :]


def rmsnorm_bf16(x, g, tm, dils=()):
    t, d = x.shape
    out_shape = [jax.ShapeDtypeStruct((t, d), BF16)]
    out_specs = [pl.BlockSpec((tm, d), lambda i: (i, 0))]
    for dil in dils:
        out_shape.append(jax.ShapeDtypeStruct((dil, t // dil, d), BF16))
        out_specs.append(pl.BlockSpec((dil, tm // dil, d), lambda i: (0, i, 0)))
    outs = pl.pallas_call(
        functools.partial(_rmsnorm_kernel, dils=tuple(dils), tm=tm),
        out_shape=out_shape,
        grid=(t // tm,),
        in_specs=[pl.BlockSpec((tm, d), lambda i: (i, 0)),
                  pl.BlockSpec((1, d), lambda i: (0, 0))],
        out_specs=out_specs,
        compiler_params=_params(1),
        name="rmsnorm",
    )(x, g.reshape(1, d))
    return outs


def _proj_kernel(*refs, tile_modes, n_piece):
    h_ref, w_refs = refs[0], refs[1:1 + n_piece]
    gain_ref, o_ref, w_sc = refs[1 + n_piece:]
    _proj_body(h_ref, w_refs, gain_ref, o_ref, w_sc, tile_modes)


def _proj_body(h_ref, w_refs, gain_ref, o_ref, w_sc, tile_modes):
    n_piece = len(w_refs)
    j = pl.program_id(0)

    @pl.when(pl.program_id(1) == 0)
    def _():
        for p in range(n_piece):
            w_sc[:, p * PROJ_PIECE:(p + 1) * PROJ_PIECE] = w_refs[p][...].astype(BF16)

    acc = jnp.dot(h_ref[...], w_sc[...], preferred_element_type=F32)
    for c in range(len(tile_modes[0])):
        sl = slice(c * HEAD_DIM, (c + 1) * HEAD_DIM)
        modes = [m[c] for m in tile_modes]
        y = acc[:, sl]
        if 'S' in modes:
            assert set(modes) == {'S'}
            y = y * _sigmoid(y)
        elif 'N' in modes:
            ms = jnp.mean(y * y, axis=-1, keepdims=True)
            normed = y * lax.rsqrt(ms + EPS) * gain_ref[:, sl]
            if set(modes) == {'N'}:
                y = normed
            else:
                is_norm = functools.reduce(jnp.logical_or, [j == jj for jj, m in enumerate(modes) if m == 'N'])
                y = jnp.where(is_norm, normed, y)
        o_ref[:, sl] = y.astype(o_ref.dtype)


def _select(j, values):
    out = values[-1]
    for jj in range(len(values) - 2, -1, -1):
        out = jnp.where(j == jj, values[jj], out)
    return out


def proj(h, w3, layer, piece_cols, gain, tile_modes, tm, out_dtype, name):
    t, k = h.shape
    n_piece = len(piece_cols)
    tn = n_piece * PROJ_PIECE
    n_tiles = len(tile_modes)
    assert t % tm == 0 and all(len(m) == tn // HEAD_DIM for m in tile_modes)
    assert all(len(cols) == n_tiles for cols in piece_cols) and gain.shape == (1, tn * n_tiles)
    w_specs = [pl.BlockSpec((None, k, PROJ_PIECE),
                            functools.partial(lambda j, i, cols: (layer, 0, _select(j, cols)), cols=tuple(cols)))
               for cols in piece_cols]
    return pl.pallas_call(
        functools.partial(_proj_kernel, tile_modes=tuple(tile_modes), n_piece=n_piece),
        out_shape=jax.ShapeDtypeStruct((t, tn * n_tiles), out_dtype),
        grid=(n_tiles, t // tm),
        in_specs=[pl.BlockSpec((tm, k), lambda j, i: (i, 0))] + w_specs
                 + [pl.BlockSpec((1, tn), lambda j, i: (0, j))],
        out_specs=pl.BlockSpec((tm, tn), lambda j, i: (i, j)),
        scratch_shapes=[pltpu.VMEM((k, tn), BF16)],
        compiler_params=_params(2),
        name=name,
    )(h, *([w3] * n_piece), gain)


def _banded_body(q_ref, kc_ref, kp_ref, vc_ref, vp_ref, sink_ref, z_ref, o_ref, lse_ref, *, tile,
                 n_q, n_kv, slopes, max_dist, dist_scale, tq, rows_per_seq, unit, batch, between=None):
    has_sink, mul_z, write_lse = sink_ref is not None, z_ref is not None, lse_ref is not None
    shared_kv = n_kv < n_q and unit > 1
    group = n_q // n_kv
    assert n_q % unit == 0 and (not shared_kv or group == unit)
    kk = lax.broadcasted_iota(jnp.int32, (2 * BLOCK, BLOCK), 0)
    r = lax.broadcasted_iota(jnp.int32, (2 * BLOCK, BLOCK), 1)
    dist = r + BLOCK - kk
    distf = (dist * dist_scale).astype(F32)
    valid = (dist >= 0) & (dist <= max_dist)
    valid_start = valid & (kk >= BLOCK)
    if rows_per_seq > tq:
        has_prev = jnp.minimum(tile % (rows_per_seq // tq), 1) * BLOCK
        valid0 = valid & (kk + has_prev >= BLOCK)
    else:
        valid0 = valid_start

    def window(cur_ref, prev_ref, kv, j):
        ks = slice(kv * HEAD_DIM, (kv + 1) * HEAD_DIM)
        prev = prev_ref[:, ks] if j == 0 else cur_ref[(j - 1) * BLOCK:j * BLOCK, ks]
        return jnp.concatenate([prev, cur_ref[j * BLOCK:(j + 1) * BLOCK, ks]], axis=0)

    units = [(list(range(u * unit, (u + 1) * unit)), j)
             for u in range(n_q // unit) for j in range(tq // BLOCK)]

    for b0 in range(0, len(units), batch or len(units)):
        batch_units = units[b0:b0 + (batch or len(units))]
        scores = []
        for heads, j in batch_units:
            rows = slice(j * BLOCK, (j + 1) * BLOCK)
            qs = [q_ref[rows, h * HEAD_DIM:(h + 1) * HEAD_DIM] for h in heads]
            if shared_kv:
                s_t = lax.dot_general(window(kc_ref, kp_ref, heads[0] // group, j), jnp.concatenate(qs, axis=0),
                                      CONTRACT_LAST, preferred_element_type=F32)
            else:
                s_t = jnp.concatenate(
                    [lax.dot_general(window(kc_ref, kp_ref, h // group, j), q, CONTRACT_LAST,
                                     preferred_element_type=F32) for h, q in zip(heads, qs)], axis=1)
            scores.append(s_t)
        if between is not None and b0 == 0:
            between()

        probs, lses = [], []
        for (heads, j), s_t in zip(batch_units, scores):
            v_mask = valid0 if j == 0 else (valid_start if (j * BLOCK) % rows_per_seq == 0 else valid)
            bias = jnp.concatenate([jnp.where(v_mask, (-slopes[h] * LOG2E) * distf, NEG) for h in heads], axis=1)
            s_t = s_t * (SCALE * LOG2E) + bias
            m = jnp.max(s_t, axis=0, keepdims=True)
            if has_sink:
                sink = jnp.concatenate([jnp.full((1, BLOCK), sink_ref[h] * LOG2E, F32) for h in heads], axis=1)
                m = jnp.maximum(m, sink)
            e = jnp.exp2(s_t - m)
            denom = jnp.sum(e, axis=0, keepdims=True)
            if has_sink:
                denom = denom + jnp.exp2(sink - m)
            probs.append((e * (1.0 / denom)).astype(BF16))
            lses.append(m + jnp.log2(denom) if write_lse else None)

        for (heads, j), p_t, lse in zip(batch_units, probs, lses):
            rows = slice(j * BLOCK, (j + 1) * BLOCK)
            if shared_kv:
                o_all = lax.dot_general(p_t, window(vc_ref, vp_ref, heads[0] // group, j), CONTRACT_FIRST,
                                        preferred_element_type=F32)
            for g, h in enumerate(heads):
                hs = slice(h * HEAD_DIM, (h + 1) * HEAD_DIM)
                gs = slice(g * BLOCK, (g + 1) * BLOCK)
                if shared_kv:
                    o = o_all[gs, :]
                else:
                    o = lax.dot_general(p_t[:, gs], window(vc_ref, vp_ref, h // group, j), CONTRACT_FIRST,
                                        preferred_element_type=F32)
                if mul_z:
                    o = o * z_ref[rows, hs]
                o_ref[rows, hs] = o.astype(o_ref.dtype)
                if write_lse:
                    lse_ref[rows, hs] = jnp.broadcast_to(lse[:, gs], (BLOCK, BLOCK)).T


def _banded_specs(q_arr, kv_arr, *, qcol, kcol, vcol, n_q, n_kv, tq, out_dtype, sinks, z, zcol, write_lse,
                  tile_of):
    rows_total = q_arr.shape[0]
    qw, kw = n_q * HEAD_DIM, n_kv * HEAD_DIM
    assert qcol % qw == 0 and kcol % kw == 0 and vcol % kw == 0
    sub = tq // BLOCK

    def cur(col, w):
        return lambda *g: (tile_of(*g), col // w)

    def prev(col, w):
        return lambda *g: (jnp.maximum(tile_of(*g) * sub - 1, 0), col // w)

    in_specs = [
        pl.BlockSpec((tq, qw), cur(qcol, qw)),
        pl.BlockSpec((tq, kw), cur(kcol, kw)),
        pl.BlockSpec((BLOCK, kw), prev(kcol, kw)),
        pl.BlockSpec((tq, kw), cur(vcol, kw)),
        pl.BlockSpec((BLOCK, kw), prev(vcol, kw)),
    ]
    args = [q_arr, kv_arr, kv_arr, kv_arr, kv_arr]
    if sinks is not None:
        in_specs.append(pl.BlockSpec(memory_space=pltpu.SMEM))
        args.append(sinks)
    if z is not None:
        assert zcol % qw == 0
        in_specs.append(pl.BlockSpec((tq, qw), cur(zcol, qw)))
        args.append(z)
    out_block = pl.BlockSpec((tq, qw), cur(0, qw))
    out_shape = [jax.ShapeDtypeStruct((rows_total, qw), out_dtype)]
    out_specs = [out_block]
    if write_lse:
        out_shape.append(jax.ShapeDtypeStruct((rows_total, qw), F32))
        out_specs.append(out_block)
    return in_specs, args, out_shape, out_specs


def _banded_cfg(*, rows_per_seq, n_q, n_kv, slopes, max_dist, dist_scale, tq, unit, batch):
    assert tq % rows_per_seq == 0 or rows_per_seq % tq == 0
    return dict(n_q=n_q, n_kv=n_kv, slopes=tuple(slopes), max_dist=max_dist, dist_scale=dist_scale,
                tq=tq, rows_per_seq=rows_per_seq, unit=unit, batch=batch)


def _proj_fused_kernel(*refs, tile_mode, n_piece, n_in, n_out, body):
    h_ref, w_refs, gain_ref = refs[0], refs[1:1 + n_piece], refs[1 + n_piece]
    extra_in = refs[2 + n_piece:2 + n_piece + n_in]
    p_out = refs[2 + n_piece + n_in]
    extra_out = refs[3 + n_piece + n_in:3 + n_piece + n_in + n_out]
    w_sc = refs[3 + n_piece + n_in + n_out]
    body(extra_in, extra_out, pl.program_id(1),
         lambda: _proj_body(h_ref, w_refs, gain_ref, p_out, w_sc, (tile_mode,)))


def proj_fused(h, w3, layer, piece_cols, gain, tile_mode, tm, out_dtype, extra, name):
    t, k = h.shape
    n_piece = len(piece_cols)
    tn = n_piece * PROJ_PIECE
    assert len(tile_mode) == tn // HEAD_DIM and gain.shape == (1, tn)
    e_specs, e_args, e_shape, e_out_specs, body = extra
    w_specs = [pl.BlockSpec((None, k, PROJ_PIECE), functools.partial(lambda j, i, c: (layer, 0, c), c=c))
               for c in piece_cols]
    return pl.pallas_call(
        functools.partial(_proj_fused_kernel, tile_mode=tile_mode, n_piece=n_piece,
                          n_in=len(e_specs), n_out=len(e_shape), body=body),
        out_shape=[jax.ShapeDtypeStruct((t, tn), out_dtype)] + list(e_shape),
        grid=(1, t // tm),
        in_specs=[pl.BlockSpec((tm, k), lambda j, i: (i, 0))] + w_specs
                 + [pl.BlockSpec((1, tn), lambda j, i: (0, 0))] + list(e_specs),
        out_specs=[pl.BlockSpec((tm, tn), lambda j, i: (i, 0))] + list(e_out_specs),
        scratch_shapes=[pltpu.VMEM((k, tn), BF16)],
        compiler_params=_params(2),
        name=name,
    )(h, *([w3] * n_piece), gain, *e_args)


def banded_extra(q_arr, kv_arr, *, rows_per_seq, qcol, kcol, vcol, n_q, n_kv, slopes, max_dist, dist_scale,
                 tq, out_dtype, unit=4, batch=None, sinks=None, z=None, zcol=0, write_lse=False):
    in_specs, args, out_shape, out_specs = _banded_specs(
        q_arr, kv_arr, qcol=qcol, kcol=kcol, vcol=vcol, n_q=n_q, n_kv=n_kv, tq=tq, out_dtype=out_dtype,
        sinks=sinks, z=z, zcol=zcol, write_lse=write_lse, tile_of=lambda j, i: i)
    cfg = _banded_cfg(rows_per_seq=rows_per_seq, n_q=n_q, n_kv=n_kv, slopes=slopes, max_dist=max_dist,
                      dist_scale=dist_scale, tq=tq, unit=unit, batch=batch)
    has_sink, mul_z = sinks is not None, z is not None

    def body(in_refs, out_refs, tile, proj_fn):
        in_refs = list(in_refs)
        sink_ref = in_refs[5] if has_sink else None
        z_ref = in_refs[5 + has_sink] if mul_z else None
        lse_ref = out_refs[1] if write_lse else None
        _banded_body(*in_refs[:5], sink_ref, z_ref, out_refs[0], lse_ref, tile=tile, between=proj_fn, **cfg)

    return in_specs, args, out_shape, out_specs, body


def _combine_kernel(o0, o1, o2, l0, l1, l2, z_ref, u_ref, o1_sc, o2_sc, l1_sc, l2_sc, *, tm, dils):
    n_chunk = o0.shape[1] // HEAD_DIM
    for src, dst, d in ((o1, o1_sc, dils[0]), (l1, l1_sc, dils[0]),
                        (o2, o2_sc, dils[1]), (l2, l2_sc, dils[1])):
        n = tm // d
        for p in range(d):
            for c in range(n_chunk):
                dst[c, pl.ds(p, n, stride=d), :] = src[p, :, c * HEAD_DIM:(c + 1) * HEAD_DIM]
    for c in range(n_chunk):
        cs = slice(c * HEAD_DIM, (c + 1) * HEAD_DIM)
        a0, a1, a2 = l0[:, cs], l1_sc[c], l2_sc[c]
        m = jnp.maximum(jnp.maximum(a0, a1), a2)
        e0, e1, e2 = jnp.exp2(a0 - m), jnp.exp2(a1 - m), jnp.exp2(a2 - m)
        inv = 1.0 / (e0 + e1 + e2)
        ya = (e0 * inv) * o0[:, cs] + (e1 * inv) * o1_sc[c] + (e2 * inv) * o2_sc[c]
        u_ref[:, cs] = (ya * z_ref[:, cs]).astype(u_ref.dtype)


def combine_a(outs, lses, z, zcol, tm, dils):
    t, w = outs[0].shape
    blk = pl.BlockSpec((tm, w), lambda i: (i, 0))

    def pblk(d):
        return pl.BlockSpec((d, tm // d, w), lambda i: (0, i, 0))

    def p3(a, d):
        return a.reshape(d, t // d, w)

    d1, d2 = dils
    return pl.pallas_call(
        functools.partial(_combine_kernel, tm=tm, dils=dils),
        out_shape=jax.ShapeDtypeStruct((t, w), BF16),
        grid=(t // tm,),
        in_specs=[blk, pblk(d1), pblk(d2), blk, pblk(d1), pblk(d2),
                  pl.BlockSpec((tm, w), lambda i: (i, zcol // w))],
        out_specs=blk,
        scratch_shapes=[pltpu.VMEM((w // HEAD_DIM, tm, HEAD_DIM), F32)] * 4,
        compiler_params=_params(1),
        name="combine_a",
    )(outs[0], p3(outs[1], d1), p3(outs[2], d2), lses[0], p3(lses[1], d1), p3(lses[2], d2), z)


def _diff_kernel(slope_ref, q_ref, k_ref, v_ref, lam_ref, g_ref, z_ref, o_ref,
                 vt_sc, acc_sc, bias_sc, s_sc, *, tq, tk, seq, lam_init):
    h = pl.program_id(1)
    qi = pl.program_id(2)
    slope = slope_ref[h]
    per_q = tq // tk
    base = qi * per_q

    @pl.when(qi == 0)
    def _():
        for cidx in range(seq // tk):
            rows = slice(cidx * tk, (cidx + 1) * tk)
            vt_sc[:, rows] = v_ref[rows, :].astype(F32).T.astype(BF16)
        kk = lax.broadcasted_iota(jnp.int32, (tk, tq), 0)
        r = lax.broadcasted_iota(jnp.int32, (tk, tq), 1)
        bias = (-slope / SCALE) * (r - kk).astype(F32)
        bias_sc[0] = bias
        for dblk in range(per_q):
            bias_sc[1 + dblk] = jnp.where(r >= kk + dblk * tk, bias, NEG)

    qs = [q_ref[:, c * HEAD_DIM:(c + 1) * HEAD_DIM] for c in range(2)]
    to_log2 = SCALE * LOG2E

    def shift_of(kj):
        return (slope * LOG2E) * ((base - kj) * tk).astype(F32)

    def max_chunk(kj0, n, ms):
        ms = list(ms)
        for u in range(n):
            kj = kj0 + u
            start = pl.multiple_of(kj * tk, tk)
            b = bias_sc[jnp.maximum(kj - base + 1, 0)]
            for comp in range(2):
                k = k_ref[pl.ds(start, tk), comp * HEAD_DIM:(comp + 1) * HEAD_DIM]
                a = (lax.dot_general(k, qs[comp], CONTRACT_LAST, preferred_element_type=F32) + b) * to_log2
                s_sc[comp, pl.ds(start, tk), :] = a
                ms[comp] = jnp.maximum(ms[comp], jnp.max(a, axis=0, keepdims=True) - shift_of(kj))
        return tuple(ms)

    def exp_chunk(kj0, n, ls):
        ls = list(ls)
        start0 = pl.multiple_of(kj0 * tk, tk)
        vt = vt_sc[:, pl.ds(start0, n * tk)]
        for comp in range(2):
            es = []
            for u in range(n):
                start = pl.multiple_of((kj0 + u) * tk, tk)
                e = jnp.exp2(s_sc[comp, pl.ds(start, tk), :] - (ms[comp] + shift_of(kj0 + u)))
                ls[comp] = ls[comp] + jnp.sum(e, axis=0, keepdims=True)
                es.append(e.astype(BF16))
            e_cat = es[0] if n == 1 else jnp.concatenate(es, axis=0)
            acc_sc[comp] += jnp.dot(vt, e_cat, preferred_element_type=F32)
        return tuple(ls)

    def over_blocks(chunk_fn, carry):
        done = 0
        for n in DIFF_UNROLL:
            if n % per_q:
                continue
            n_it = (base + per_q - done) // n
            carry = lax.fori_loop(
                0, n_it, lambda t, c, n=n, done=done: chunk_fn(done + t * n, n, c), carry)
            done = done + n_it * n
        return carry

    neg = jnp.full((1, tq), NEG, F32)
    ms = over_blocks(max_chunk, (neg, neg))
    acc_sc[...] = jnp.zeros(acc_sc.shape, F32)
    zero = jnp.zeros((1, tq), F32)
    l1, l2 = over_blocks(exp_chunk, (zero, zero))

    lp = lam_ref[...]
    lam_full = (jnp.exp(jnp.sum(lp[0:1] * lp[1:2], axis=-1, keepdims=True))
                - jnp.exp(jnp.sum(lp[2:3] * lp[3:4], axis=-1, keepdims=True)) + lam_init)
    y_t = acc_sc[0] * (1.0 / l1) - lam_full * (acc_sc[1] * (1.0 / l2))
    ms2 = jnp.mean(y_t * y_t, axis=0, keepdims=True)
    y = (y_t * lax.rsqrt(ms2 + EPS)).T
    y = y * g_ref[...] * (1.0 - lam_init)
    o_ref[...] = (y * z_ref[...]).astype(o_ref.dtype)


def diff_attention(cq, ck, cv, slopes, lam_l, subln_g, z, *, seq, zcol, tq, tk, lam_init):
    t = cq.shape[0]
    n_batch = t // seq
    nq = seq // tq
    hw = 2 * HEAD_DIM
    kern = functools.partial(_diff_kernel, tq=tq, tk=tk, seq=seq, lam_init=lam_init)
    return pl.pallas_call(
        kern,
        out_shape=jax.ShapeDtypeStruct((t, C_OUT), BF16),
        grid=(n_batch, C_HEADS, nq),
        in_specs=[
            pl.BlockSpec(memory_space=pltpu.SMEM),
            pl.BlockSpec((tq, hw), lambda b, h, i: (b * nq + i, h)),
            pl.BlockSpec((seq, hw), lambda b, h, i: (b, h)),
            pl.BlockSpec((seq, hw), lambda b, h, i: (b, h)),
            pl.BlockSpec((4, HEAD_DIM), lambda b, h, i: (0, 0)),
            pl.BlockSpec((1, hw), lambda b, h, i: (0, 0)),
            pl.BlockSpec((tq, hw), lambda b, h, i: (b * nq + i, zcol // hw + h)),
        ],
        out_specs=pl.BlockSpec((tq, hw), lambda b, h, i: (b * nq + i, h)),
        scratch_shapes=[pltpu.VMEM((hw, seq), BF16), pltpu.VMEM((2, hw, tq), F32),
                        pltpu.VMEM((1 + tq // tk, tk, tq), F32), pltpu.VMEM((2, seq, tq), F32)],
        compiler_params=_params(3),
        name="diff_attention",
    )(slopes, cq, ck, cv, lam_l, subln_g.reshape(1, hw), z)


def _mem_kernel(q_ref, k_ref, v_ref, z_ref, o_ref):
    for h in range(M_HEADS):
        hs = slice(h * HEAD_DIM, (h + 1) * HEAD_DIM)
        s_t = lax.dot_general(k_ref[:, hs], q_ref[:, hs], CONTRACT_LAST,
                              preferred_element_type=F32) * (SCALE * LOG2E)
        m = jnp.max(s_t, axis=0, keepdims=True)
        e = jnp.exp2(s_t - m)
        p_t = (e * (1.0 / jnp.sum(e, axis=0, keepdims=True))).astype(BF16)
        o = lax.dot_general(p_t, v_ref[:, hs], CONTRACT_FIRST, preferred_element_type=F32)
        o_ref[:, hs] = (o * z_ref[:, hs]).astype(o_ref.dtype)


def mem_extra(cp, mkv, z, *, seq, qcol, zcol, tq):
    t = cp.shape[0]
    nq = seq // tq
    in_specs = [
        pl.BlockSpec((tq, M_OUT), lambda j, i: (i, qcol // M_OUT)),
        pl.BlockSpec((N_MEM, M_OUT), lambda j, i: (i // nq, 0)),
        pl.BlockSpec((N_MEM, M_OUT), lambda j, i: (i // nq, 1)),
        pl.BlockSpec((tq, M_OUT), lambda j, i: (i, zcol // M_OUT)),
    ]
    out_shape = [jax.ShapeDtypeStruct((t, M_OUT), BF16)]
    out_specs = [pl.BlockSpec((tq, M_OUT), lambda j, i: (i, 0))]

    def body(in_refs, out_refs, tile, proj_fn):
        proj_fn()
        _mem_kernel(*in_refs, out_refs[0])

    return in_specs, [cp, mkv, mkv, z], out_shape, out_specs, body


_BRANCH_ROWS = ((0, A_OUT), (A_OUT, A_OUT + B_OUT), (A_OUT + B_OUT, A_OUT + B_OUT + C_OUT),
                (A_OUT + B_OUT + C_OUT, A_OUT + B_OUT + C_OUT + M_OUT))
_WB_CHUNK = 512


def _merge_kernel(*refs):
    h_ref, us = refs[0], refs[1:5]
    wgs, bgs = refs[5:9], refs[9:13]
    n_chunk = (A_OUT + B_OUT + C_OUT + M_OUT) // _WB_CHUNK
    wbs = refs[13:13 + n_chunk]
    o_ref, wg_sc, wb_sc = refs[13 + n_chunk:]

    @pl.when(pl.program_id(1) == 0)
    def _():
        for b in range(N_BRANCH):
            wg_sc[b] = wgs[b][...].astype(BF16)
        for c in range(n_chunk):
            wb_sc[c * _WB_CHUNK:(c + 1) * _WB_CHUNK, :] = wbs[c][...].astype(BF16)

    h = h_ref[...]
    acc = None
    for b in range(N_BRANCH):
        lo, hi = _BRANCH_ROWS[b]
        gate = _sigmoid(jnp.dot(h, wg_sc[b], preferred_element_type=F32) + bgs[b][...])
        term = gate * jnp.dot(us[b][...], wb_sc[lo:hi, :], preferred_element_type=F32)
        acc = term if acc is None else acc + term
    o_ref[...] = acc.astype(o_ref.dtype)


def gated_merge(h, us, w_in, b_gate3, w_branch, layer, tm, tn):
    t, d = h.shape
    nj = d // tn
    gate0 = O_GATE // tn
    n_chunk = w_branch.shape[1] // _WB_CHUNK
    in_specs = [pl.BlockSpec((tm, d), lambda j, i: (i, 0))]
    in_specs += [pl.BlockSpec((tm, u.shape[1]), lambda j, i: (i, 0)) for u in us]
    in_specs += [pl.BlockSpec((None, d, tn), functools.partial(
        lambda j, i, b: (layer, 0, gate0 + b * nj + j), b=b)) for b in range(N_BRANCH)]
    in_specs += [pl.BlockSpec((None, 1, tn), functools.partial(
        lambda j, i, b: (layer, 0, b * nj + j), b=b)) for b in range(N_BRANCH)]
    in_specs += [pl.BlockSpec((None, _WB_CHUNK, tn), functools.partial(
        lambda j, i, c: (layer, c, j), c=c)) for c in range(n_chunk)]
    return pl.pallas_call(
        _merge_kernel,
        out_shape=jax.ShapeDtypeStruct((t, d), BF16),
        grid=(nj, t // tm),
        in_specs=in_specs,
        out_specs=pl.BlockSpec((tm, tn), lambda j, i: (i, j)),
        scratch_shapes=[pltpu.VMEM((N_BRANCH, d, tn), BF16),
                        pltpu.VMEM((w_branch.shape[1], tn), BF16)],
        compiler_params=_params(2),
        name="gated_merge",
    )(h, *us, *([w_in] * N_BRANCH), *([b_gate3] * N_BRANCH), *([w_branch] * n_chunk))


def _out_kernel(x_ref, m_ref, w_ref, o_ref, w_sc):
    @pl.when(pl.program_id(1) == 0)
    def _():
        w_sc[...] = w_ref[...].astype(BF16)

    o_ref[...] = x_ref[...] + jnp.dot(m_ref[...], w_sc[...], preferred_element_type=F32)


def out_proj(x, merged, w_out, layer, tm, tn):
    t, d = x.shape
    return pl.pallas_call(
        _out_kernel,
        out_shape=jax.ShapeDtypeStruct((t, d), F32),
        grid=(d // tn, t // tm),
        in_specs=[pl.BlockSpec((tm, tn), lambda j, i: (i, j)),
                  pl.BlockSpec((tm, d), lambda j, i: (i, 0)),
                  pl.BlockSpec((None, d, tn), lambda j, i: (layer, 0, j))],
        out_specs=pl.BlockSpec((tm, tn), lambda j, i: (i, j)),
        scratch_shapes=[pltpu.VMEM((d, tn), BF16)],
        compiler_params=_params(2),
        name="out_proj",
    )(x, merged, w_out)


def kernel(x, mem, norm_g, w_in, b_gate, qk_gain, sinks, lam, subln_g, mem_norm_g,
           w_mem_kv, w_branch, w_out):
    bsz, s_len, d = x.shape
    depth = w_in.shape[0]
    t = bsz * s_len
    xf = x.reshape(t, d)
    memf = mem.reshape(bsz * N_MEM, d)
    b_gate3 = b_gate.reshape(depth, 1, N_BRANCH * d)

    slopes_a = _alibi_slopes(A_HEADS_PER_GROUP)
    slopes_b = _alibi_slopes(B_Q_HEADS)
    slopes_c = jnp.asarray(_alibi_slopes(C_HEADS), F32)
    ones = lambda n: jnp.ones((n,), F32)
    dils = tuple(dil for _, dil in A_PATTERNS)

    pc = PROJ_PIECE
    z_zb, z_zc, z_za, z_zm = 0, B_OUT, B_OUT + C_OUT, B_OUT + C_OUT + A_OUT
    g_bm = B_OUT
    g_cq = g_bm + 2 * B_KV_HEADS * HEAD_DIM + M_OUT
    g_ck, g_cv = g_cq + C_OUT, g_cq + 2 * C_OUT

    for l in range(depth):
        gq = qk_gain[l]
        gain_a = jnp.concatenate([jnp.tile(gq[QK_A_Q], 4), jnp.tile(gq[QK_A_K], 4), ones(A_OUT)]).reshape(1, -1)
        gain_bc = jnp.concatenate([
            jnp.tile(gq[QK_B_Q], B_Q_HEADS), jnp.tile(gq[QK_B_K], B_KV_HEADS), ones(B_KV_HEADS * HEAD_DIM),
            jnp.tile(gq[QK_M_Q], M_HEADS), jnp.tile(gq[QK_C_Q], 2 * C_HEADS),
            jnp.tile(gq[QK_C_K], 2 * C_HEADS), ones(C_OUT)]).reshape(1, -1)
        gain_z = ones(O_GATE - O_ZA).reshape(1, -1)
        gain_m = jnp.concatenate([jnp.tile(gq[QK_M_K], M_HEADS), ones(M_OUT)]).reshape(1, -1)

        hs = rmsnorm_bf16(xf, norm_g[l], 256, dils[1:])
        h = hs[0]
        h_perm = [h] + [hp.reshape(t, d) for hp in hs[1:]]

        def a_cols(g):
            return [O_AQ // pc + g, O_AK // pc + g, O_AV // pc + g]

        def a_attn(g):
            win, dil = A_PATTERNS[g]
            return dict(rows_per_seq=s_len // dil, qcol=0, kcol=A_OUT, vcol=2 * A_OUT,
                        n_q=A_HEADS_PER_GROUP, n_kv=A_HEADS_PER_GROUP, slopes=slopes_a,
                        max_dist=win // dil, dist_scale=dil, tq=1024, out_dtype=F32, unit=1, write_lse=True)

        a_mode = "NNNNNNNNPPPP"
        n8 = "NNNNNNNN"
        sz = proj(h, w_in, l, [(O_ZB // pc, O_ZC // pc, O_ZA // pc), (O_ZB // pc + 1, O_ZC // pc + 1, O_ZM // pc)],
                  gain_z, ("SSSSSSSS",) * 3, 1024, F32, "proj_z")
        ap0 = proj(h_perm[0], w_in, l, [(c,) for c in a_cols(0)], gain_a, (a_mode,), 1024, BF16, "proj_a0")
        ap1, o0, l0 = proj_fused(h_perm[1], w_in, l, a_cols(1), gain_a, a_mode, 1024, BF16,
                                 banded_extra(ap0, ap0, **a_attn(0)), "proj_a1_attn_a0")
        ap2, o1, l1 = proj_fused(h_perm[2], w_in, l, a_cols(2), gain_a, a_mode, 1024, BF16,
                                 banded_extra(ap1, ap1, **a_attn(1)), "proj_a2_attn_a1")
        bqa, o2, l2 = proj_fused(h, w_in, l, [O_BQ // pc, O_BQ // pc + 1], gain_bc[:, :g_bm], n8, 1024, BF16,
                                 banded_extra(ap2, ap2, **a_attn(2)), "proj_bq_attn_a2")
        bm = proj(h, w_in, l, [(O_BK // pc,), (O_MQ // pc,)], gain_bc[:, g_bm:g_cq], ("NNPPNNNN",),
                  1024, BF16, "proj_bm")
        mn = rmsnorm_bf16(memf, mem_norm_g[l], 256)[0]
        mkv = proj(mn, w_mem_kv, l, [(0,), (1,)], gain_m, ("NNNNPPPP",), bsz * N_MEM, BF16, "proj_mem")
        cq, u_b = proj_fused(
            h, w_in, l, [O_CQ // pc, O_CQ // pc + 1], gain_bc[:, g_cq:g_ck], n8, 1024, BF16,
            banded_extra(bqa, bm, rows_per_seq=s_len, qcol=0, kcol=0, vcol=B_KV_HEADS * HEAD_DIM,
                         n_q=B_Q_HEADS, n_kv=B_KV_HEADS, slopes=slopes_b, max_dist=B_WINDOW - 1, dist_scale=1,
                         tq=1024, out_dtype=BF16, batch=1, sinks=sinks[l], z=sz, zcol=z_zb), "proj_cq_attn_b")
        ck, u_m = proj_fused(
            h, w_in, l, [O_CK // pc, O_CK // pc + 1], gain_bc[:, g_ck:g_cv], n8, 1024, BF16,
            mem_extra(bm, mkv, sz, seq=s_len, qcol=2 * B_KV_HEADS * HEAD_DIM, zcol=z_zm, tq=1024),
            "proj_ck_attn_m")
        cv = proj(h, w_in, l, [(O_CV // pc,), (O_CV // pc + 1,)], gain_bc[:, g_cv:], ("PPPPPPPP",),
                  1024, BF16, "proj_cv")
        u_a = combine_a([o0, o1, o2], [l0, l1, l2], sz, z_za, 1024, dils[1:])

        lam_init = 0.8 - 0.6 * math.exp(-0.3 * l)
        u_c = diff_attention(cq, ck, cv, slopes_c, lam[l], subln_g[l], sz, seq=s_len,
                             zcol=z_zc, tq=512, tk=256, lam_init=lam_init)

        merged = gated_merge(h, [u_a, u_b, u_c, u_m], w_in, b_gate3, w_branch, l, 1024, 256)
        xf = out_proj(xf, merged, w_out, l, 1024, 1024)

    return xf.reshape(bsz, s_len, d)
```

```python
import functools
import math

import jax
import jax.numpy as jnp
from jax import lax
from jax.experimental import pallas as pl
from jax.experimental.pallas import tpu as pltpu

D_MODEL = 2048
HEAD_DIM = 128
BLOCK = 128
EPS = 1e-6
A_PATTERNS = ((128, 1), (512, 4), (2048, 16))
A_GROUPS = 3
A_HEADS_PER_GROUP = 4
A_HEADS = A_GROUPS * A_HEADS_PER_GROUP
A_OUT = A_HEADS_PER_GROUP * HEAD_DIM
B_Q_HEADS = 8
B_KV_HEADS = 2
B_WINDOW = 128
B_OUT = B_Q_HEADS * HEAD_DIM
C_HEADS = 4
C_OUT = C_HEADS * 2 * HEAD_DIM
N_MEM = 256
M_HEADS = 4
M_OUT = M_HEADS * HEAD_DIM
N_BRANCH = 4
QK_A_Q, QK_A_K, QK_B_Q, QK_B_K, QK_C_Q, QK_C_K, QK_M_Q, QK_M_K = range(8)

_IN_SIZES = (A_HEADS * HEAD_DIM, A_HEADS * HEAD_DIM, A_HEADS * HEAD_DIM,
             B_Q_HEADS * HEAD_DIM, B_KV_HEADS * HEAD_DIM, B_KV_HEADS * HEAD_DIM,
             C_HEADS * 2 * HEAD_DIM, C_HEADS * 2 * HEAD_DIM, C_OUT,
             M_HEADS * HEAD_DIM, A_OUT, B_OUT, C_OUT, M_OUT, N_BRANCH * D_MODEL)
_OFF = [0]
for _s in _IN_SIZES:
    _OFF.append(_OFF[-1] + _s)
(O_AQ, O_AK, O_AV, O_BQ, O_BK, O_BV, O_CQ, O_CK, O_CV, O_MQ,
 O_ZA, O_ZB, O_ZC, O_ZM, O_GATE, O_END) = _OFF

NEG = -1e30
SCALE = HEAD_DIM ** -0.5
VMEM_LIMIT = 58 * 1024 * 1024
PROJ_PIECE = 512
DIFF_UNROLL = (8, 6, 4, 2, 1)
LOG2E = math.log2(math.e)
BF16 = jnp.bfloat16
F32 = jnp.float32
CONTRACT_LAST = (((1,), (1,)), ((), ()))
CONTRACT_FIRST = (((0,), (0,)), ((), ()))


def _params(n_axes):
    return pltpu.CompilerParams(dimension_semantics=("arbitrary",) * n_axes,
                                vmem_limit_bytes=VMEM_LIMIT)


def _sigmoid(x):
    return 0.5 * jnp.tanh(0.5 * x) + 0.5


def _alibi_slopes(n):
    return [2.0 ** (-8.0 * i / n) for i in range(1, n + 1)]


def _rmsnorm_kernel(x_ref, g_ref, o_ref, *perm_refs, dils, tm):
    x = x_ref[...]
    ms = jnp.mean(x * x, axis=-1, keepdims=True)
    h = (x * lax.rsqrt(ms + EPS) * g_ref[...]).astype(BF16)
    o_ref[...] = h
    row = lax.broadcasted_iota(jnp.int32, (tm, tm), 0)
    col = lax.broadcasted_iota(jnp.int32, (tm, tm), 1)
    for d, p_ref in zip(dils, perm_refs):
        n = tm // d
        src = (row % n) * d + row // n
        perm = jnp.where(col == src, 1.0, 0.0).astype(BF16)
        hp = jnp.dot(perm, h, preferred_element_type=F32).astype(BF16)
        for p in range(d):
            p_ref[p] = hp[p * n:(p + 1) * n, :]


def rmsnorm_bf16(x, g, tm, dils=()):
    t, d = x.shape
    out_shape = [jax.ShapeDtypeStruct((t, d), BF16)]
    out_specs = [pl.BlockSpec((tm, d), lambda i: (i, 0))]
    for dil in dils:
        out_shape.append(jax.ShapeDtypeStruct((dil, t // dil, d), BF16))
        out_specs.append(pl.BlockSpec((dil, tm // dil, d), lambda i: (0, i, 0)))
    outs = pl.pallas_call(
        functools.partial(_rmsnorm_kernel, dils=tuple(dils), tm=tm),
        out_shape=out_shape,
        grid=(t // tm,),
        in_specs=[pl.BlockSpec((tm, d), lambda i: (i, 0)),
                  pl.BlockSpec((1, d), lambda i: (0, 0))],
        out_specs=out_specs,
        compiler_params=_params(1),
        name="rmsnorm",
    )(x, g.reshape(1, d))
    return outs


def _proj_kernel(*refs, tile_modes, n_piece):
    h_ref, w_refs = refs[0], refs[1:1 + n_piece]
    gain_ref, o_ref, w_sc = refs[1 + n_piece:]
    _proj_body(h_ref, w_refs, gain_ref, o_ref, w_sc, tile_modes)


def _proj_body(h_ref, w_refs, gain_ref, o_ref, w_sc, tile_modes):
    n_piece = len(w_refs)
    j = pl.program_id(0)

    @pl.when(pl.program_id(1) == 0)
    def _():
        for p in range(n_piece):
            w_sc[:, p * PROJ_PIECE:(p + 1) * PROJ_PIECE] = w_refs[p][...].astype(BF16)

    acc = jnp.dot(h_ref[...], w_sc[...], preferred_element_type=F32)
    for c in range(len(tile_modes[0])):
        sl = slice(c * HEAD_DIM, (c + 1) * HEAD_DIM)
        modes = [m[c] for m in tile_modes]
        y = acc[:, sl]
        if 'S' in modes:
            assert set(modes) == {'S'}
            y = y * _sigmoid(y)
        elif 'N' in modes:
            ms = jnp.mean(y * y, axis=-1, keepdims=True)
            normed = y * lax.rsqrt(ms + EPS) * gain_ref[:, sl]
            if set(modes) == {'N'}:
                y = normed
            else:
                is_norm = functools.reduce(jnp.logical_or, [j == jj for jj, m in enumerate(modes) if m == 'N'])
                y = jnp.where(is_norm, normed, y)
        o_ref[:, sl] = y.astype(o_ref.dtype)


def _select(j, values):
    out = values[-1]
    for jj in range(len(values) - 2, -1, -1):
        out = jnp.where(j == jj, values[jj], out)
    return out


def proj(h, w3, layer, piece_cols, gain, tile_modes, tm, out_dtype, name):
    t, k = h.shape
    n_piece = len(piece_cols)
    tn = n_piece * PROJ_PIECE
    n_tiles = len(tile_modes)
    assert t % tm == 0 and all(len(m) == tn // HEAD_DIM for m in tile_modes)
    assert all(len(cols) == n_tiles for cols in piece_cols) and gain.shape == (1, tn * n_tiles)
    w_specs = [pl.BlockSpec((None, k, PROJ_PIECE),
                            functools.partial(lambda j, i, cols: (layer, 0, _select(j, cols)), cols=tuple(cols)))
               for cols in piece_cols]
    return pl.pallas_call(
        functools.partial(_proj_kernel, tile_modes=tuple(tile_modes), n_piece=n_piece),
        out_shape=jax.ShapeDtypeStruct((t, tn * n_tiles), out_dtype),
        grid=(n_tiles, t // tm),
        in_specs=[pl.BlockSpec((tm, k), lambda j, i: (i, 0))] + w_specs
                 + [pl.BlockSpec((1, tn), lambda j, i: (0, j))],
        out_specs=pl.BlockSpec((tm, tn), lambda j, i: (i, j)),
        scratch_shapes=[pltpu.VMEM((k, tn), BF16)],
        compiler_params=_params(2),
        name=name,
    )(h, *([w3] * n_piece), gain)


def _banded_body(q_ref, kc_ref, kp_ref, vc_ref, vp_ref, sink_ref, z_ref, o_ref, lse_ref, *, tile,
                 n_q, n_kv, slopes, max_dist, dist_scale, tq, rows_per_seq, unit, batch, between=None):
    has_sink, mul_z, write_lse = sink_ref is not None, z_ref is not None, lse_ref is not None
    shared_kv = n_kv < n_q and unit > 1
    group = n_q // n_kv
    assert n_q % unit == 0 and (not shared_kv or group == unit)
    kk = lax.broadcasted_iota(jnp.int32, (2 * BLOCK, BLOCK), 0)
    r = lax.broadcasted_iota(jnp.int32, (2 * BLOCK, BLOCK), 1)
    dist = r + BLOCK - kk
    distf = (dist * dist_scale).astype(F32)
    valid = (dist >= 0) & (dist <= max_dist)
    valid_start = valid & (kk >= BLOCK)
    if rows_per_seq > tq:
        has_prev = jnp.minimum(tile % (rows_per_seq // tq), 1) * BLOCK
        valid0 = valid & (kk + has_prev >= BLOCK)
    else:
        valid0 = valid_start

    def window(cur_ref, prev_ref, kv, j):
        ks = slice(kv * HEAD_DIM, (kv + 1) * HEAD_DIM)
        prev = prev_ref[:, ks] if j == 0 else cur_ref[(j - 1) * BLOCK:j * BLOCK, ks]
        return jnp.concatenate([prev, cur_ref[j * BLOCK:(j + 1) * BLOCK, ks]], axis=0)

    units = [(list(range(u * unit, (u + 1) * unit)), j)
             for u in range(n_q // unit) for j in range(tq // BLOCK)]

    for b0 in range(0, len(units), batch or len(units)):
        batch_units = units[b0:b0 + (batch or len(units))]
        scores = []
        for heads, j in batch_units:
            rows = slice(j * BLOCK, (j + 1) * BLOCK)
            qs = [q_ref[rows, h * HEAD_DIM:(h + 1) * HEAD_DIM] for h in heads]
            if shared_kv:
                s_t = lax.dot_general(window(kc_ref, kp_ref, heads[0] // group, j), jnp.concatenate(qs, axis=0),
                                      CONTRACT_LAST, preferred_element_type=F32)
            else:
                s_t = jnp.concatenate(
                    [lax.dot_general(window(kc_ref, kp_ref, h // group, j), q, CONTRACT_LAST,
                                     preferred_element_type=F32) for h, q in zip(heads, qs)], axis=1)
            scores.append(s_t)
        if between is not None and b0 == 0:
            between()

        probs, lses = [], []
        for (heads, j), s_t in zip(batch_units, scores):
            v_mask = valid0 if j == 0 else (valid_start if (j * BLOCK) % rows_per_seq == 0 else valid)
            bias = jnp.concatenate([jnp.where(v_mask, (-slopes[h] * LOG2E) * distf, NEG) for h in heads], axis=1)
            s_t = s_t * (SCALE * LOG2E) + bias
            m = jnp.max(s_t, axis=0, keepdims=True)
            if has_sink:
                sink = jnp.concatenate([jnp.full((1, BLOCK), sink_ref[h] * LOG2E, F32) for h in heads], axis=1)
                m = jnp.maximum(m, sink)
            e = jnp.exp2(s_t - m)
            denom = jnp.sum(e, axis=0, keepdims=True)
            if has_sink:
                denom = denom + jnp.exp2(sink - m)
            probs.append((e * (1.0 / denom)).astype(BF16))
            lses.append(m + jnp.log2(denom) if write_lse else None)

        for (heads, j), p_t, lse in zip(batch_units, probs, lses):
            rows = slice(j * BLOCK, (j + 1) * BLOCK)
            if shared_kv:
                o_all = lax.dot_general(p_t, window(vc_ref, vp_ref, heads[0] // group, j), CONTRACT_FIRST,
                                        preferred_element_type=F32)
            for g, h in enumerate(heads):
                hs = slice(h * HEAD_DIM, (h + 1) * HEAD_DIM)
                gs = slice(g * BLOCK, (g + 1) * BLOCK)
                if shared_kv:
                    o = o_all[gs, :]
                else:
                    o = lax.dot_general(p_t[:, gs], window(vc_ref, vp_ref, h // group, j), CONTRACT_FIRST,
                                        preferred_element_type=F32)
                if mul_z:
                    o = o * z_ref[rows, hs]
                o_ref[rows, hs] = o.astype(o_ref.dtype)
                if write_lse:
                    lse_ref[rows, hs] = jnp.broadcast_to(lse[:, gs], (BLOCK, BLOCK)).T


def _banded_specs(q_arr, kv_arr, *, qcol, kcol, vcol, n_q, n_kv, tq, out_dtype, sinks, z, zcol, write_lse,
                  tile_of):
    rows_total = q_arr.shape[0]
    qw, kw = n_q * HEAD_DIM, n_kv * HEAD_DIM
    assert qcol % qw == 0 and kcol % kw == 0 and vcol % kw == 0
    sub = tq // BLOCK

    def cur(col, w):
        return lambda *g: (tile_of(*g), col // w)

    def prev(col, w):
        return lambda *g: (jnp.maximum(tile_of(*g) * sub - 1, 0), col // w)

    in_specs = [
        pl.BlockSpec((tq, qw), cur(qcol, qw)),
        pl.BlockSpec((tq, kw), cur(kcol, kw)),
        pl.BlockSpec((BLOCK, kw), prev(kcol, kw)),
        pl.BlockSpec((tq, kw), cur(vcol, kw)),
        pl.BlockSpec((BLOCK, kw), prev(vcol, kw)),
    ]
    args = [q_arr, kv_arr, kv_arr, kv_arr, kv_arr]
    if sinks is not None:
        in_specs.append(pl.BlockSpec(memory_space=pltpu.SMEM))
        args.append(sinks)
    if z is not None:
        assert zcol % qw == 0
        in_specs.append(pl.BlockSpec((tq, qw), cur(zcol, qw)))
        args.append(z)
    out_block = pl.BlockSpec((tq, qw), cur(0, qw))
    out_shape = [jax.ShapeDtypeStruct((rows_total, qw), out_dtype)]
    out_specs = [out_block]
    if write_lse:
        out_shape.append(jax.ShapeDtypeStruct((rows_total, qw), F32))
        out_specs.append(out_block)
    return in_specs, args, out_shape, out_specs


def _banded_cfg(*, rows_per_seq, n_q, n_kv, slopes, max_dist, dist_scale, tq, unit, batch):
    assert tq % rows_per_seq == 0 or rows_per_seq % tq == 0
    return dict(n_q=n_q, n_kv=n_kv, slopes=tuple(slopes), max_dist=max_dist, dist_scale=dist_scale,
                tq=tq, rows_per_seq=rows_per_seq, unit=unit, batch=batch)


def _proj_fused_kernel(*refs, tile_mode, n_piece, n_in, n_out, body):
    h_ref, w_refs, gain_ref = refs[0], refs[1:1 + n_piece], refs[1 + n_piece]
    extra_in = refs[2 + n_piece:2 + n_piece + n_in]
    p_out = refs[2 + n_piece + n_in]
    extra_out = refs[3 + n_piece + n_in:3 + n_piece + n_in + n_out]
    w_sc = refs[3 + n_piece + n_in + n_out]
    extra_scratch = refs[4 + n_piece + n_in + n_out:]
    body(extra_in, extra_out, extra_scratch, pl.program_id(1),
         lambda: _proj_body(h_ref, w_refs, gain_ref, p_out, w_sc, (tile_mode,)))


def proj_fused(h, w3, layer, piece_cols, gain, tile_mode, tm, out_dtype, extra, name):
    t, k = h.shape
    n_piece = len(piece_cols)
    tn = n_piece * PROJ_PIECE
    assert len(tile_mode) == tn // HEAD_DIM and gain.shape == (1, tn)
    e_specs, e_args, e_shape, e_out_specs, e_scratch, body = extra
    w_specs = [pl.BlockSpec((None, k, PROJ_PIECE), functools.partial(lambda j, i, c: (layer, 0, c), c=c))
               for c in piece_cols]
    return pl.pallas_call(
        functools.partial(_proj_fused_kernel, tile_mode=tile_mode, n_piece=n_piece,
                          n_in=len(e_specs), n_out=len(e_shape), body=body),
        out_shape=[jax.ShapeDtypeStruct((t, tn), out_dtype)] + list(e_shape),
        grid=(1, t // tm),
        in_specs=[pl.BlockSpec((tm, k), lambda j, i: (i, 0))] + w_specs
                 + [pl.BlockSpec((1, tn), lambda j, i: (0, 0))] + list(e_specs),
        out_specs=[pl.BlockSpec((tm, tn), lambda j, i: (i, 0))] + list(e_out_specs),
        scratch_shapes=[pltpu.VMEM((k, tn), BF16)] + list(e_scratch),
        compiler_params=_params(2),
        name=name,
    )(h, *([w3] * n_piece), gain, *e_args)


def banded_extra(q_arr, kv_arr, *, rows_per_seq, qcol, kcol, vcol, n_q, n_kv, slopes, max_dist, dist_scale,
                 tq, out_dtype, unit=4, batch=None, sinks=None, z=None, zcol=0, write_lse=False):
    in_specs, args, out_shape, out_specs = _banded_specs(
        q_arr, kv_arr, qcol=qcol, kcol=kcol, vcol=vcol, n_q=n_q, n_kv=n_kv, tq=tq, out_dtype=out_dtype,
        sinks=sinks, z=z, zcol=zcol, write_lse=write_lse, tile_of=lambda j, i: i)
    cfg = _banded_cfg(rows_per_seq=rows_per_seq, n_q=n_q, n_kv=n_kv, slopes=slopes, max_dist=max_dist,
                      dist_scale=dist_scale, tq=tq, unit=unit, batch=batch)
    has_sink, mul_z = sinks is not None, z is not None

    def body(in_refs, out_refs, scratch_refs, tile, proj_fn):
        in_refs = list(in_refs)
        sink_ref = in_refs[5] if has_sink else None
        z_ref = in_refs[5 + has_sink] if mul_z else None
        lse_ref = out_refs[1] if write_lse else None
        _banded_body(*in_refs[:5], sink_ref, z_ref, out_refs[0], lse_ref, tile=tile, between=proj_fn, **cfg)

    return in_specs, args, out_shape, out_specs, [], body


def _combine_kernel(o0, o1, o2, l0, l1, l2, z_ref, u_ref, o1_sc, o2_sc, l1_sc, l2_sc, *, tm, dils):
    n_chunk = o0.shape[1] // HEAD_DIM
    for src, dst, d in ((o1, o1_sc, dils[0]), (l1, l1_sc, dils[0]),
                        (o2, o2_sc, dils[1]), (l2, l2_sc, dils[1])):
        n = tm // d
        for p in range(d):
            for c in range(n_chunk):
                dst[c, pl.ds(p, n, stride=d), :] = src[p, :, c * HEAD_DIM:(c + 1) * HEAD_DIM]
    for c in range(n_chunk):
        cs = slice(c * HEAD_DIM, (c + 1) * HEAD_DIM)
        a0, a1, a2 = l0[:, cs], l1_sc[c], l2_sc[c]
        m = jnp.maximum(jnp.maximum(a0, a1), a2)
        e0, e1, e2 = jnp.exp2(a0 - m), jnp.exp2(a1 - m), jnp.exp2(a2 - m)
        inv = 1.0 / (e0 + e1 + e2)
        ya = (e0 * inv) * o0[:, cs] + (e1 * inv) * o1_sc[c] + (e2 * inv) * o2_sc[c]
        u_ref[:, cs] = (ya * z_ref[:, cs]).astype(u_ref.dtype)


def combine_extra(outs, lses, z, zcol, tm, dils):
    t, w = outs[0].shape
    blk = pl.BlockSpec((tm, w), lambda j, i: (i, 0))

    def pblk(d):
        return pl.BlockSpec((d, tm // d, w), lambda j, i: (0, i, 0))

    def p3(a, d):
        return a.reshape(d, t // d, w)

    d1, d2 = dils
    in_specs = [blk, pblk(d1), pblk(d2), blk, pblk(d1), pblk(d2),
                pl.BlockSpec((tm, w), lambda j, i: (i, zcol // w))]
    args = [outs[0], p3(outs[1], d1), p3(outs[2], d2), lses[0], p3(lses[1], d1), p3(lses[2], d2), z]

    def body(in_refs, out_refs, scratch_refs, tile, proj_fn):
        _combine_kernel(*in_refs, out_refs[0], *scratch_refs, tm=tm, dils=dils)
        proj_fn()

    return (in_specs, args, [jax.ShapeDtypeStruct((t, w), BF16)], [blk],
            [pltpu.VMEM((w // HEAD_DIM, tm, HEAD_DIM), F32)] * 4, body)


def _diff_kernel(slope_ref, q_ref, k_ref, v_ref, lam_ref, g_ref, z_ref, o_ref,
                 vt_sc, acc_sc, bias_sc, s_sc, *, tq, tk, seq, lam_init):
    h = pl.program_id(1)
    qi = pl.program_id(2)
    slope = slope_ref[h]
    per_q = tq // tk
    base = qi * per_q

    @pl.when(qi == 0)
    def _():
        for cidx in range(seq // tk):
            rows = slice(cidx * tk, (cidx + 1) * tk)
            vt_sc[:, rows] = v_ref[rows, :].astype(F32).T.astype(BF16)
        kk = lax.broadcasted_iota(jnp.int32, (tk, tq), 0)
        r = lax.broadcasted_iota(jnp.int32, (tk, tq), 1)
        bias = (-slope / SCALE) * (r - kk).astype(F32)
        bias_sc[0] = bias
        for dblk in range(per_q):
            bias_sc[1 + dblk] = jnp.where(r >= kk + dblk * tk, bias, NEG)

    qs = [q_ref[:, c * HEAD_DIM:(c + 1) * HEAD_DIM] for c in range(2)]
    to_log2 = SCALE * LOG2E

    def shift_of(kj):
        return (slope * LOG2E) * ((base - kj) * tk).astype(F32)

    def max_chunk(kj0, n, ms):
        ms = list(ms)
        for u in range(n):
            kj = kj0 + u
            start = pl.multiple_of(kj * tk, tk)
            b = bias_sc[jnp.maximum(kj - base + 1, 0)]
            for comp in range(2):
                k = k_ref[pl.ds(start, tk), comp * HEAD_DIM:(comp + 1) * HEAD_DIM]
                a = (lax.dot_general(k, qs[comp], CONTRACT_LAST, preferred_element_type=F32) + b) * to_log2
                s_sc[comp, pl.ds(start, tk), :] = a
                ms[comp] = jnp.maximum(ms[comp], jnp.max(a, axis=0, keepdims=True) - shift_of(kj))
        return tuple(ms)

    def exp_chunk(kj0, n, ls):
        ls = list(ls)
        start0 = pl.multiple_of(kj0 * tk, tk)
        vt = vt_sc[:, pl.ds(start0, n * tk)]
        for comp in range(2):
            es = []
            for u in range(n):
                start = pl.multiple_of((kj0 + u) * tk, tk)
                e = jnp.exp2(s_sc[comp, pl.ds(start, tk), :] - (ms[comp] + shift_of(kj0 + u)))
                ls[comp] = ls[comp] + jnp.sum(e, axis=0, keepdims=True)
                es.append(e.astype(BF16))
            e_cat = es[0] if n == 1 else jnp.concatenate(es, axis=0)
            acc_sc[comp] += jnp.dot(vt, e_cat, preferred_element_type=F32)
        return tuple(ls)

    def over_blocks(chunk_fn, carry):
        done = 0
        for n in DIFF_UNROLL:
            if n % per_q:
                continue
            n_it = (base + per_q - done) // n
            carry = lax.fori_loop(
                0, n_it, lambda t, c, n=n, done=done: chunk_fn(done + t * n, n, c), carry)
            done = done + n_it * n
        return carry

    neg = jnp.full((1, tq), NEG, F32)
    ms = over_blocks(max_chunk, (neg, neg))
    acc_sc[...] = jnp.zeros(acc_sc.shape, F32)
    zero = jnp.zeros((1, tq), F32)
    l1, l2 = over_blocks(exp_chunk, (zero, zero))

    lp = lam_ref[...]
    lam_full = (jnp.exp(jnp.sum(lp[0:1] * lp[1:2], axis=-1, keepdims=True))
                - jnp.exp(jnp.sum(lp[2:3] * lp[3:4], axis=-1, keepdims=True)) + lam_init)
    y_t = acc_sc[0] * (1.0 / l1) - lam_full * (acc_sc[1] * (1.0 / l2))
    ms2 = jnp.mean(y_t * y_t, axis=0, keepdims=True)
    y = (y_t * lax.rsqrt(ms2 + EPS)).T
    y = y * g_ref[...] * (1.0 - lam_init)
    o_ref[...] = (y * z_ref[...]).astype(o_ref.dtype)


def diff_attention(cq, ck, cv, slopes, lam_l, subln_g, z, *, seq, zcol, tq, tk, lam_init):
    t = cq.shape[0]
    n_batch = t // seq
    nq = seq // tq
    hw = 2 * HEAD_DIM
    kern = functools.partial(_diff_kernel, tq=tq, tk=tk, seq=seq, lam_init=lam_init)
    return pl.pallas_call(
        kern,
        out_shape=jax.ShapeDtypeStruct((t, C_OUT), BF16),
        grid=(n_batch, C_HEADS, nq),
        in_specs=[
            pl.BlockSpec(memory_space=pltpu.SMEM),
            pl.BlockSpec((tq, hw), lambda b, h, i: (b * nq + i, h)),
            pl.BlockSpec((seq, hw), lambda b, h, i: (b, h)),
            pl.BlockSpec((seq, hw), lambda b, h, i: (b, h)),
            pl.BlockSpec((4, HEAD_DIM), lambda b, h, i: (0, 0)),
            pl.BlockSpec((1, hw), lambda b, h, i: (0, 0)),
            pl.BlockSpec((tq, hw), lambda b, h, i: (b * nq + i, zcol // hw + h)),
        ],
        out_specs=pl.BlockSpec((tq, hw), lambda b, h, i: (b * nq + i, h)),
        scratch_shapes=[pltpu.VMEM((hw, seq), BF16), pltpu.VMEM((2, hw, tq), F32),
                        pltpu.VMEM((1 + tq // tk, tk, tq), F32), pltpu.VMEM((2, seq, tq), F32)],
        compiler_params=_params(3),
        name="diff_attention",
    )(slopes, cq, ck, cv, lam_l, subln_g.reshape(1, hw), z)


def _mem_kernel(q_ref, k_ref, v_ref, z_ref, o_ref):
    for h in range(M_HEADS):
        hs = slice(h * HEAD_DIM, (h + 1) * HEAD_DIM)
        s_t = lax.dot_general(k_ref[:, hs], q_ref[:, hs], CONTRACT_LAST,
                              preferred_element_type=F32) * (SCALE * LOG2E)
        m = jnp.max(s_t, axis=0, keepdims=True)
        e = jnp.exp2(s_t - m)
        p_t = (e * (1.0 / jnp.sum(e, axis=0, keepdims=True))).astype(BF16)
        o = lax.dot_general(p_t, v_ref[:, hs], CONTRACT_FIRST, preferred_element_type=F32)
        o_ref[:, hs] = (o * z_ref[:, hs]).astype(o_ref.dtype)


def mem_extra(cp, mkv, z, *, seq, qcol, zcol, tq):
    t = cp.shape[0]
    nq = seq // tq
    in_specs = [
        pl.BlockSpec((tq, M_OUT), lambda j, i: (i, qcol // M_OUT)),
        pl.BlockSpec((N_MEM, M_OUT), lambda j, i: (i // nq, 0)),
        pl.BlockSpec((N_MEM, M_OUT), lambda j, i: (i // nq, 1)),
        pl.BlockSpec((tq, M_OUT), lambda j, i: (i, zcol // M_OUT)),
    ]
    out_shape = [jax.ShapeDtypeStruct((t, M_OUT), BF16)]
    out_specs = [pl.BlockSpec((tq, M_OUT), lambda j, i: (i, 0))]

    def body(in_refs, out_refs, scratch_refs, tile, proj_fn):
        proj_fn()
        _mem_kernel(*in_refs, out_refs[0])

    return in_specs, [cp, mkv, mkv, z], out_shape, out_specs, [], body


_BRANCH_ROWS = ((0, A_OUT), (A_OUT, A_OUT + B_OUT), (A_OUT + B_OUT, A_OUT + B_OUT + C_OUT),
                (A_OUT + B_OUT + C_OUT, A_OUT + B_OUT + C_OUT + M_OUT))
_WB_CHUNK = 512


def _merge_kernel(*refs):
    h_ref, us = refs[0], refs[1:5]
    wgs, bgs = refs[5:9], refs[9:13]
    n_chunk = (A_OUT + B_OUT + C_OUT + M_OUT) // _WB_CHUNK
    wbs = refs[13:13 + n_chunk]
    o_ref, wg_sc, wb_sc = refs[13 + n_chunk:]

    @pl.when(pl.program_id(1) == 0)
    def _():
        for b in range(N_BRANCH):
            wg_sc[b] = wgs[b][...].astype(BF16)
        for c in range(n_chunk):
            wb_sc[c * _WB_CHUNK:(c + 1) * _WB_CHUNK, :] = wbs[c][...].astype(BF16)

    h = h_ref[...]
    acc = None
    for b in range(N_BRANCH):
        lo, hi = _BRANCH_ROWS[b]
        gate = _sigmoid(jnp.dot(h, wg_sc[b], preferred_element_type=F32) + bgs[b][...])
        term = gate * jnp.dot(us[b][...], wb_sc[lo:hi, :], preferred_element_type=F32)
        acc = term if acc is None else acc + term
    o_ref[...] = acc.astype(o_ref.dtype)


def gated_merge(h, us, w_in, b_gate3, w_branch, layer, tm, tn):
    t, d = h.shape
    nj = d // tn
    gate0 = O_GATE // tn
    n_chunk = w_branch.shape[1] // _WB_CHUNK
    in_specs = [pl.BlockSpec((tm, d), lambda j, i: (i, 0))]
    in_specs += [pl.BlockSpec((tm, u.shape[1]), lambda j, i: (i, 0)) for u in us]
    in_specs += [pl.BlockSpec((None, d, tn), functools.partial(
        lambda j, i, b: (layer, 0, gate0 + b * nj + j), b=b)) for b in range(N_BRANCH)]
    in_specs += [pl.BlockSpec((None, 1, tn), functools.partial(
        lambda j, i, b: (layer, 0, b * nj + j), b=b)) for b in range(N_BRANCH)]
    in_specs += [pl.BlockSpec((None, _WB_CHUNK, tn), functools.partial(
        lambda j, i, c: (layer, c, j), c=c)) for c in range(n_chunk)]
    return pl.pallas_call(
        _merge_kernel,
        out_shape=jax.ShapeDtypeStruct((t, d), BF16),
        grid=(nj, t // tm),
        in_specs=in_specs,
        out_specs=pl.BlockSpec((tm, tn), lambda j, i: (i, j)),
        scratch_shapes=[pltpu.VMEM((N_BRANCH, d, tn), BF16),
                        pltpu.VMEM((w_branch.shape[1], tn), BF16)],
        compiler_params=_params(2),
        name="gated_merge",
    )(h, *us, *([w_in] * N_BRANCH), *([b_gate3] * N_BRANCH), *([w_branch] * n_chunk))


def _out_kernel(x_ref, m_ref, w_ref, o_ref, w_sc):
    @pl.when(pl.program_id(1) == 0)
    def _():
        w_sc[...] = w_ref[...].astype(BF16)

    o_ref[...] = x_ref[...] + jnp.dot(m_ref[...], w_sc[...], preferred_element_type=F32)


def out_proj(x, merged, w_out, layer, tm, tn):
    t, d = x.shape
    return pl.pallas_call(
        _out_kernel,
        out_shape=jax.ShapeDtypeStruct((t, d), F32),
        grid=(d // tn, t // tm),
        in_specs=[pl.BlockSpec((tm, tn), lambda j, i: (i, j)),
                  pl.BlockSpec((tm, d), lambda j, i: (i, 0)),
                  pl.BlockSpec((None, d, tn), lambda j, i: (layer, 0, j))],
        out_specs=pl.BlockSpec((tm, tn), lambda j, i: (i, j)),
        scratch_shapes=[pltpu.VMEM((d, tn), BF16)],
        compiler_params=_params(2),
        name="out_proj",
    )(x, merged, w_out)


def kernel(x, mem, norm_g, w_in, b_gate, qk_gain, sinks, lam, subln_g, mem_norm_g,
           w_mem_kv, w_branch, w_out):
    bsz, s_len, d = x.shape
    depth = w_in.shape[0]
    t = bsz * s_len
    xf = x.reshape(t, d)
    memf = mem.reshape(bsz * N_MEM, d)
    b_gate3 = b_gate.reshape(depth, 1, N_BRANCH * d)

    slopes_a = _alibi_slopes(A_HEADS_PER_GROUP)
    slopes_b = _alibi_slopes(B_Q_HEADS)
    slopes_c = jnp.asarray(_alibi_slopes(C_HEADS), F32)
    ones = lambda n: jnp.ones((n,), F32)
    dils = tuple(dil for _, dil in A_PATTERNS)

    pc = PROJ_PIECE
    z_zb, z_zc, z_za, z_zm = 0, B_OUT, B_OUT + C_OUT, B_OUT + C_OUT + A_OUT
    g_bm = B_OUT
    g_cq = g_bm + 2 * B_KV_HEADS * HEAD_DIM + M_OUT
    g_ck, g_cv = g_cq + C_OUT, g_cq + 2 * C_OUT

    for l in range(depth):
        gq = qk_gain[l]
        gain_a = jnp.concatenate([jnp.tile(gq[QK_A_Q], 4), jnp.tile(gq[QK_A_K], 4), ones(A_OUT)]).reshape(1, -1)
        gain_bc = jnp.concatenate([
            jnp.tile(gq[QK_B_Q], B_Q_HEADS), jnp.tile(gq[QK_B_K], B_KV_HEADS), ones(B_KV_HEADS * HEAD_DIM),
            jnp.tile(gq[QK_M_Q], M_HEADS), jnp.tile(gq[QK_C_Q], 2 * C_HEADS),
            jnp.tile(gq[QK_C_K], 2 * C_HEADS), ones(C_OUT)]).reshape(1, -1)
        gain_z = ones(O_GATE - O_ZA).reshape(1, -1)
        gain_m = jnp.concatenate([jnp.tile(gq[QK_M_K], M_HEADS), ones(M_OUT)]).reshape(1, -1)

        hs = rmsnorm_bf16(xf, norm_g[l], 256, dils[1:])
        h = hs[0]
        h_perm = [h] + [hp.reshape(t, d) for hp in hs[1:]]

        def a_cols(g):
            return [O_AQ // pc + g, O_AK // pc + g, O_AV // pc + g]

        def a_attn(g):
            win, dil = A_PATTERNS[g]
            return dict(rows_per_seq=s_len // dil, qcol=0, kcol=A_OUT, vcol=2 * A_OUT,
                        n_q=A_HEADS_PER_GROUP, n_kv=A_HEADS_PER_GROUP, slopes=slopes_a,
                        max_dist=win // dil, dist_scale=dil, tq=1024, out_dtype=F32, unit=1, write_lse=True)

        a_mode = "NNNNNNNNPPPP"
        n8 = "NNNNNNNN"
        sz = proj(h, w_in, l, [(O_ZB // pc, O_ZC // pc, O_ZA // pc), (O_ZB // pc + 1, O_ZC // pc + 1, O_ZM // pc)],
                  gain_z, ("SSSSSSSS",) * 3, 1024, F32, "proj_z")
        ap0 = proj(h_perm[0], w_in, l, [(c,) for c in a_cols(0)], gain_a, (a_mode,), 1024, BF16, "proj_a0")
        ap1, o0, l0 = proj_fused(h_perm[1], w_in, l, a_cols(1), gain_a, a_mode, 1024, BF16,
                                 banded_extra(ap0, ap0, **a_attn(0)), "proj_a1_attn_a0")
        ap2, o1, l1 = proj_fused(h_perm[2], w_in, l, a_cols(2), gain_a, a_mode, 1024, BF16,
                                 banded_extra(ap1, ap1, **a_attn(1)), "proj_a2_attn_a1")
        bqa, o2, l2 = proj_fused(h, w_in, l, [O_BQ // pc, O_BQ // pc + 1], gain_bc[:, :g_bm], n8, 1024, BF16,
                                 banded_extra(ap2, ap2, **a_attn(2)), "proj_bq_attn_a2")
        bm = proj(h, w_in, l, [(O_BK // pc,), (O_MQ // pc,)], gain_bc[:, g_bm:g_cq], ("NNPPNNNN",),
                  1024, BF16, "proj_bm")
        mn = rmsnorm_bf16(memf, mem_norm_g[l], 256)[0]
        mkv = proj(mn, w_mem_kv, l, [(0,), (1,)], gain_m, ("NNNNPPPP",), bsz * N_MEM, BF16, "proj_mem")
        cq, u_b = proj_fused(
            h, w_in, l, [O_CQ // pc, O_CQ // pc + 1], gain_bc[:, g_cq:g_ck], n8, 1024, BF16,
            banded_extra(bqa, bm, rows_per_seq=s_len, qcol=0, kcol=0, vcol=B_KV_HEADS * HEAD_DIM,
                         n_q=B_Q_HEADS, n_kv=B_KV_HEADS, slopes=slopes_b, max_dist=B_WINDOW - 1, dist_scale=1,
                         tq=1024, out_dtype=BF16, batch=1, sinks=sinks[l], z=sz, zcol=z_zb), "proj_cq_attn_b")
        ck, u_m = proj_fused(
            h, w_in, l, [O_CK // pc, O_CK // pc + 1], gain_bc[:, g_ck:g_cv], n8, 1024, BF16,
            mem_extra(bm, mkv, sz, seq=s_len, qcol=2 * B_KV_HEADS * HEAD_DIM, zcol=z_zm, tq=1024),
            "proj_ck_attn_m")
        cv, u_a = proj_fused(h, w_in, l, [O_CV // pc, O_CV // pc + 1], gain_bc[:, g_cv:], "PPPPPPPP", 512, BF16,
                             combine_extra([o0, o1, o2], [l0, l1, l2], sz, z_za, 512, dils[1:]),
                             "proj_cv_combine_a")

        lam_init = 0.8 - 0.6 * math.exp(-0.3 * l)
        u_c = diff_attention(cq, ck, cv, slopes_c, lam[l], subln_g[l], sz, seq=s_len,
                             zcol=z_zc, tq=512, tk=256, lam_init=lam_init)

        merged = gated_merge(h, [u_a, u_b, u_c, u_m], w_in, b_gate3, w_branch, l, 1024, 256)
        xf = out_proj(xf, merged, w_out, l, 1024, 1024)

    return xf.reshape(bsz, s_len, d)
```

```python
import functools
import math

import jax
import jax.numpy as jnp
from jax import lax
from jax.experimental import pallas as pl
from jax.experimental.pallas import tpu as pltpu

D_MODEL = 2048
HEAD_DIM = 128
BLOCK = 128
EPS = 1e-6
A_PATTERNS = ((128, 1), (512, 4), (2048, 16))
A_GROUPS = 3
A_HEADS_PER_GROUP = 4
A_HEADS = A_GROUPS * A_HEADS_PER_GROUP
A_OUT = A_HEADS_PER_GROUP * HEAD_DIM
B_Q_HEADS = 8
B_KV_HEADS = 2
B_WINDOW = 128
B_OUT = B_Q_HEADS * HEAD_DIM
C_HEADS = 4
C_OUT = C_HEADS * 2 * HEAD_DIM
N_MEM = 256
M_HEADS = 4
M_OUT = M_HEADS * HEAD_DIM
N_BRANCH = 4
QK_A_Q, QK_A_K, QK_B_Q, QK_B_K, QK_C_Q, QK_C_K, QK_M_Q, QK_M_K = range(8)

_IN_SIZES = (A_HEADS * HEAD_DIM, A_HEADS * HEAD_DIM, A_HEADS * HEAD_DIM,
             B_Q_HEADS * HEAD_DIM, B_KV_HEADS * HEAD_DIM, B_KV_HEADS * HEAD_DIM,
             C_HEADS * 2 * HEAD_DIM, C_HEADS * 2 * HEAD_DIM, C_OUT,
             M_HEADS * HEAD_DIM, A_OUT, B_OUT, C_OUT, M_OUT, N_BRANCH * D_MODEL)
_OFF = [0]
for _s in _IN_SIZES:
    _OFF.append(_OFF[-1] + _s)
(O_AQ, O_AK, O_AV, O_BQ, O_BK, O_BV, O_CQ, O_CK, O_CV, O_MQ,
 O_ZA, O_ZB, O_ZC, O_ZM, O_GATE, O_END) = _OFF

NEG = -1e30
SCALE = HEAD_DIM ** -0.5
VMEM_LIMIT = 58 * 1024 * 1024
PROJ_PIECE = 512
DIFF_UNROLL = (8, 6, 4, 2, 1)
LOG2E = math.log2(math.e)
BF16 = jnp.bfloat16
F32 = jnp.float32
CONTRACT_LAST = (((1,), (1,)), ((), ()))
CONTRACT_FIRST = (((0,), (0,)), ((), ()))


def _params(n_axes):
    return pltpu.CompilerParams(dimension_semantics=("arbitrary",) * n_axes,
                                vmem_limit_bytes=VMEM_LIMIT)


def _sigmoid(x):
    return 0.5 * jnp.tanh(0.5 * x) + 0.5


def _alibi_slopes(n):
    return [2.0 ** (-8.0 * i / n) for i in range(1, n + 1)]


def _rmsnorm_kernel(x_ref, g_ref, o_ref, *perm_refs, dils, tm):
    x = x_ref[...]
    ms = jnp.mean(x * x, axis=-1, keepdims=True)
    h = (x * lax.rsqrt(ms + EPS) * g_ref[...]).astype(BF16)
    o_ref[...] = h
    row = lax.broadcasted_iota(jnp.int32, (tm, tm), 0)
    col = lax.broadcasted_iota(jnp.int32, (tm, tm), 1)
    for d, p_ref in zip(dils, perm_refs):
        n = tm // d
        src = (row % n) * d + row // n
        perm = jnp.where(col == src, 1.0, 0.0).astype(BF16)
        hp = jnp.dot(perm, h, preferred_element_type=F32).astype(BF16)
        for p in range(d):
            p_ref[p] = hp[p * n:(p + 1) * n, :]


def rmsnorm_bf16(x, g, tm, dils=()):
    t, d = x.shape
    out_shape = [jax.ShapeDtypeStruct((t, d), BF16)]
    out_specs = [pl.BlockSpec((tm, d), lambda i: (i, 0))]
    for dil in dils:
        out_shape.append(jax.ShapeDtypeStruct((dil, t // dil, d), BF16))
        out_specs.append(pl.BlockSpec((dil, tm // dil, d), lambda i: (0, i, 0)))
    outs = pl.pallas_call(
        functools.partial(_rmsnorm_kernel, dils=tuple(dils), tm=tm),
        out_shape=out_shape,
        grid=(t // tm,),
        in_specs=[pl.BlockSpec((tm, d), lambda i: (i, 0)),
                  pl.BlockSpec((1, d), lambda i: (0, 0))],
        out_specs=out_specs,
        compiler_params=_params(1),
        name="rmsnorm",
    )(x, g.reshape(1, d))
    return outs


def _proj_kernel(*refs, tile_modes, n_piece):
    h_ref, w_refs = refs[0], refs[1:1 + n_piece]
    gain_ref, o_ref, w_sc = refs[1 + n_piece:]
    _proj_body(h_ref, w_refs, gain_ref, o_ref, w_sc, tile_modes)


def _proj_body(h_ref, w_refs, gain_ref, o_ref, w_sc, tile_modes):
    n_piece = len(w_refs)
    j = pl.program_id(0)

    @pl.when(pl.program_id(1) == 0)
    def _():
        for p in range(n_piece):
            w_sc[:, p * PROJ_PIECE:(p + 1) * PROJ_PIECE] = w_refs[p][...].astype(BF16)

    acc = jnp.dot(h_ref[...], w_sc[...], preferred_element_type=F32)
    for c in range(len(tile_modes[0])):
        sl = slice(c * HEAD_DIM, (c + 1) * HEAD_DIM)
        modes = [m[c] for m in tile_modes]
        y = acc[:, sl]
        if 'S' in modes:
            assert set(modes) == {'S'}
            y = y * _sigmoid(y)
        elif 'N' in modes:
            ms = jnp.mean(y * y, axis=-1, keepdims=True)
            normed = y * lax.rsqrt(ms + EPS) * gain_ref[:, sl]
            if set(modes) == {'N'}:
                y = normed
            else:
                is_norm = functools.reduce(jnp.logical_or, [j == jj for jj, m in enumerate(modes) if m == 'N'])
                y = jnp.where(is_norm, normed, y)
        o_ref[:, sl] = y.astype(o_ref.dtype)


def _select(j, values):
    out = values[-1]
    for jj in range(len(values) - 2, -1, -1):
        out = jnp.where(j == jj, values[jj], out)
    return out


def proj(h, w3, layer, piece_cols, gain, tile_modes, tm, out_dtype, name):
    t, k = h.shape
    n_piece = len(piece_cols)
    tn = n_piece * PROJ_PIECE
    n_tiles = len(tile_modes)
    assert t % tm == 0 and all(len(m) == tn // HEAD_DIM for m in tile_modes)
    assert all(len(cols) == n_tiles for cols in piece_cols) and gain.shape == (1, tn * n_tiles)
    w_specs = [pl.BlockSpec((None, k, PROJ_PIECE),
                            functools.partial(lambda j, i, cols: (layer, 0, _select(j, cols)), cols=tuple(cols)))
               for cols in piece_cols]
    return pl.pallas_call(
        functools.partial(_proj_kernel, tile_modes=tuple(tile_modes), n_piece=n_piece),
        out_shape=jax.ShapeDtypeStruct((t, tn * n_tiles), out_dtype),
        grid=(n_tiles, t // tm),
        in_specs=[pl.BlockSpec((tm, k), lambda j, i: (i, 0))] + w_specs
                 + [pl.BlockSpec((1, tn), lambda j, i: (0, j))],
        out_specs=pl.BlockSpec((tm, tn), lambda j, i: (i, j)),
        scratch_shapes=[pltpu.VMEM((k, tn), BF16)],
        compiler_params=_params(2),
        name=name,
    )(h, *([w3] * n_piece), gain)


def _banded_body(q_ref, kc_ref, kp_ref, vc_ref, vp_ref, sink_ref, z_ref, o_ref, lse_ref, *, tile,
                 n_q, n_kv, slopes, max_dist, dist_scale, tq, rows_per_seq, unit, batch, between=None):
    has_sink, mul_z, write_lse = sink_ref is not None, z_ref is not None, lse_ref is not None
    shared_kv = n_kv < n_q and unit > 1
    group = n_q // n_kv
    assert n_q % unit == 0 and (not shared_kv or group == unit)
    kk = lax.broadcasted_iota(jnp.int32, (2 * BLOCK, BLOCK), 0)
    r = lax.broadcasted_iota(jnp.int32, (2 * BLOCK, BLOCK), 1)
    dist = r + BLOCK - kk
    distf = (dist * dist_scale).astype(F32)
    valid = (dist >= 0) & (dist <= max_dist)
    valid_start = valid & (kk >= BLOCK)
    if rows_per_seq > tq:
        has_prev = jnp.minimum(tile % (rows_per_seq // tq), 1) * BLOCK
        valid0 = valid & (kk + has_prev >= BLOCK)
    else:
        valid0 = valid_start

    def window(cur_ref, prev_ref, kv, j):
        ks = slice(kv * HEAD_DIM, (kv + 1) * HEAD_DIM)
        prev = prev_ref[:, ks] if j == 0 else cur_ref[(j - 1) * BLOCK:j * BLOCK, ks]
        return jnp.concatenate([prev, cur_ref[j * BLOCK:(j + 1) * BLOCK, ks]], axis=0)

    units = [(list(range(u * unit, (u + 1) * unit)), j)
             for u in range(n_q // unit) for j in range(tq // BLOCK)]

    for b0 in range(0, len(units), batch or len(units)):
        batch_units = units[b0:b0 + (batch or len(units))]
        scores = []
        for heads, j in batch_units:
            rows = slice(j * BLOCK, (j + 1) * BLOCK)
            qs = [q_ref[rows, h * HEAD_DIM:(h + 1) * HEAD_DIM] for h in heads]
            if shared_kv:
                s_t = lax.dot_general(window(kc_ref, kp_ref, heads[0] // group, j), jnp.concatenate(qs, axis=0),
                                      CONTRACT_LAST, preferred_element_type=F32)
            else:
                s_t = jnp.concatenate(
                    [lax.dot_general(window(kc_ref, kp_ref, h // group, j), q, CONTRACT_LAST,
                                     preferred_element_type=F32) for h, q in zip(heads, qs)], axis=1)
            scores.append(s_t)
        if between is not None and b0 == 0:
            between()

        probs, lses = [], []
        for (heads, j), s_t in zip(batch_units, scores):
            v_mask = valid0 if j == 0 else (valid_start if (j * BLOCK) % rows_per_seq == 0 else valid)
            bias = jnp.concatenate([jnp.where(v_mask, (-slopes[h] * LOG2E) * distf, NEG) for h in heads], axis=1)
            s_t = s_t * (SCALE * LOG2E) + bias
            m = jnp.max(s_t, axis=0, keepdims=True)
            if has_sink:
                sink = jnp.concatenate([jnp.full((1, BLOCK), sink_ref[h] * LOG2E, F32) for h in heads], axis=1)
                m = jnp.maximum(m, sink)
            e = jnp.exp2(s_t - m)
            denom = jnp.sum(e, axis=0, keepdims=True)
            if has_sink:
                denom = denom + jnp.exp2(sink - m)
            probs.append((e * (1.0 / denom)).astype(BF16))
            lses.append(m + jnp.log2(denom) if write_lse else None)

        for (heads, j), p_t, lse in zip(batch_units, probs, lses):
            rows = slice(j * BLOCK, (j + 1) * BLOCK)
            if shared_kv:
                o_all = lax.dot_general(p_t, window(vc_ref, vp_ref, heads[0] // group, j), CONTRACT_FIRST,
                                        preferred_element_type=F32)
            for g, h in enumerate(heads):
                hs = slice(h * HEAD_DIM, (h + 1) * HEAD_DIM)
                gs = slice(g * BLOCK, (g + 1) * BLOCK)
                if shared_kv:
                    o = o_all[gs, :]
                else:
                    o = lax.dot_general(p_t[:, gs], window(vc_ref, vp_ref, h // group, j), CONTRACT_FIRST,
                                        preferred_element_type=F32)
                if mul_z:
                    o = o * z_ref[rows, hs]
                o_ref[rows, hs] = o.astype(o_ref.dtype)
                if write_lse:
                    lse_ref[rows, hs] = jnp.broadcast_to(lse[:, gs], (BLOCK, BLOCK)).T


def _banded_specs(q_arr, kv_arr, *, qcol, kcol, vcol, n_q, n_kv, tq, out_dtype, sinks, z, zcol, write_lse,
                  tile_of):
    rows_total = q_arr.shape[0]
    qw, kw = n_q * HEAD_DIM, n_kv * HEAD_DIM
    assert qcol % qw == 0 and kcol % kw == 0 and vcol % kw == 0
    sub = tq // BLOCK

    def cur(col, w):
        return lambda *g: (tile_of(*g), col // w)

    def prev(col, w):
        return lambda *g: (jnp.maximum(tile_of(*g) * sub - 1, 0), col // w)

    in_specs = [
        pl.BlockSpec((tq, qw), cur(qcol, qw)),
        pl.BlockSpec((tq, kw), cur(kcol, kw)),
        pl.BlockSpec((BLOCK, kw), prev(kcol, kw)),
        pl.BlockSpec((tq, kw), cur(vcol, kw)),
        pl.BlockSpec((BLOCK, kw), prev(vcol, kw)),
    ]
    args = [q_arr, kv_arr, kv_arr, kv_arr, kv_arr]
    if sinks is not None:
        in_specs.append(pl.BlockSpec(memory_space=pltpu.SMEM))
        args.append(sinks)
    if z is not None:
        assert zcol % qw == 0
        in_specs.append(pl.BlockSpec((tq, qw), cur(zcol, qw)))
        args.append(z)
    out_block = pl.BlockSpec((tq, qw), cur(0, qw))
    out_shape = [jax.ShapeDtypeStruct((rows_total, qw), out_dtype)]
    out_specs = [out_block]
    if write_lse:
        out_shape.append(jax.ShapeDtypeStruct((rows_total, qw), F32))
        out_specs.append(out_block)
    return in_specs, args, out_shape, out_specs


def _banded_cfg(*, rows_per_seq, n_q, n_kv, slopes, max_dist, dist_scale, tq, unit, batch):
    assert tq % rows_per_seq == 0 or rows_per_seq % tq == 0
    return dict(n_q=n_q, n_kv=n_kv, slopes=tuple(slopes), max_dist=max_dist, dist_scale=dist_scale,
                tq=tq, rows_per_seq=rows_per_seq, unit=unit, batch=batch)


def _proj_fused_kernel(*refs, tile_mode, n_piece, n_in, n_out, body):
    h_ref, w_refs, gain_ref = refs[0], refs[1:1 + n_piece], refs[1 + n_piece]
    extra_in = refs[2 + n_piece:2 + n_piece + n_in]
    p_out = refs[2 + n_piece + n_in]
    extra_out = refs[3 + n_piece + n_in:3 + n_piece + n_in + n_out]
    w_sc = refs[3 + n_piece + n_in + n_out]
    extra_scratch = refs[4 + n_piece + n_in + n_out:]
    body(extra_in, extra_out, extra_scratch, pl.program_id(1),
         lambda: _proj_body(h_ref, w_refs, gain_ref, p_out, w_sc, (tile_mode,)))


def proj_fused(h, w3, layer, piece_cols, gain, tile_mode, tm, out_dtype, extra, name):
    t, k = h.shape
    n_piece = len(piece_cols)
    tn = n_piece * PROJ_PIECE
    assert len(tile_mode) == tn // HEAD_DIM and gain.shape == (1, tn)
    e_specs, e_args, e_shape, e_out_specs, e_scratch, body = extra
    w_specs = [pl.BlockSpec((None, k, PROJ_PIECE), functools.partial(lambda j, i, c: (layer, 0, c), c=c))
               for c in piece_cols]
    return pl.pallas_call(
        functools.partial(_proj_fused_kernel, tile_mode=tile_mode, n_piece=n_piece,
                          n_in=len(e_specs), n_out=len(e_shape), body=body),
        out_shape=[jax.ShapeDtypeStruct((t, tn), out_dtype)] + list(e_shape),
        grid=(1, t // tm),
        in_specs=[pl.BlockSpec((tm, k), lambda j, i: (i, 0))] + w_specs
                 + [pl.BlockSpec((1, tn), lambda j, i: (0, 0))] + list(e_specs),
        out_specs=[pl.BlockSpec((tm, tn), lambda j, i: (i, 0))] + list(e_out_specs),
        scratch_shapes=[pltpu.VMEM((k, tn), BF16)] + list(e_scratch),
        compiler_params=_params(2),
        name=name,
    )(h, *([w3] * n_piece), gain, *e_args)


def banded_extra(q_arr, kv_arr, *, rows_per_seq, qcol, kcol, vcol, n_q, n_kv, slopes, max_dist, dist_scale,
                 tq, out_dtype, unit=4, batch=None, sinks=None, z=None, zcol=0, write_lse=False):
    in_specs, args, out_shape, out_specs = _banded_specs(
        q_arr, kv_arr, qcol=qcol, kcol=kcol, vcol=vcol, n_q=n_q, n_kv=n_kv, tq=tq, out_dtype=out_dtype,
        sinks=sinks, z=z, zcol=zcol, write_lse=write_lse, tile_of=lambda j, i: i)
    cfg = _banded_cfg(rows_per_seq=rows_per_seq, n_q=n_q, n_kv=n_kv, slopes=slopes, max_dist=max_dist,
                      dist_scale=dist_scale, tq=tq, unit=unit, batch=batch)
    has_sink, mul_z = sinks is not None, z is not None

    def body(in_refs, out_refs, scratch_refs, tile, proj_fn):
        in_refs = list(in_refs)
        sink_ref = in_refs[5] if has_sink else None
        z_ref = in_refs[5 + has_sink] if mul_z else None
        lse_ref = out_refs[1] if write_lse else None
        _banded_body(*in_refs[:5], sink_ref, z_ref, out_refs[0], lse_ref, tile=tile, between=proj_fn, **cfg)

    return in_specs, args, out_shape, out_specs, [], body


def _combine_kernel(o0, o1, o2, l0, l1, l2, z_ref, u_ref, o1_sc, o2_sc, l1_sc, l2_sc, *, tm, dils):
    n_chunk = o0.shape[1] // HEAD_DIM
    for src, dst, d in ((o1, o1_sc, dils[0]), (l1, l1_sc, dils[0]),
                        (o2, o2_sc, dils[1]), (l2, l2_sc, dils[1])):
        n = tm // d
        for p in range(d):
            for c in range(n_chunk):
                dst[c, pl.ds(p, n, stride=d), :] = src[p, :, c * HEAD_DIM:(c + 1) * HEAD_DIM]
    for c in range(n_chunk):
        cs = slice(c * HEAD_DIM, (c + 1) * HEAD_DIM)
        a0, a1, a2 = l0[:, cs], l1_sc[c], l2_sc[c]
        m = jnp.maximum(jnp.maximum(a0, a1), a2)
        e0, e1, e2 = jnp.exp2(a0 - m), jnp.exp2(a1 - m), jnp.exp2(a2 - m)
        inv = 1.0 / (e0 + e1 + e2)
        ya = (e0 * inv) * o0[:, cs] + (e1 * inv) * o1_sc[c] + (e2 * inv) * o2_sc[c]
        u_ref[:, cs] = (ya * z_ref[:, cs]).astype(u_ref.dtype)


def combine_extra(outs, lses, z, zcol, tm, dils):
    t, w = outs[0].shape
    blk = pl.BlockSpec((tm, w), lambda j, i: (i, 0))

    def pblk(d):
        return pl.BlockSpec((d, tm // d, w), lambda j, i: (0, i, 0))

    def p3(a, d):
        return a.reshape(d, t // d, w)

    d1, d2 = dils
    in_specs = [blk, pblk(d1), pblk(d2), blk, pblk(d1), pblk(d2),
                pl.BlockSpec((tm, w), lambda j, i: (i, zcol // w))]
    args = [outs[0], p3(outs[1], d1), p3(outs[2], d2), lses[0], p3(lses[1], d1), p3(lses[2], d2), z]

    def body(in_refs, out_refs, scratch_refs, tile, proj_fn):
        _combine_kernel(*in_refs, out_refs[0], *scratch_refs, tm=tm, dils=dils)
        proj_fn()

    return (in_specs, args, [jax.ShapeDtypeStruct((t, w), BF16)], [blk],
            [pltpu.VMEM((w // HEAD_DIM, tm, HEAD_DIM), F32)] * 4, body)


def _diff_kernel(slope_ref, q_ref, k_ref, v_ref, lam_ref, g_ref, z_ref, o_ref,
                 vt_sc, acc_sc, bias_sc, s_sc, *, tq, tk, seq, lam_init):
    slope = slope_ref[pl.program_id(1)]
    per_q = tq // tk
    to_log2 = SCALE * LOG2E

    for cidx in range(seq // tk):
        rows = slice(cidx * tk, (cidx + 1) * tk)
        vt_sc[:, rows] = v_ref[rows, :].astype(F32).T.astype(BF16)
    kk = lax.broadcasted_iota(jnp.int32, (tk, tq), 0)
    r = lax.broadcasted_iota(jnp.int32, (tk, tq), 1)
    bias = (-slope / SCALE) * (r - kk).astype(F32)
    bias_sc[0] = bias
    for dblk in range(per_q):
        bias_sc[1 + dblk] = jnp.where(r >= kk + dblk * tk, bias, NEG)

    lp = lam_ref[...]
    lam_full = (jnp.exp(jnp.sum(lp[0:1] * lp[1:2], axis=-1, keepdims=True))
                - jnp.exp(jnp.sum(lp[2:3] * lp[3:4], axis=-1, keepdims=True)) + lam_init)
    out_gain = g_ref[...] * (1.0 - lam_init)

    def query_tile(qi, carry):
        base = qi * per_q
        q0 = pl.multiple_of(qi * tq, tq)
        qs = [q_ref[pl.ds(q0, tq), c * HEAD_DIM:(c + 1) * HEAD_DIM] for c in range(2)]

        def shift_of(kj):
            return (slope * LOG2E) * ((base - kj) * tk).astype(F32)

        def max_chunk(kj0, n, ms):
            ms = list(ms)
            for u in range(n):
                kj = kj0 + u
                start = pl.multiple_of(kj * tk, tk)
                b = bias_sc[jnp.maximum(kj - base + 1, 0)]
                for comp in range(2):
                    k = k_ref[pl.ds(start, tk), comp * HEAD_DIM:(comp + 1) * HEAD_DIM]
                    a = (lax.dot_general(k, qs[comp], CONTRACT_LAST, preferred_element_type=F32) + b) * to_log2
                    s_sc[comp, pl.ds(start, tk), :] = a
                    ms[comp] = jnp.maximum(ms[comp], jnp.max(a, axis=0, keepdims=True) - shift_of(kj))
            return tuple(ms)

        def over_blocks(chunk_fn, carry):
            done = 0
            for n in DIFF_UNROLL:
                if n % per_q:
                    continue
                n_it = (base + per_q - done) // n
                carry = lax.fori_loop(
                    0, n_it, lambda t, c, n=n, done=done: chunk_fn(done + t * n, n, c), carry)
                done = done + n_it * n
            return carry

        neg = jnp.full((1, tq), NEG, F32)
        ms = over_blocks(max_chunk, (neg, neg))

        def exp_chunk(kj0, n, ls):
            ls = list(ls)
            start0 = pl.multiple_of(kj0 * tk, tk)
            vt = vt_sc[:, pl.ds(start0, n * tk)]
            for comp in range(2):
                es = []
                for u in range(n):
                    start = pl.multiple_of((kj0 + u) * tk, tk)
                    e = jnp.exp2(s_sc[comp, pl.ds(start, tk), :] - (ms[comp] + shift_of(kj0 + u)))
                    ls[comp] = ls[comp] + jnp.sum(e, axis=0, keepdims=True)
                    es.append(e.astype(BF16))
                e_cat = es[0] if n == 1 else jnp.concatenate(es, axis=0)
                acc_sc[comp] += jnp.dot(vt, e_cat, preferred_element_type=F32)
            return tuple(ls)

        acc_sc[...] = jnp.zeros(acc_sc.shape, F32)
        zero = jnp.zeros((1, tq), F32)
        l1, l2 = over_blocks(exp_chunk, (zero, zero))

        y_t = acc_sc[0] * (1.0 / l1) - lam_full * (acc_sc[1] * (1.0 / l2))
        ms2 = jnp.mean(y_t * y_t, axis=0, keepdims=True)
        y = (y_t * lax.rsqrt(ms2 + EPS)).T * out_gain
        o_ref[pl.ds(q0, tq), :] = (y * z_ref[pl.ds(q0, tq), :]).astype(o_ref.dtype)
        return carry

    lax.fori_loop(0, seq // tq, query_tile, 0)


def diff_attention(cq, ck, cv, slopes, lam_l, subln_g, z, *, seq, zcol, tq, tk, lam_init):
    t = cq.shape[0]
    n_batch = t // seq
    hw = 2 * HEAD_DIM
    kern = functools.partial(_diff_kernel, tq=tq, tk=tk, seq=seq, lam_init=lam_init)
    whole = pl.BlockSpec((seq, hw), lambda b, h: (b, h))
    return pl.pallas_call(
        kern,
        out_shape=jax.ShapeDtypeStruct((t, C_OUT), BF16),
        grid=(n_batch, C_HEADS),
        in_specs=[
            pl.BlockSpec(memory_space=pltpu.SMEM),
            whole, whole, whole,
            pl.BlockSpec((4, HEAD_DIM), lambda b, h: (0, 0)),
            pl.BlockSpec((1, hw), lambda b, h: (0, 0)),
            pl.BlockSpec((seq, hw), lambda b, h: (b, zcol // hw + h)),
        ],
        out_specs=whole,
        scratch_shapes=[pltpu.VMEM((hw, seq), BF16), pltpu.VMEM((2, hw, tq), F32),
                        pltpu.VMEM((1 + tq // tk, tk, tq), F32), pltpu.VMEM((2, seq, tq), F32)],
        compiler_params=_params(2),
        name="diff_attention",
    )(slopes, cq, ck, cv, lam_l, subln_g.reshape(1, hw), z)


def _mem_kernel(q_ref, k_ref, v_ref, z_ref, o_ref):
    for h in range(M_HEADS):
        hs = slice(h * HEAD_DIM, (h + 1) * HEAD_DIM)
        s_t = lax.dot_general(k_ref[:, hs], q_ref[:, hs], CONTRACT_LAST,
                              preferred_element_type=F32) * (SCALE * LOG2E)
        m = jnp.max(s_t, axis=0, keepdims=True)
        e = jnp.exp2(s_t - m)
        p_t = (e * (1.0 / jnp.sum(e, axis=0, keepdims=True))).astype(BF16)
        o = lax.dot_general(p_t, v_ref[:, hs], CONTRACT_FIRST, preferred_element_type=F32)
        o_ref[:, hs] = (o * z_ref[:, hs]).astype(o_ref.dtype)


def mem_extra(cp, mkv, z, *, seq, qcol, zcol, tq):
    t = cp.shape[0]
    nq = seq // tq
    in_specs = [
        pl.BlockSpec((tq, M_OUT), lambda j, i: (i, qcol // M_OUT)),
        pl.BlockSpec((N_MEM, M_OUT), lambda j, i: (i // nq, 0)),
        pl.BlockSpec((N_MEM, M_OUT), lambda j, i: (i // nq, 1)),
        pl.BlockSpec((tq, M_OUT), lambda j, i: (i, zcol // M_OUT)),
    ]
    out_shape = [jax.ShapeDtypeStruct((t, M_OUT), BF16)]
    out_specs = [pl.BlockSpec((tq, M_OUT), lambda j, i: (i, 0))]

    def body(in_refs, out_refs, scratch_refs, tile, proj_fn):
        proj_fn()
        _mem_kernel(*in_refs, out_refs[0])

    return in_specs, [cp, mkv, mkv, z], out_shape, out_specs, [], body


_BRANCH_ROWS = ((0, A_OUT), (A_OUT, A_OUT + B_OUT), (A_OUT + B_OUT, A_OUT + B_OUT + C_OUT),
                (A_OUT + B_OUT + C_OUT, A_OUT + B_OUT + C_OUT + M_OUT))
_WB_CHUNK = 512


def _merge_kernel(*refs):
    h_ref, us = refs[0], refs[1:5]
    wgs, bgs = refs[5:9], refs[9:13]
    n_chunk = (A_OUT + B_OUT + C_OUT + M_OUT) // _WB_CHUNK
    wbs = refs[13:13 + n_chunk]
    o_ref, wg_sc, wb_sc = refs[13 + n_chunk:]

    @pl.when(pl.program_id(1) == 0)
    def _():
        for b in range(N_BRANCH):
            wg_sc[b] = wgs[b][...].astype(BF16)
        for c in range(n_chunk):
            wb_sc[c * _WB_CHUNK:(c + 1) * _WB_CHUNK, :] = wbs[c][...].astype(BF16)

    h = h_ref[...]
    acc = None
    for b in range(N_BRANCH):
        lo, hi = _BRANCH_ROWS[b]
        gate = _sigmoid(jnp.dot(h, wg_sc[b], preferred_element_type=F32) + bgs[b][...])
        term = gate * jnp.dot(us[b][...], wb_sc[lo:hi, :], preferred_element_type=F32)
        acc = term if acc is None else acc + term
    o_ref[...] = acc.astype(o_ref.dtype)


def gated_merge(h, us, w_in, b_gate3, w_branch, layer, tm, tn):
    t, d = h.shape
    nj = d // tn
    gate0 = O_GATE // tn
    n_chunk = w_branch.shape[1] // _WB_CHUNK
    in_specs = [pl.BlockSpec((tm, d), lambda j, i: (i, 0))]
    in_specs += [pl.BlockSpec((tm, u.shape[1]), lambda j, i: (i, 0)) for u in us]
    in_specs += [pl.BlockSpec((None, d, tn), functools.partial(
        lambda j, i, b: (layer, 0, gate0 + b * nj + j), b=b)) for b in range(N_BRANCH)]
    in_specs += [pl.BlockSpec((None, 1, tn), functools.partial(
        lambda j, i, b: (layer, 0, b * nj + j), b=b)) for b in range(N_BRANCH)]
    in_specs += [pl.BlockSpec((None, _WB_CHUNK, tn), functools.partial(
        lambda j, i, c: (layer, c, j), c=c)) for c in range(n_chunk)]
    return pl.pallas_call(
        _merge_kernel,
        out_shape=jax.ShapeDtypeStruct((t, d), BF16),
        grid=(nj, t // tm),
        in_specs=in_specs,
        out_specs=pl.BlockSpec((tm, tn), lambda j, i: (i, j)),
        scratch_shapes=[pltpu.VMEM((N_BRANCH, d, tn), BF16),
                        pltpu.VMEM((w_branch.shape[1], tn), BF16)],
        compiler_params=_params(2),
        name="gated_merge",
    )(h, *us, *([w_in] * N_BRANCH), *([b_gate3] * N_BRANCH), *([w_branch] * n_chunk))


def _out_kernel(x_ref, m_ref, w_ref, o_ref, w_sc):
    @pl.when(pl.program_id(1) == 0)
    def _():
        w_sc[...] = w_ref[...].astype(BF16)

    o_ref[...] = x_ref[...] + jnp.dot(m_ref[...], w_sc[...], preferred_element_type=F32)


def out_proj(x, merged, w_out, layer, tm, tn):
    t, d = x.shape
    return pl.pallas_call(
        _out_kernel,
        out_shape=jax.ShapeDtypeStruct((t, d), F32),
        grid=(d // tn, t // tm),
        in_specs=[pl.BlockSpec((tm, tn), lambda j, i: (i, j)),
                  pl.BlockSpec((tm, d), lambda j, i: (i, 0)),
                  pl.BlockSpec((None, d, tn), lambda j, i: (layer, 0, j))],
        out_specs=pl.BlockSpec((tm, tn), lambda j, i: (i, j)),
        scratch_shapes=[pltpu.VMEM((d, tn), BF16)],
        compiler_params=_params(2),
        name="out_proj",
    )(x, merged, w_out)


def kernel(x, mem, norm_g, w_in, b_gate, qk_gain, sinks, lam, subln_g, mem_norm_g,
           w_mem_kv, w_branch, w_out):
    bsz, s_len, d = x.shape
    depth = w_in.shape[0]
    t = bsz * s_len
    xf = x.reshape(t, d)
    memf = mem.reshape(bsz * N_MEM, d)
    b_gate3 = b_gate.reshape(depth, 1, N_BRANCH * d)

    slopes_a = _alibi_slopes(A_HEADS_PER_GROUP)
    slopes_b = _alibi_slopes(B_Q_HEADS)
    slopes_c = jnp.asarray(_alibi_slopes(C_HEADS), F32)
    ones = lambda n: jnp.ones((n,), F32)
    dils = tuple(dil for _, dil in A_PATTERNS)

    pc = PROJ_PIECE
    z_zb, z_zc, z_za, z_zm = 0, B_OUT, B_OUT + C_OUT, B_OUT + C_OUT + A_OUT
    g_bm = B_OUT
    g_cq = g_bm + 2 * B_KV_HEADS * HEAD_DIM + M_OUT
    g_ck, g_cv = g_cq + C_OUT, g_cq + 2 * C_OUT

    for l in range(depth):
        gq = qk_gain[l]
        gain_a = jnp.concatenate([jnp.tile(gq[QK_A_Q], 4), jnp.tile(gq[QK_A_K], 4), ones(A_OUT)]).reshape(1, -1)
        gain_bc = jnp.concatenate([
            jnp.tile(gq[QK_B_Q], B_Q_HEADS), jnp.tile(gq[QK_B_K], B_KV_HEADS), ones(B_KV_HEADS * HEAD_DIM),
            jnp.tile(gq[QK_M_Q], M_HEADS), jnp.tile(gq[QK_C_Q], 2 * C_HEADS),
            jnp.tile(gq[QK_C_K], 2 * C_HEADS), ones(C_OUT)]).reshape(1, -1)
        gain_z = ones(O_GATE - O_ZA).reshape(1, -1)
        gain_m = jnp.concatenate([jnp.tile(gq[QK_M_K], M_HEADS), ones(M_OUT)]).reshape(1, -1)

        hs = rmsnorm_bf16(xf, norm_g[l], 256, dils[1:])
        h = hs[0]
        h_perm = [h] + [hp.reshape(t, d) for hp in hs[1:]]

        def a_cols(g):
            return [O_AQ // pc + g, O_AK // pc + g, O_AV // pc + g]

        def a_attn(g):
            win, dil = A_PATTERNS[g]
            return dict(rows_per_seq=s_len // dil, qcol=0, kcol=A_OUT, vcol=2 * A_OUT,
                        n_q=A_HEADS_PER_GROUP, n_kv=A_HEADS_PER_GROUP, slopes=slopes_a,
                        max_dist=win // dil, dist_scale=dil, tq=1024, out_dtype=F32, unit=1, write_lse=True)

        a_mode = "NNNNNNNNPPPP"
        n8 = "NNNNNNNN"
        sz = proj(h, w_in, l, [(O_ZB // pc, O_ZC // pc, O_ZA // pc), (O_ZB // pc + 1, O_ZC // pc + 1, O_ZM // pc)],
                  gain_z, ("SSSSSSSS",) * 3, 1024, F32, "proj_z")
        ap0 = proj(h_perm[0], w_in, l, [(c,) for c in a_cols(0)], gain_a, (a_mode,), 1024, BF16, "proj_a0")
        ap1, o0, l0 = proj_fused(h_perm[1], w_in, l, a_cols(1), gain_a, a_mode, 1024, BF16,
                                 banded_extra(ap0, ap0, **a_attn(0)), "proj_a1_attn_a0")
        ap2, o1, l1 = proj_fused(h_perm[2], w_in, l, a_cols(2), gain_a, a_mode, 1024, BF16,
                                 banded_extra(ap1, ap1, **a_attn(1)), "proj_a2_attn_a1")
        bqa, o2, l2 = proj_fused(h, w_in, l, [O_BQ // pc, O_BQ // pc + 1], gain_bc[:, :g_bm], n8, 1024, BF16,
                                 banded_extra(ap2, ap2, **a_attn(2)), "proj_bq_attn_a2")
        bm = proj(h, w_in, l, [(O_BK // pc,), (O_MQ // pc,)], gain_bc[:, g_bm:g_cq], ("NNPPNNNN",),
                  1024, BF16, "proj_bm")
        mn = rmsnorm_bf16(memf, mem_norm_g[l], 256)[0]
        mkv = proj(mn, w_mem_kv, l, [(0,), (1,)], gain_m, ("NNNNPPPP",), bsz * N_MEM, BF16, "proj_mem")
        cq, u_b = proj_fused(
            h, w_in, l, [O_CQ // pc, O_CQ // pc + 1], gain_bc[:, g_cq:g_ck], n8, 1024, BF16,
            banded_extra(bqa, bm, rows_per_seq=s_len, qcol=0, kcol=0, vcol=B_KV_HEADS * HEAD_DIM,
                         n_q=B_Q_HEADS, n_kv=B_KV_HEADS, slopes=slopes_b, max_dist=B_WINDOW - 1, dist_scale=1,
                         tq=1024, out_dtype=BF16, batch=1, sinks=sinks[l], z=sz, zcol=z_zb), "proj_cq_attn_b")
        ck, u_m = proj_fused(
            h, w_in, l, [O_CK // pc, O_CK // pc + 1], gain_bc[:, g_ck:g_cv], n8, 1024, BF16,
            mem_extra(bm, mkv, sz, seq=s_len, qcol=2 * B_KV_HEADS * HEAD_DIM, zcol=z_zm, tq=1024),
            "proj_ck_attn_m")
        cv, u_a = proj_fused(h, w_in, l, [O_CV // pc, O_CV // pc + 1], gain_bc[:, g_cv:], "PPPPPPPP", 512, BF16,
                             combine_extra([o0, o1, o2], [l0, l1, l2], sz, z_za, 512, dils[1:]),
                             "proj_cv_combine_a")

        lam_init = 0.8 - 0.6 * math.exp(-0.3 * l)
        u_c = diff_attention(cq, ck, cv, slopes_c, lam[l], subln_g[l], sz, seq=s_len,
                             zcol=z_zc, tq=512, tk=256, lam_init=lam_init)

        merged = gated_merge(h, [u_a, u_b, u_c, u_m], w_in, b_gate3, w_branch, l, 1024, 256)
        xf = out_proj(xf, merged, w_out, l, 1024, 1024)

    return xf.reshape(bsz, s_len, d)
```

```python
import functools
import math

import jax
import jax.numpy as jnp
from jax import lax
from jax.experimental import pallas as pl
from jax.experimental.pallas import tpu as pltpu

D_MODEL = 2048
HEAD_DIM = 128
BLOCK = 128
EPS = 1e-6
A_PATTERNS = ((128, 1), (512, 4), (2048, 16))
A_GROUPS = 3
A_HEADS_PER_GROUP = 4
A_HEADS = A_GROUPS * A_HEADS_PER_GROUP
A_OUT = A_HEADS_PER_GROUP * HEAD_DIM
B_Q_HEADS = 8
B_KV_HEADS = 2
B_WINDOW = 128
B_OUT = B_Q_HEADS * HEAD_DIM
C_HEADS = 4
C_OUT = C_HEADS * 2 * HEAD_DIM
N_MEM = 256
M_HEADS = 4
M_OUT = M_HEADS * HEAD_DIM
N_BRANCH = 4
QK_A_Q, QK_A_K, QK_B_Q, QK_B_K, QK_C_Q, QK_C_K, QK_M_Q, QK_M_K = range(8)

_IN_SIZES = (A_HEADS * HEAD_DIM, A_HEADS * HEAD_DIM, A_HEADS * HEAD_DIM,
             B_Q_HEADS * HEAD_DIM, B_KV_HEADS * HEAD_DIM, B_KV_HEADS * HEAD_DIM,
             C_HEADS * 2 * HEAD_DIM, C_HEADS * 2 * HEAD_DIM, C_OUT,
             M_HEADS * HEAD_DIM, A_OUT, B_OUT, C_OUT, M_OUT, N_BRANCH * D_MODEL)
_OFF = [0]
for _s in _IN_SIZES:
    _OFF.append(_OFF[-1] + _s)
(O_AQ, O_AK, O_AV, O_BQ, O_BK, O_BV, O_CQ, O_CK, O_CV, O_MQ,
 O_ZA, O_ZB, O_ZC, O_ZM, O_GATE, O_END) = _OFF

NEG = -1e30
SCALE = HEAD_DIM ** -0.5
VMEM_LIMIT = 58 * 1024 * 1024
PROJ_PIECE = 512
DIFF_UNROLL = (8, 6, 4, 2, 1)
LOG2E = math.log2(math.e)
BF16 = jnp.bfloat16
F32 = jnp.float32
CONTRACT_LAST = (((1,), (1,)), ((), ()))
CONTRACT_FIRST = (((0,), (0,)), ((), ()))


def _params(n_axes):
    return pltpu.CompilerParams(dimension_semantics=("arbitrary",) * n_axes,
                                vmem_limit_bytes=VMEM_LIMIT)


def _sigmoid(x):
    return 0.5 * jnp.tanh(0.5 * x) + 0.5


def _alibi_slopes(n):
    return [2.0 ** (-8.0 * i / n) for i in range(1, n + 1)]


def _rmsnorm_kernel(x_ref, g_ref, o_ref, *perm_refs, dils, tm):
    x = x_ref[...]
    ms = jnp.mean(x * x, axis=-1, keepdims=True)
    h = (x * lax.rsqrt(ms + EPS) * g_ref[...]).astype(BF16)
    o_ref[...] = h
    row = lax.broadcasted_iota(jnp.int32, (tm, tm), 0)
    col = lax.broadcasted_iota(jnp.int32, (tm, tm), 1)
    for d, p_ref in zip(dils, perm_refs):
        n = tm // d
        src = (row % n) * d + row // n
        perm = jnp.where(col == src, 1.0, 0.0).astype(BF16)
        hp = jnp.dot(perm, h, preferred_element_type=F32).astype(BF16)
        for p in range(d):
            p_ref[p] = hp[p * n:(p + 1) * n, :]


def rmsnorm_bf16(x, g, tm, dils=()):
    t, d = x.shape
    out_shape = [jax.ShapeDtypeStruct((t, d), BF16)]
    out_specs = [pl.BlockSpec((tm, d), lambda i: (i, 0))]
    for dil in dils:
        out_shape.append(jax.ShapeDtypeStruct((dil, t // dil, d), BF16))
        out_specs.append(pl.BlockSpec((dil, tm // dil, d), lambda i: (0, i, 0)))
    outs = pl.pallas_call(
        functools.partial(_rmsnorm_kernel, dils=tuple(dils), tm=tm),
        out_shape=out_shape,
        grid=(t // tm,),
        in_specs=[pl.BlockSpec((tm, d), lambda i: (i, 0)),
                  pl.BlockSpec((1, d), lambda i: (0, 0))],
        out_specs=out_specs,
        compiler_params=_params(1),
        name="rmsnorm",
    )(x, g.reshape(1, d))
    return outs


def _norm_proj_kernel(*refs, dils, tm, sub, n_piece, tile_mode):
    x_ref, g_ref = refs[0], refs[1]
    w_refs, gain_ref = refs[2:2 + n_piece], refs[2 + n_piece]
    o_ref, h_ref = refs[3 + n_piece], refs[4 + n_piece]
    perm_refs = refs[5 + n_piece:5 + n_piece + len(dils)]
    w_sc = refs[5 + n_piece + len(dils)]
    x = x_ref[...]
    ms = jnp.mean(x * x, axis=-1, keepdims=True)
    h_ref[...] = (x * lax.rsqrt(ms + EPS) * g_ref[...]).astype(BF16)
    _proj_body(h_ref, w_refs, gain_ref, o_ref, w_sc, (tile_mode,))
    row = lax.broadcasted_iota(jnp.int32, (sub, sub), 0)
    col = lax.broadcasted_iota(jnp.int32, (sub, sub), 1)
    for d, p_ref in zip(dils, perm_refs):
        n = sub // d
        perm = jnp.where(col == (row % n) * d + row // n, 1.0, 0.0).astype(BF16)
        for s0 in range(tm // sub):
            hp = jnp.dot(perm, h_ref[s0 * sub:(s0 + 1) * sub, :], preferred_element_type=F32).astype(BF16)
            for p in range(d):
                p_ref[p, s0 * n:(s0 + 1) * n, :] = hp[p * n:(p + 1) * n, :]


def norm_proj(x, g, dils, w3, layer, piece_cols, gain, tile_mode, tm, out_dtype, name):
    t, d = x.shape
    n_piece = len(piece_cols)
    tn = n_piece * PROJ_PIECE
    sub = 256
    assert len(tile_mode) == tn // HEAD_DIM and gain.shape == (1, tn) and tm % sub == 0
    out_shape = [jax.ShapeDtypeStruct((t, tn), out_dtype), jax.ShapeDtypeStruct((t, d), BF16)]
    out_specs = [pl.BlockSpec((tm, tn), lambda j, i: (i, 0)), pl.BlockSpec((tm, d), lambda j, i: (i, 0))]
    for dil in dils:
        out_shape.append(jax.ShapeDtypeStruct((dil, t // dil, d), BF16))
        out_specs.append(pl.BlockSpec((dil, tm // dil, d), lambda j, i: (0, i, 0)))
    w_specs = [pl.BlockSpec((None, d, PROJ_PIECE), functools.partial(lambda j, i, c: (layer, 0, c), c=c))
               for c in piece_cols]
    return pl.pallas_call(
        functools.partial(_norm_proj_kernel, dils=tuple(dils), tm=tm, sub=sub, n_piece=n_piece,
                          tile_mode=tile_mode),
        out_shape=out_shape,
        grid=(1, t // tm),
        in_specs=[pl.BlockSpec((tm, d), lambda j, i: (i, 0)), pl.BlockSpec((1, d), lambda j, i: (0, 0))]
                 + w_specs + [pl.BlockSpec((1, tn), lambda j, i: (0, 0))],
        out_specs=out_specs,
        scratch_shapes=[pltpu.VMEM((d, tn), BF16)],
        compiler_params=_params(2),
        name=name,
    )(x, g.reshape(1, d), *([w3] * n_piece), gain)


def _proj_kernel(*refs, tile_modes, n_piece):
    h_ref, w_refs = refs[0], refs[1:1 + n_piece]
    gain_ref, o_ref, w_sc = refs[1 + n_piece:]
    _proj_body(h_ref, w_refs, gain_ref, o_ref, w_sc, tile_modes)


def _proj_body(h_ref, w_refs, gain_ref, o_ref, w_sc, tile_modes):
    n_piece = len(w_refs)
    j = pl.program_id(0)

    @pl.when(pl.program_id(1) == 0)
    def _():
        for p in range(n_piece):
            w_sc[:, p * PROJ_PIECE:(p + 1) * PROJ_PIECE] = w_refs[p][...].astype(BF16)

    acc = jnp.dot(h_ref[...], w_sc[...], preferred_element_type=F32)
    for c in range(len(tile_modes[0])):
        sl = slice(c * HEAD_DIM, (c + 1) * HEAD_DIM)
        modes = [m[c] for m in tile_modes]
        y = acc[:, sl]
        if 'S' in modes:
            assert set(modes) == {'S'}
            y = y * _sigmoid(y)
        elif 'N' in modes:
            ms = jnp.mean(y * y, axis=-1, keepdims=True)
            normed = y * lax.rsqrt(ms + EPS) * gain_ref[:, sl]
            if set(modes) == {'N'}:
                y = normed
            else:
                is_norm = functools.reduce(jnp.logical_or, [j == jj for jj, m in enumerate(modes) if m == 'N'])
                y = jnp.where(is_norm, normed, y)
        o_ref[:, sl] = y.astype(o_ref.dtype)


def _select(j, values):
    out = values[-1]
    for jj in range(len(values) - 2, -1, -1):
        out = jnp.where(j == jj, values[jj], out)
    return out


def proj(h, w3, layer, piece_cols, gain, tile_modes, tm, out_dtype, name):
    t, k = h.shape
    n_piece = len(piece_cols)
    tn = n_piece * PROJ_PIECE
    n_tiles = len(tile_modes)
    assert t % tm == 0 and all(len(m) == tn // HEAD_DIM for m in tile_modes)
    assert all(len(cols) == n_tiles for cols in piece_cols) and gain.shape == (1, tn * n_tiles)
    w_specs = [pl.BlockSpec((None, k, PROJ_PIECE),
                            functools.partial(lambda j, i, cols: (layer, 0, _select(j, cols)), cols=tuple(cols)))
               for cols in piece_cols]
    return pl.pallas_call(
        functools.partial(_proj_kernel, tile_modes=tuple(tile_modes), n_piece=n_piece),
        out_shape=jax.ShapeDtypeStruct((t, tn * n_tiles), out_dtype),
        grid=(n_tiles, t // tm),
        in_specs=[pl.BlockSpec((tm, k), lambda j, i: (i, 0))] + w_specs
                 + [pl.BlockSpec((1, tn), lambda j, i: (0, j))],
        out_specs=pl.BlockSpec((tm, tn), lambda j, i: (i, j)),
        scratch_shapes=[pltpu.VMEM((k, tn), BF16)],
        compiler_params=_params(2),
        name=name,
    )(h, *([w3] * n_piece), gain)


def _banded_body(q_ref, kc_ref, kp_ref, vc_ref, vp_ref, sink_ref, z_ref, o_ref, lse_ref, *, tile,
                 n_q, n_kv, slopes, max_dist, dist_scale, tq, rows_per_seq, unit, batch, between=None):
    has_sink, mul_z, write_lse = sink_ref is not None, z_ref is not None, lse_ref is not None
    shared_kv = n_kv < n_q and unit > 1
    group = n_q // n_kv
    assert n_q % unit == 0 and (not shared_kv or group == unit)
    kk = lax.broadcasted_iota(jnp.int32, (2 * BLOCK, BLOCK), 0)
    r = lax.broadcasted_iota(jnp.int32, (2 * BLOCK, BLOCK), 1)
    dist = r + BLOCK - kk
    distf = (dist * dist_scale).astype(F32)
    valid = (dist >= 0) & (dist <= max_dist)
    valid_start = valid & (kk >= BLOCK)
    if rows_per_seq > tq:
        has_prev = jnp.minimum(tile % (rows_per_seq // tq), 1) * BLOCK
        valid0 = valid & (kk + has_prev >= BLOCK)
    else:
        valid0 = valid_start

    def window(cur_ref, prev_ref, kv, j):
        ks = slice(kv * HEAD_DIM, (kv + 1) * HEAD_DIM)
        prev = prev_ref[:, ks] if j == 0 else cur_ref[(j - 1) * BLOCK:j * BLOCK, ks]
        return jnp.concatenate([prev, cur_ref[j * BLOCK:(j + 1) * BLOCK, ks]], axis=0)

    units = [(list(range(u * unit, (u + 1) * unit)), j)
             for u in range(n_q // unit) for j in range(tq // BLOCK)]

    for b0 in range(0, len(units), batch or len(units)):
        batch_units = units[b0:b0 + (batch or len(units))]
        scores = []
        for heads, j in batch_units:
            rows = slice(j * BLOCK, (j + 1) * BLOCK)
            qs = [q_ref[rows, h * HEAD_DIM:(h + 1) * HEAD_DIM] for h in heads]
            if shared_kv:
                s_t = lax.dot_general(window(kc_ref, kp_ref, heads[0] // group, j), jnp.concatenate(qs, axis=0),
                                      CONTRACT_LAST, preferred_element_type=F32)
            else:
                s_t = jnp.concatenate(
                    [lax.dot_general(window(kc_ref, kp_ref, h // group, j), q, CONTRACT_LAST,
                                     preferred_element_type=F32) for h, q in zip(heads, qs)], axis=1)
            scores.append(s_t)
        if between is not None and b0 == 0:
            between()

        probs, lses = [], []
        for (heads, j), s_t in zip(batch_units, scores):
            v_mask = valid0 if j == 0 else (valid_start if (j * BLOCK) % rows_per_seq == 0 else valid)
            bias = jnp.concatenate([jnp.where(v_mask, (-slopes[h] * LOG2E) * distf, NEG) for h in heads], axis=1)
            s_t = s_t * (SCALE * LOG2E) + bias
            m = jnp.max(s_t, axis=0, keepdims=True)
            if has_sink:
                sink = jnp.concatenate([jnp.full((1, BLOCK), sink_ref[h] * LOG2E, F32) for h in heads], axis=1)
                m = jnp.maximum(m, sink)
            e = jnp.exp2(s_t - m)
            denom = jnp.sum(e, axis=0, keepdims=True)
            if has_sink:
                denom = denom + jnp.exp2(sink - m)
            probs.append((e * (1.0 / denom)).astype(BF16))
            lses.append(m + jnp.log2(denom) if write_lse else None)

        for (heads, j), p_t, lse in zip(batch_units, probs, lses):
            rows = slice(j * BLOCK, (j + 1) * BLOCK)
            if shared_kv:
                o_all = lax.dot_general(p_t, window(vc_ref, vp_ref, heads[0] // group, j), CONTRACT_FIRST,
                                        preferred_element_type=F32)
            for g, h in enumerate(heads):
                hs = slice(h * HEAD_DIM, (h + 1) * HEAD_DIM)
                gs = slice(g * BLOCK, (g + 1) * BLOCK)
                if shared_kv:
                    o = o_all[gs, :]
                else:
                    o = lax.dot_general(p_t[:, gs], window(vc_ref, vp_ref, h // group, j), CONTRACT_FIRST,
                                        preferred_element_type=F32)
                if mul_z:
                    o = o * z_ref[rows, hs]
                o_ref[rows, hs] = o.astype(o_ref.dtype)
                if write_lse:
                    lse_ref[rows, hs] = jnp.broadcast_to(lse[:, gs], (BLOCK, BLOCK)).T


def _banded_specs(q_arr, kv_arr, *, qcol, kcol, vcol, n_q, n_kv, tq, out_dtype, sinks, z, zcol, write_lse,
                  tile_of):
    rows_total = q_arr.shape[0]
    qw, kw = n_q * HEAD_DIM, n_kv * HEAD_DIM
    assert qcol % qw == 0 and kcol % kw == 0 and vcol % kw == 0
    sub = tq // BLOCK

    def cur(col, w):
        return lambda *g: (tile_of(*g), col // w)

    def prev(col, w):
        return lambda *g: (jnp.maximum(tile_of(*g) * sub - 1, 0), col // w)

    in_specs = [
        pl.BlockSpec((tq, qw), cur(qcol, qw)),
        pl.BlockSpec((tq, kw), cur(kcol, kw)),
        pl.BlockSpec((BLOCK, kw), prev(kcol, kw)),
        pl.BlockSpec((tq, kw), cur(vcol, kw)),
        pl.BlockSpec((BLOCK, kw), prev(vcol, kw)),
    ]
    args = [q_arr, kv_arr, kv_arr, kv_arr, kv_arr]
    if sinks is not None:
        in_specs.append(pl.BlockSpec(memory_space=pltpu.SMEM))
        args.append(sinks)
    if z is not None:
        assert zcol % qw == 0
        in_specs.append(pl.BlockSpec((tq, qw), cur(zcol, qw)))
        args.append(z)
    out_block = pl.BlockSpec((tq, qw), cur(0, qw))
    out_shape = [jax.ShapeDtypeStruct((rows_total, qw), out_dtype)]
    out_specs = [out_block]
    if write_lse:
        out_shape.append(jax.ShapeDtypeStruct((rows_total, qw), F32))
        out_specs.append(out_block)
    return in_specs, args, out_shape, out_specs


def _banded_cfg(*, rows_per_seq, n_q, n_kv, slopes, max_dist, dist_scale, tq, unit, batch):
    assert tq % rows_per_seq == 0 or rows_per_seq % tq == 0
    return dict(n_q=n_q, n_kv=n_kv, slopes=tuple(slopes), max_dist=max_dist, dist_scale=dist_scale,
                tq=tq, rows_per_seq=rows_per_seq, unit=unit, batch=batch)


def _proj_fused_kernel(*refs, tile_mode, n_piece, n_in, n_out, body):
    h_ref, w_refs, gain_ref = refs[0], refs[1:1 + n_piece], refs[1 + n_piece]
    extra_in = refs[2 + n_piece:2 + n_piece + n_in]
    p_out = refs[2 + n_piece + n_in]
    extra_out = refs[3 + n_piece + n_in:3 + n_piece + n_in + n_out]
    w_sc = refs[3 + n_piece + n_in + n_out]
    extra_scratch = refs[4 + n_piece + n_in + n_out:]
    body(extra_in, extra_out, extra_scratch, pl.program_id(1),
         lambda: _proj_body(h_ref, w_refs, gain_ref, p_out, w_sc, (tile_mode,)))


def proj_fused(h, w3, layer, piece_cols, gain, tile_mode, tm, out_dtype, extra, name):
    t, k = h.shape
    n_piece = len(piece_cols)
    tn = n_piece * PROJ_PIECE
    assert len(tile_mode) == tn // HEAD_DIM and gain.shape == (1, tn)
    e_specs, e_args, e_shape, e_out_specs, e_scratch, body = extra
    w_specs = [pl.BlockSpec((None, k, PROJ_PIECE), functools.partial(lambda j, i, c: (layer, 0, c), c=c))
               for c in piece_cols]
    return pl.pallas_call(
        functools.partial(_proj_fused_kernel, tile_mode=tile_mode, n_piece=n_piece,
                          n_in=len(e_specs), n_out=len(e_shape), body=body),
        out_shape=[jax.ShapeDtypeStruct((t, tn), out_dtype)] + list(e_shape),
        grid=(1, t // tm),
        in_specs=[pl.BlockSpec((tm, k), lambda j, i: (i, 0))] + w_specs
                 + [pl.BlockSpec((1, tn), lambda j, i: (0, 0))] + list(e_specs),
        out_specs=[pl.BlockSpec((tm, tn), lambda j, i: (i, 0))] + list(e_out_specs),
        scratch_shapes=[pltpu.VMEM((k, tn), BF16)] + list(e_scratch),
        compiler_params=_params(2),
        name=name,
    )(h, *([w3] * n_piece), gain, *e_args)


def banded_extra(q_arr, kv_arr, *, rows_per_seq, qcol, kcol, vcol, n_q, n_kv, slopes, max_dist, dist_scale,
                 tq, out_dtype, unit=4, batch=None, sinks=None, z=None, zcol=0, write_lse=False):
    in_specs, args, out_shape, out_specs = _banded_specs(
        q_arr, kv_arr, qcol=qcol, kcol=kcol, vcol=vcol, n_q=n_q, n_kv=n_kv, tq=tq, out_dtype=out_dtype,
        sinks=sinks, z=z, zcol=zcol, write_lse=write_lse, tile_of=lambda j, i: i)
    cfg = _banded_cfg(rows_per_seq=rows_per_seq, n_q=n_q, n_kv=n_kv, slopes=slopes, max_dist=max_dist,
                      dist_scale=dist_scale, tq=tq, unit=unit, batch=batch)
    has_sink, mul_z = sinks is not None, z is not None

    def body(in_refs, out_refs, scratch_refs, tile, proj_fn):
        in_refs = list(in_refs)
        sink_ref = in_refs[5] if has_sink else None
        z_ref = in_refs[5 + has_sink] if mul_z else None
        lse_ref = out_refs[1] if write_lse else None
        _banded_body(*in_refs[:5], sink_ref, z_ref, out_refs[0], lse_ref, tile=tile, between=proj_fn, **cfg)

    return in_specs, args, out_shape, out_specs, [], body


def _combine_kernel(o0, o1, o2, l0, l1, l2, z_ref, u_ref, o1_sc, o2_sc, l1_sc, l2_sc, *, tm, dils):
    n_chunk = o0.shape[1] // HEAD_DIM
    for src, dst, d in ((o1, o1_sc, dils[0]), (l1, l1_sc, dils[0]),
                        (o2, o2_sc, dils[1]), (l2, l2_sc, dils[1])):
        n = tm // d
        for p in range(d):
            for c in range(n_chunk):
                dst[c, pl.ds(p, n, stride=d), :] = src[p, :, c * HEAD_DIM:(c + 1) * HEAD_DIM]
    for c in range(n_chunk):
        cs = slice(c * HEAD_DIM, (c + 1) * HEAD_DIM)
        a0, a1, a2 = l0[:, cs], l1_sc[c], l2_sc[c]
        m = jnp.maximum(jnp.maximum(a0, a1), a2)
        e0, e1, e2 = jnp.exp2(a0 - m), jnp.exp2(a1 - m), jnp.exp2(a2 - m)
        inv = 1.0 / (e0 + e1 + e2)
        ya = (e0 * inv) * o0[:, cs] + (e1 * inv) * o1_sc[c] + (e2 * inv) * o2_sc[c]
        u_ref[:, cs] = (ya * z_ref[:, cs]).astype(u_ref.dtype)


def combine_extra(outs, lses, z, zcol, tm, dils):
    t, w = outs[0].shape
    blk = pl.BlockSpec((tm, w), lambda j, i: (i, 0))

    def pblk(d):
        return pl.BlockSpec((d, tm // d, w), lambda j, i: (0, i, 0))

    def p3(a, d):
        return a.reshape(d, t // d, w)

    d1, d2 = dils
    in_specs = [blk, pblk(d1), pblk(d2), blk, pblk(d1), pblk(d2),
                pl.BlockSpec((tm, w), lambda j, i: (i, zcol // w))]
    args = [outs[0], p3(outs[1], d1), p3(outs[2], d2), lses[0], p3(lses[1], d1), p3(lses[2], d2), z]

    def body(in_refs, out_refs, scratch_refs, tile, proj_fn):
        _combine_kernel(*in_refs, out_refs[0], *scratch_refs, tm=tm, dils=dils)
        proj_fn()

    return (in_specs, args, [jax.ShapeDtypeStruct((t, w), BF16)], [blk],
            [pltpu.VMEM((w // HEAD_DIM, tm, HEAD_DIM), F32)] * 4, body)


def _diff_kernel(slope_ref, q_ref, k_ref, v_ref, lam_ref, g_ref, z_ref, o_ref,
                 vt_sc, acc_sc, bias_sc, s_sc, *, tq, tk, seq, lam_init):
    slope = slope_ref[pl.program_id(1)]
    per_q = tq // tk
    to_log2 = SCALE * LOG2E

    for cidx in range(seq // tk):
        rows = slice(cidx * tk, (cidx + 1) * tk)
        vt_sc[:, rows] = v_ref[rows, :].astype(F32).T.astype(BF16)
    kk = lax.broadcasted_iota(jnp.int32, (tk, tq), 0)
    r = lax.broadcasted_iota(jnp.int32, (tk, tq), 1)
    bias = (-slope / SCALE) * (r - kk).astype(F32)
    bias_sc[0] = bias
    for dblk in range(per_q):
        bias_sc[1 + dblk] = jnp.where(r >= kk + dblk * tk, bias, NEG)

    lp = lam_ref[...]
    lam_full = (jnp.exp(jnp.sum(lp[0:1] * lp[1:2], axis=-1, keepdims=True))
                - jnp.exp(jnp.sum(lp[2:3] * lp[3:4], axis=-1, keepdims=True)) + lam_init)
    out_gain = g_ref[...] * (1.0 - lam_init)

    def query_tile(qi, carry):
        base = qi * per_q
        q0 = pl.multiple_of(qi * tq, tq)
        qs = [q_ref[pl.ds(q0, tq), c * HEAD_DIM:(c + 1) * HEAD_DIM] for c in range(2)]

        def shift_of(kj):
            return (slope * LOG2E) * ((base - kj) * tk).astype(F32)

        def max_chunk(kj0, n, ms):
            ms = list(ms)
            for u in range(n):
                kj = kj0 + u
                start = pl.multiple_of(kj * tk, tk)
                b = bias_sc[jnp.maximum(kj - base + 1, 0)]
                for comp in range(2):
                    k = k_ref[pl.ds(start, tk), comp * HEAD_DIM:(comp + 1) * HEAD_DIM]
                    a = (lax.dot_general(k, qs[comp], CONTRACT_LAST, preferred_element_type=F32) + b) * to_log2
                    s_sc[comp, pl.ds(start, tk), :] = a
                    ms[comp] = jnp.maximum(ms[comp], jnp.max(a, axis=0, keepdims=True) - shift_of(kj))
            return tuple(ms)

        def over_blocks(chunk_fn, carry):
            done = 0
            for n in DIFF_UNROLL:
                if n % per_q:
                    continue
                n_it = (base + per_q - done) // n
                carry = lax.fori_loop(
                    0, n_it, lambda t, c, n=n, done=done: chunk_fn(done + t * n, n, c), carry)
                done = done + n_it * n
            return carry

        neg = jnp.full((1, tq), NEG, F32)
        ms = over_blocks(max_chunk, (neg, neg))

        def exp_chunk(kj0, n, ls):
            ls = list(ls)
            start0 = pl.multiple_of(kj0 * tk, tk)
            vt = vt_sc[:, pl.ds(start0, n * tk)]
            for comp in range(2):
                es = []
                for u in range(n):
                    start = pl.multiple_of((kj0 + u) * tk, tk)
                    e = jnp.exp2(s_sc[comp, pl.ds(start, tk), :] - (ms[comp] + shift_of(kj0 + u)))
                    ls[comp] = ls[comp] + jnp.sum(e, axis=0, keepdims=True)
                    es.append(e.astype(BF16))
                e_cat = es[0] if n == 1 else jnp.concatenate(es, axis=0)
                acc_sc[comp] += jnp.dot(vt, e_cat, preferred_element_type=F32)
            return tuple(ls)

        acc_sc[...] = jnp.zeros(acc_sc.shape, F32)
        zero = jnp.zeros((1, tq), F32)
        l1, l2 = over_blocks(exp_chunk, (zero, zero))

        y_t = acc_sc[0] * (1.0 / l1) - lam_full * (acc_sc[1] * (1.0 / l2))
        ms2 = jnp.mean(y_t * y_t, axis=0, keepdims=True)
        y = (y_t * lax.rsqrt(ms2 + EPS)).T * out_gain
        o_ref[pl.ds(q0, tq), :] = (y * z_ref[pl.ds(q0, tq), :]).astype(o_ref.dtype)
        return carry

    lax.fori_loop(0, seq // tq, query_tile, 0)


def diff_attention(cq, ck, cv, slopes, lam_l, subln_g, z, *, seq, zcol, tq, tk, lam_init):
    t = cq.shape[0]
    n_batch = t // seq
    hw = 2 * HEAD_DIM
    kern = functools.partial(_diff_kernel, tq=tq, tk=tk, seq=seq, lam_init=lam_init)
    whole = pl.BlockSpec((seq, hw), lambda b, h: (b, h))
    return pl.pallas_call(
        kern,
        out_shape=jax.ShapeDtypeStruct((t, C_OUT), BF16),
        grid=(n_batch, C_HEADS),
        in_specs=[
            pl.BlockSpec(memory_space=pltpu.SMEM),
            whole, whole, whole,
            pl.BlockSpec((4, HEAD_DIM), lambda b, h: (0, 0)),
            pl.BlockSpec((1, hw), lambda b, h: (0, 0)),
            pl.BlockSpec((seq, hw), lambda b, h: (b, zcol // hw + h)),
        ],
        out_specs=whole,
        scratch_shapes=[pltpu.VMEM((hw, seq), BF16), pltpu.VMEM((2, hw, tq), F32),
                        pltpu.VMEM((1 + tq // tk, tk, tq), F32), pltpu.VMEM((2, seq, tq), F32)],
        compiler_params=_params(2),
        name="diff_attention",
    )(slopes, cq, ck, cv, lam_l, subln_g.reshape(1, hw), z)


def _mem_kernel(q_ref, k_ref, v_ref, z_ref, o_ref):
    for h in range(M_HEADS):
        hs = slice(h * HEAD_DIM, (h + 1) * HEAD_DIM)
        s_t = lax.dot_general(k_ref[:, hs], q_ref[:, hs], CONTRACT_LAST,
                              preferred_element_type=F32) * (SCALE * LOG2E)
        m = jnp.max(s_t, axis=0, keepdims=True)
        e = jnp.exp2(s_t - m)
        p_t = (e * (1.0 / jnp.sum(e, axis=0, keepdims=True))).astype(BF16)
        o = lax.dot_general(p_t, v_ref[:, hs], CONTRACT_FIRST, preferred_element_type=F32)
        o_ref[:, hs] = (o * z_ref[:, hs]).astype(o_ref.dtype)


def mem_extra(cp, mkv, z, *, seq, qcol, zcol, tq):
    t = cp.shape[0]
    nq = seq // tq
    in_specs = [
        pl.BlockSpec((tq, M_OUT), lambda j, i: (i, qcol // M_OUT)),
        pl.BlockSpec((N_MEM, M_OUT), lambda j, i: (i // nq, 0)),
        pl.BlockSpec((N_MEM, M_OUT), lambda j, i: (i // nq, 1)),
        pl.BlockSpec((tq, M_OUT), lambda j, i: (i, zcol // M_OUT)),
    ]
    out_shape = [jax.ShapeDtypeStruct((t, M_OUT), BF16)]
    out_specs = [pl.BlockSpec((tq, M_OUT), lambda j, i: (i, 0))]

    def body(in_refs, out_refs, scratch_refs, tile, proj_fn):
        proj_fn()
        _mem_kernel(*in_refs, out_refs[0])

    return in_specs, [cp, mkv, mkv, z], out_shape, out_specs, [], body


_BRANCH_ROWS = ((0, A_OUT), (A_OUT, A_OUT + B_OUT), (A_OUT + B_OUT, A_OUT + B_OUT + C_OUT),
                (A_OUT + B_OUT + C_OUT, A_OUT + B_OUT + C_OUT + M_OUT))
_WB_CHUNK = 512


def _merge_kernel(*refs):
    h_ref, us = refs[0], refs[1:5]
    wgs, bgs = refs[5:9], refs[9:13]
    n_chunk = (A_OUT + B_OUT + C_OUT + M_OUT) // _WB_CHUNK
    wbs = refs[13:13 + n_chunk]
    o_ref, wg_sc, wb_sc = refs[13 + n_chunk:]

    @pl.when(pl.program_id(1) == 0)
    def _():
        for b in range(N_BRANCH):
            wg_sc[b] = wgs[b][...].astype(BF16)
        for c in range(n_chunk):
            wb_sc[c * _WB_CHUNK:(c + 1) * _WB_CHUNK, :] = wbs[c][...].astype(BF16)

    h = h_ref[...]
    acc = None
    for b in range(N_BRANCH):
        lo, hi = _BRANCH_ROWS[b]
        gate = _sigmoid(jnp.dot(h, wg_sc[b], preferred_element_type=F32) + bgs[b][...])
        term = gate * jnp.dot(us[b][...], wb_sc[lo:hi, :], preferred_element_type=F32)
        acc = term if acc is None else acc + term
    o_ref[...] = acc.astype(o_ref.dtype)


def gated_merge(h, us, w_in, b_gate3, w_branch, layer, tm, tn):
    t, d = h.shape
    nj = d // tn
    gate0 = O_GATE // tn
    n_chunk = w_branch.shape[1] // _WB_CHUNK
    in_specs = [pl.BlockSpec((tm, d), lambda j, i: (i, 0))]
    in_specs += [pl.BlockSpec((tm, u.shape[1]), lambda j, i: (i, 0)) for u in us]
    in_specs += [pl.BlockSpec((None, d, tn), functools.partial(
        lambda j, i, b: (layer, 0, gate0 + b * nj + j), b=b)) for b in range(N_BRANCH)]
    in_specs += [pl.BlockSpec((None, 1, tn), functools.partial(
        lambda j, i, b: (layer, 0, b * nj + j), b=b)) for b in range(N_BRANCH)]
    in_specs += [pl.BlockSpec((None, _WB_CHUNK, tn), functools.partial(
        lambda j, i, c: (layer, c, j), c=c)) for c in range(n_chunk)]
    return pl.pallas_call(
        _merge_kernel,
        out_shape=jax.ShapeDtypeStruct((t, d), BF16),
        grid=(nj, t // tm),
        in_specs=in_specs,
        out_specs=pl.BlockSpec((tm, tn), lambda j, i: (i, j)),
        scratch_shapes=[pltpu.VMEM((N_BRANCH, d, tn), BF16),
                        pltpu.VMEM((w_branch.shape[1], tn), BF16)],
        compiler_params=_params(2),
        name="gated_merge",
    )(h, *us, *([w_in] * N_BRANCH), *([b_gate3] * N_BRANCH), *([w_branch] * n_chunk))


def _out_kernel(x_ref, m_ref, w_ref, o_ref, w_sc):
    @pl.when(pl.program_id(1) == 0)
    def _():
        w_sc[...] = w_ref[...].astype(BF16)

    o_ref[...] = x_ref[...] + jnp.dot(m_ref[...], w_sc[...], preferred_element_type=F32)


def out_proj(x, merged, w_out, layer, tm, tn):
    t, d = x.shape
    return pl.pallas_call(
        _out_kernel,
        out_shape=jax.ShapeDtypeStruct((t, d), F32),
        grid=(d // tn, t // tm),
        in_specs=[pl.BlockSpec((tm, tn), lambda j, i: (i, j)),
                  pl.BlockSpec((tm, d), lambda j, i: (i, 0)),
                  pl.BlockSpec((None, d, tn), lambda j, i: (layer, 0, j))],
        out_specs=pl.BlockSpec((tm, tn), lambda j, i: (i, j)),
        scratch_shapes=[pltpu.VMEM((d, tn), BF16)],
        compiler_params=_params(2),
        name="out_proj",
    )(x, merged, w_out)


def kernel(x, mem, norm_g, w_in, b_gate, qk_gain, sinks, lam, subln_g, mem_norm_g,
           w_mem_kv, w_branch, w_out):
    bsz, s_len, d = x.shape
    depth = w_in.shape[0]
    t = bsz * s_len
    xf = x.reshape(t, d)
    memf = mem.reshape(bsz * N_MEM, d)
    b_gate3 = b_gate.reshape(depth, 1, N_BRANCH * d)

    slopes_a = _alibi_slopes(A_HEADS_PER_GROUP)
    slopes_b = _alibi_slopes(B_Q_HEADS)
    slopes_c = jnp.asarray(_alibi_slopes(C_HEADS), F32)
    ones = lambda n: jnp.ones((n,), F32)
    dils = tuple(dil for _, dil in A_PATTERNS)

    pc = PROJ_PIECE
    z_zb, z_zc, z_za, z_zm = 0, B_OUT, B_OUT + C_OUT, B_OUT + C_OUT + A_OUT
    g_bm = B_OUT
    g_cq = g_bm + 2 * B_KV_HEADS * HEAD_DIM + M_OUT
    g_ck, g_cv = g_cq + C_OUT, g_cq + 2 * C_OUT

    for l in range(depth):
        gq = qk_gain[l]
        gain_a = jnp.concatenate([jnp.tile(gq[QK_A_Q], 4), jnp.tile(gq[QK_A_K], 4), ones(A_OUT)]).reshape(1, -1)
        gain_bc = jnp.concatenate([
            jnp.tile(gq[QK_B_Q], B_Q_HEADS), jnp.tile(gq[QK_B_K], B_KV_HEADS), ones(B_KV_HEADS * HEAD_DIM),
            jnp.tile(gq[QK_M_Q], M_HEADS), jnp.tile(gq[QK_C_Q], 2 * C_HEADS),
            jnp.tile(gq[QK_C_K], 2 * C_HEADS), ones(C_OUT)]).reshape(1, -1)
        gain_z = ones(O_GATE - O_ZA).reshape(1, -1)
        gain_m = jnp.concatenate([jnp.tile(gq[QK_M_K], M_HEADS), ones(M_OUT)]).reshape(1, -1)

        bm, h, *hps = norm_proj(xf, norm_g[l], dils[1:], w_in, l, [O_BK // pc, O_MQ // pc],
                                gain_bc[:, g_bm:g_cq], "NNPPNNNN", 512, BF16, "norm_proj_bm")
        h_perm = [h] + [hp.reshape(t, d) for hp in hps]

        def a_cols(g):
            return [O_AQ // pc + g, O_AK // pc + g, O_AV // pc + g]

        def a_attn(g):
            win, dil = A_PATTERNS[g]
            return dict(rows_per_seq=s_len // dil, qcol=0, kcol=A_OUT, vcol=2 * A_OUT,
                        n_q=A_HEADS_PER_GROUP, n_kv=A_HEADS_PER_GROUP, slopes=slopes_a,
                        max_dist=win // dil, dist_scale=dil, tq=1024, out_dtype=F32, unit=1, write_lse=True)

        a_mode = "NNNNNNNNPPPP"
        n8 = "NNNNNNNN"
        sz = proj(h, w_in, l, [(O_ZB // pc, O_ZC // pc, O_ZA // pc), (O_ZB // pc + 1, O_ZC // pc + 1, O_ZM // pc)],
                  gain_z, ("SSSSSSSS",) * 3, 1024, F32, "proj_z")
        ap0 = proj(h_perm[0], w_in, l, [(c,) for c in a_cols(0)], gain_a, (a_mode,), 1024, BF16, "proj_a0")
        ap1, o0, l0 = proj_fused(h_perm[1], w_in, l, a_cols(1), gain_a, a_mode, 1024, BF16,
                                 banded_extra(ap0, ap0, **a_attn(0)), "proj_a1_attn_a0")
        ap2, o1, l1 = proj_fused(h_perm[2], w_in, l, a_cols(2), gain_a, a_mode, 1024, BF16,
                                 banded_extra(ap1, ap1, **a_attn(1)), "proj_a2_attn_a1")
        bqa, o2, l2 = proj_fused(h, w_in, l, [O_BQ // pc, O_BQ // pc + 1], gain_bc[:, :g_bm], n8, 1024, BF16,
                                 banded_extra(ap2, ap2, **a_attn(2)), "proj_bq_attn_a2")
        mn = rmsnorm_bf16(memf, mem_norm_g[l], 256)[0]
        mkv = proj(mn, w_mem_kv, l, [(0,), (1,)], gain_m, ("NNNNPPPP",), bsz * N_MEM, BF16, "proj_mem")
        cq, u_b = proj_fused(
            h, w_in, l, [O_CQ // pc, O_CQ // pc + 1], gain_bc[:, g_cq:g_ck], n8, 1024, BF16,
            banded_extra(bqa, bm, rows_per_seq=s_len, qcol=0, kcol=0, vcol=B_KV_HEADS * HEAD_DIM,
                         n_q=B_Q_HEADS, n_kv=B_KV_HEADS, slopes=slopes_b, max_dist=B_WINDOW - 1, dist_scale=1,
                         tq=1024, out_dtype=BF16, batch=1, sinks=sinks[l], z=sz, zcol=z_zb), "proj_cq_attn_b")
        ck, u_m = proj_fused(
            h, w_in, l, [O_CK // pc, O_CK // pc + 1], gain_bc[:, g_ck:g_cv], n8, 1024, BF16,
            mem_extra(bm, mkv, sz, seq=s_len, qcol=2 * B_KV_HEADS * HEAD_DIM, zcol=z_zm, tq=1024),
            "proj_ck_attn_m")
        cv, u_a = proj_fused(h, w_in, l, [O_CV // pc, O_CV // pc + 1], gain_bc[:, g_cv:], "PPPPPPPP", 512, BF16,
                             combine_extra([o0, o1, o2], [l0, l1, l2], sz, z_za, 512, dils[1:]),
                             "proj_cv_combine_a")

        lam_init = 0.8 - 0.6 * math.exp(-0.3 * l)
        u_c = diff_attention(cq, ck, cv, slopes_c, lam[l], subln_g[l], sz, seq=s_len,
                             zcol=z_zc, tq=512, tk=256, lam_init=lam_init)

        merged = gated_merge(h, [u_a, u_b, u_c, u_m], w_in, b_gate3, w_branch, l, 1024, 256)
        xf = out_proj(xf, merged, w_out, l, 1024, 1024)

    return xf.reshape(bsz, s_len, d)
```

```python
import functools
import math

import jax
import jax.numpy as jnp
from jax import lax
from jax.experimental import pallas as pl
from jax.experimental.pallas import tpu as pltpu

D_MODEL = 2048
HEAD_DIM = 128
BLOCK = 128
EPS = 1e-6
A_PATTERNS = ((128, 1), (512, 4), (2048, 16))
A_GROUPS = 3
A_HEADS_PER_GROUP = 4
A_HEADS = A_GROUPS * A_HEADS_PER_GROUP
A_OUT = A_HEADS_PER_GROUP * HEAD_DIM
B_Q_HEADS = 8
B_KV_HEADS = 2
B_WINDOW = 128
B_OUT = B_Q_HEADS * HEAD_DIM
C_HEADS = 4
C_OUT = C_HEADS * 2 * HEAD_DIM
N_MEM = 256
M_HEADS = 4
M_OUT = M_HEADS * HEAD_DIM
N_BRANCH = 4
QK_A_Q, QK_A_K, QK_B_Q, QK_B_K, QK_C_Q, QK_C_K, QK_M_Q, QK_M_K = range(8)

_IN_SIZES = (A_HEADS * HEAD_DIM, A_HEADS * HEAD_DIM, A_HEADS * HEAD_DIM,
             B_Q_HEADS * HEAD_DIM, B_KV_HEADS * HEAD_DIM, B_KV_HEADS * HEAD_DIM,
             C_HEADS * 2 * HEAD_DIM, C_HEADS * 2 * HEAD_DIM, C_OUT,
             M_HEADS * HEAD_DIM, A_OUT, B_OUT, C_OUT, M_OUT, N_BRANCH * D_MODEL)
_OFF = [0]
for _s in _IN_SIZES:
    _OFF.append(_OFF[-1] + _s)
(O_AQ, O_AK, O_AV, O_BQ, O_BK, O_BV, O_CQ, O_CK, O_CV, O_MQ,
 O_ZA, O_ZB, O_ZC, O_ZM, O_GATE, O_END) = _OFF

NEG = -1e30
SCALE = HEAD_DIM ** -0.5
VMEM_LIMIT = 58 * 1024 * 1024
PROJ_PIECE = 512
ROW_TILE = 1024
HALF_TILE = 512
MERGE_TN = 256
OUT_TN = 1024
DIFF_TQ, DIFF_TK = 512, 256
DIFF_UNROLL = (8, 6, 4, 2, 1)
LOG2E = math.log2(math.e)
BF16 = jnp.bfloat16
F32 = jnp.float32
CONTRACT_LAST = (((1,), (1,)), ((), ()))
CONTRACT_FIRST = (((0,), (0,)), ((), ()))


def _params(n_axes):
    return pltpu.CompilerParams(dimension_semantics=("arbitrary",) * n_axes,
                                vmem_limit_bytes=VMEM_LIMIT)


def _sigmoid(x):
    return 0.5 * jnp.tanh(0.5 * x) + 0.5


def _alibi_slopes(n):
    return [2.0 ** (-8.0 * i / n) for i in range(1, n + 1)]


def _norm_proj_kernel(*refs, dils, tm, sub, n_piece, tile_mode):
    x_ref, g_ref = refs[0], refs[1]
    w_refs, gain_ref = refs[2:2 + n_piece], refs[2 + n_piece]
    o_ref, h_ref = refs[3 + n_piece], refs[4 + n_piece]
    perm_refs = refs[5 + n_piece:5 + n_piece + len(dils)]
    w_sc = refs[5 + n_piece + len(dils)]
    x = x_ref[...]
    ms = jnp.mean(x * x, axis=-1, keepdims=True)
    h_ref[...] = (x * lax.rsqrt(ms + EPS) * g_ref[...]).astype(BF16)
    _proj_body(h_ref, w_refs, gain_ref, o_ref, w_sc, (tile_mode,))
    row = lax.broadcasted_iota(jnp.int32, (sub, sub), 0)
    col = lax.broadcasted_iota(jnp.int32, (sub, sub), 1)
    for d, p_ref in zip(dils, perm_refs):
        n = sub // d
        perm = jnp.where(col == (row % n) * d + row // n, 1.0, 0.0).astype(BF16)
        for s0 in range(tm // sub):
            hp = jnp.dot(perm, h_ref[s0 * sub:(s0 + 1) * sub, :], preferred_element_type=F32).astype(BF16)
            for p in range(d):
                p_ref[p, s0 * n:(s0 + 1) * n, :] = hp[p * n:(p + 1) * n, :]


def norm_proj(x, g, dils, w3, layer, piece_cols, gain, tile_mode, tm, out_dtype, name):
    t, d = x.shape
    n_piece = len(piece_cols)
    tn = n_piece * PROJ_PIECE
    sub = 256
    assert len(tile_mode) == tn // HEAD_DIM and gain.shape == (1, tn) and tm % sub == 0
    out_shape = [jax.ShapeDtypeStruct((t, tn), out_dtype), jax.ShapeDtypeStruct((t, d), BF16)]
    out_specs = [pl.BlockSpec((tm, tn), lambda j, i: (i, 0)), pl.BlockSpec((tm, d), lambda j, i: (i, 0))]
    for dil in dils:
        out_shape.append(jax.ShapeDtypeStruct((dil, t // dil, d), BF16))
        out_specs.append(pl.BlockSpec((dil, tm // dil, d), lambda j, i: (0, i, 0)))
    w_specs = [pl.BlockSpec((None, d, PROJ_PIECE), functools.partial(lambda j, i, c: (layer, 0, c), c=c))
               for c in piece_cols]
    return pl.pallas_call(
        functools.partial(_norm_proj_kernel, dils=tuple(dils), tm=tm, sub=sub, n_piece=n_piece,
                          tile_mode=tile_mode),
        out_shape=out_shape,
        grid=(1, t // tm),
        in_specs=[pl.BlockSpec((tm, d), lambda j, i: (i, 0)), pl.BlockSpec((1, d), lambda j, i: (0, 0))]
                 + w_specs + [pl.BlockSpec((1, tn), lambda j, i: (0, 0))],
        out_specs=out_specs,
        scratch_shapes=[pltpu.VMEM((d, tn), BF16)],
        compiler_params=_params(2),
        name=name,
    )(x, g.reshape(1, d), *([w3] * n_piece), gain)


def _proj_kernel(*refs, tile_modes, n_piece):
    h_ref, w_refs = refs[0], refs[1:1 + n_piece]
    gain_ref, o_ref, w_sc = refs[1 + n_piece:]
    _proj_body(h_ref, w_refs, gain_ref, o_ref, w_sc, tile_modes)


def _proj_body(h_ref, w_refs, gain_ref, o_ref, w_sc, tile_modes):
    n_piece = len(w_refs)
    j = pl.program_id(0)

    @pl.when(pl.program_id(1) == 0)
    def _():
        for p in range(n_piece):
            w_sc[:, p * PROJ_PIECE:(p + 1) * PROJ_PIECE] = w_refs[p][...].astype(BF16)

    acc = jnp.dot(h_ref[...], w_sc[...], preferred_element_type=F32)
    for c in range(len(tile_modes[0])):
        sl = slice(c * HEAD_DIM, (c + 1) * HEAD_DIM)
        modes = [m[c] for m in tile_modes]
        y = acc[:, sl]
        if 'S' in modes:
            assert set(modes) == {'S'}
            y = y * _sigmoid(y)
        elif 'N' in modes:
            ms = jnp.mean(y * y, axis=-1, keepdims=True)
            normed = y * lax.rsqrt(ms + EPS) * gain_ref[:, sl]
            if set(modes) == {'N'}:
                y = normed
            else:
                is_norm = functools.reduce(jnp.logical_or, [j == jj for jj, m in enumerate(modes) if m == 'N'])
                y = jnp.where(is_norm, normed, y)
        o_ref[:, sl] = y.astype(o_ref.dtype)


def _select(j, values):
    out = values[-1]
    for jj in range(len(values) - 2, -1, -1):
        out = jnp.where(j == jj, values[jj], out)
    return out


def proj(h, w3, layer, piece_cols, gain, tile_modes, tm, out_dtype, name):
    t, k = h.shape
    n_piece = len(piece_cols)
    tn = n_piece * PROJ_PIECE
    n_tiles = len(tile_modes)
    assert t % tm == 0 and all(len(m) == tn // HEAD_DIM for m in tile_modes)
    assert all(len(cols) == n_tiles for cols in piece_cols) and gain.shape == (1, tn * n_tiles)
    w_specs = [pl.BlockSpec((None, k, PROJ_PIECE),
                            functools.partial(lambda j, i, cols: (layer, 0, _select(j, cols)), cols=tuple(cols)))
               for cols in piece_cols]
    return pl.pallas_call(
        functools.partial(_proj_kernel, tile_modes=tuple(tile_modes), n_piece=n_piece),
        out_shape=jax.ShapeDtypeStruct((t, tn * n_tiles), out_dtype),
        grid=(n_tiles, t // tm),
        in_specs=[pl.BlockSpec((tm, k), lambda j, i: (i, 0))] + w_specs
                 + [pl.BlockSpec((1, tn), lambda j, i: (0, j))],
        out_specs=pl.BlockSpec((tm, tn), lambda j, i: (i, j)),
        scratch_shapes=[pltpu.VMEM((k, tn), BF16)],
        compiler_params=_params(2),
        name=name,
    )(h, *([w3] * n_piece), gain)


def _banded_body(q_ref, kc_ref, kp_ref, vc_ref, vp_ref, sink_ref, z_ref, o_ref, lse_ref, *, tile,
                 n_q, n_kv, slopes, max_dist, dist_scale, tq, rows_per_seq, unit, batch, between=None):
    has_sink, mul_z, write_lse = sink_ref is not None, z_ref is not None, lse_ref is not None
    shared_kv = n_kv < n_q and unit > 1
    group = n_q // n_kv
    assert n_q % unit == 0 and (not shared_kv or group == unit)
    kk = lax.broadcasted_iota(jnp.int32, (2 * BLOCK, BLOCK), 0)
    r = lax.broadcasted_iota(jnp.int32, (2 * BLOCK, BLOCK), 1)
    dist = r + BLOCK - kk
    distf = (dist * dist_scale).astype(F32)
    valid = (dist >= 0) & (dist <= max_dist)
    valid_start = valid & (kk >= BLOCK)
    if rows_per_seq > tq:
        has_prev = jnp.minimum(tile % (rows_per_seq // tq), 1) * BLOCK
        valid0 = valid & (kk + has_prev >= BLOCK)
    else:
        valid0 = valid_start

    def window(cur_ref, prev_ref, kv, j):
        ks = slice(kv * HEAD_DIM, (kv + 1) * HEAD_DIM)
        prev = prev_ref[:, ks] if j == 0 else cur_ref[(j - 1) * BLOCK:j * BLOCK, ks]
        return jnp.concatenate([prev, cur_ref[j * BLOCK:(j + 1) * BLOCK, ks]], axis=0)

    units = [(list(range(u * unit, (u + 1) * unit)), j)
             for u in range(n_q // unit) for j in range(tq // BLOCK)]

    for b0 in range(0, len(units), batch or len(units)):
        batch_units = units[b0:b0 + (batch or len(units))]
        scores = []
        for heads, j in batch_units:
            rows = slice(j * BLOCK, (j + 1) * BLOCK)
            qs = [q_ref[rows, h * HEAD_DIM:(h + 1) * HEAD_DIM] for h in heads]
            if shared_kv:
                s_t = lax.dot_general(window(kc_ref, kp_ref, heads[0] // group, j), jnp.concatenate(qs, axis=0),
                                      CONTRACT_LAST, preferred_element_type=F32)
            else:
                s_t = jnp.concatenate(
                    [lax.dot_general(window(kc_ref, kp_ref, h // group, j), q, CONTRACT_LAST,
                                     preferred_element_type=F32) for h, q in zip(heads, qs)], axis=1)
            scores.append(s_t)
        if between is not None and b0 == 0:
            between()

        probs, lses = [], []
        for (heads, j), s_t in zip(batch_units, scores):
            v_mask = valid0 if j == 0 else (valid_start if (j * BLOCK) % rows_per_seq == 0 else valid)
            bias = jnp.concatenate([jnp.where(v_mask, (-slopes[h] * LOG2E) * distf, NEG) for h in heads], axis=1)
            s_t = s_t * (SCALE * LOG2E) + bias
            m = jnp.max(s_t, axis=0, keepdims=True)
            if has_sink:
                sink = jnp.concatenate([jnp.full((1, BLOCK), sink_ref[h] * LOG2E, F32) for h in heads], axis=1)
                m = jnp.maximum(m, sink)
            e = jnp.exp2(s_t - m)
            denom = jnp.sum(e, axis=0, keepdims=True)
            if has_sink:
                denom = denom + jnp.exp2(sink - m)
            probs.append((e * (1.0 / denom)).astype(BF16))
            lses.append(m + jnp.log2(denom) if write_lse else None)

        for (heads, j), p_t, lse in zip(batch_units, probs, lses):
            rows = slice(j * BLOCK, (j + 1) * BLOCK)
            if shared_kv:
                o_all = lax.dot_general(p_t, window(vc_ref, vp_ref, heads[0] // group, j), CONTRACT_FIRST,
                                        preferred_element_type=F32)
            for g, h in enumerate(heads):
                hs = slice(h * HEAD_DIM, (h + 1) * HEAD_DIM)
                gs = slice(g * BLOCK, (g + 1) * BLOCK)
                if shared_kv:
                    o = o_all[gs, :]
                else:
                    o = lax.dot_general(p_t[:, gs], window(vc_ref, vp_ref, h // group, j), CONTRACT_FIRST,
                                        preferred_element_type=F32)
                if mul_z:
                    o = o * z_ref[rows, hs]
                o_ref[rows, hs] = o.astype(o_ref.dtype)
                if write_lse:
                    lse_ref[rows, hs] = jnp.broadcast_to(lse[:, gs], (BLOCK, BLOCK)).T


def _banded_specs(q_arr, kv_arr, *, qcol, kcol, vcol, n_q, n_kv, tq, out_dtype, sinks, z, zcol, write_lse,
                  tile_of):
    rows_total = q_arr.shape[0]
    qw, kw = n_q * HEAD_DIM, n_kv * HEAD_DIM
    assert qcol % qw == 0 and kcol % kw == 0 and vcol % kw == 0
    sub = tq // BLOCK

    def cur(col, w):
        return lambda *g: (tile_of(*g), col // w)

    def prev(col, w):
        return lambda *g: (jnp.maximum(tile_of(*g) * sub - 1, 0), col // w)

    in_specs = [
        pl.BlockSpec((tq, qw), cur(qcol, qw)),
        pl.BlockSpec((tq, kw), cur(kcol, kw)),
        pl.BlockSpec((BLOCK, kw), prev(kcol, kw)),
        pl.BlockSpec((tq, kw), cur(vcol, kw)),
        pl.BlockSpec((BLOCK, kw), prev(vcol, kw)),
    ]
    args = [q_arr, kv_arr, kv_arr, kv_arr, kv_arr]
    if sinks is not None:
        in_specs.append(pl.BlockSpec(memory_space=pltpu.SMEM))
        args.append(sinks)
    if z is not None:
        assert zcol % qw == 0
        in_specs.append(pl.BlockSpec((tq, qw), cur(zcol, qw)))
        args.append(z)
    out_block = pl.BlockSpec((tq, qw), cur(0, qw))
    out_shape = [jax.ShapeDtypeStruct((rows_total, qw), out_dtype)]
    out_specs = [out_block]
    if write_lse:
        out_shape.append(jax.ShapeDtypeStruct((rows_total, qw), F32))
        out_specs.append(out_block)
    return in_specs, args, out_shape, out_specs


def _banded_cfg(*, rows_per_seq, n_q, n_kv, slopes, max_dist, dist_scale, tq, unit, batch):
    assert tq % rows_per_seq == 0 or rows_per_seq % tq == 0
    return dict(n_q=n_q, n_kv=n_kv, slopes=tuple(slopes), max_dist=max_dist, dist_scale=dist_scale,
                tq=tq, rows_per_seq=rows_per_seq, unit=unit, batch=batch)


def _proj_fused_kernel(*refs, tile_mode, n_piece, n_in, n_out, body):
    h_ref, w_refs, gain_ref = refs[0], refs[1:1 + n_piece], refs[1 + n_piece]
    extra_in = refs[2 + n_piece:2 + n_piece + n_in]
    p_out = refs[2 + n_piece + n_in]
    extra_out = refs[3 + n_piece + n_in:3 + n_piece + n_in + n_out]
    w_sc = refs[3 + n_piece + n_in + n_out]
    extra_scratch = refs[4 + n_piece + n_in + n_out:]
    body(extra_in, extra_out, extra_scratch, pl.program_id(1),
         lambda: _proj_body(h_ref, w_refs, gain_ref, p_out, w_sc, (tile_mode,)))


def proj_fused(h, w3, layer, piece_cols, gain, tile_mode, tm, out_dtype, extra, name):
    t, k = h.shape
    n_piece = len(piece_cols)
    tn = n_piece * PROJ_PIECE
    assert len(tile_mode) == tn // HEAD_DIM and gain.shape == (1, tn)
    e_specs, e_args, e_shape, e_out_specs, e_scratch, body = extra
    w_specs = [pl.BlockSpec((None, k, PROJ_PIECE), functools.partial(lambda j, i, c: (layer, 0, c), c=c))
               for c in piece_cols]
    return pl.pallas_call(
        functools.partial(_proj_fused_kernel, tile_mode=tile_mode, n_piece=n_piece,
                          n_in=len(e_specs), n_out=len(e_shape), body=body),
        out_shape=[jax.ShapeDtypeStruct((t, tn), out_dtype)] + list(e_shape),
        grid=(1, t // tm),
        in_specs=[pl.BlockSpec((tm, k), lambda j, i: (i, 0))] + w_specs
                 + [pl.BlockSpec((1, tn), lambda j, i: (0, 0))] + list(e_specs),
        out_specs=[pl.BlockSpec((tm, tn), lambda j, i: (i, 0))] + list(e_out_specs),
        scratch_shapes=[pltpu.VMEM((k, tn), BF16)] + list(e_scratch),
        compiler_params=_params(2),
        name=name,
    )(h, *([w3] * n_piece), gain, *e_args)


def banded_extra(q_arr, kv_arr, *, rows_per_seq, qcol, kcol, vcol, n_q, n_kv, slopes, max_dist, dist_scale,
                 tq, out_dtype, unit=4, batch=None, sinks=None, z=None, zcol=0, write_lse=False):
    in_specs, args, out_shape, out_specs = _banded_specs(
        q_arr, kv_arr, qcol=qcol, kcol=kcol, vcol=vcol, n_q=n_q, n_kv=n_kv, tq=tq, out_dtype=out_dtype,
        sinks=sinks, z=z, zcol=zcol, write_lse=write_lse, tile_of=lambda j, i: i)
    cfg = _banded_cfg(rows_per_seq=rows_per_seq, n_q=n_q, n_kv=n_kv, slopes=slopes, max_dist=max_dist,
                      dist_scale=dist_scale, tq=tq, unit=unit, batch=batch)
    has_sink, mul_z = sinks is not None, z is not None

    def body(in_refs, out_refs, scratch_refs, tile, proj_fn):
        in_refs = list(in_refs)
        sink_ref = in_refs[5] if has_sink else None
        z_ref = in_refs[5 + has_sink] if mul_z else None
        lse_ref = out_refs[1] if write_lse else None
        _banded_body(*in_refs[:5], sink_ref, z_ref, out_refs[0], lse_ref, tile=tile, between=proj_fn, **cfg)

    return in_specs, args, out_shape, out_specs, [], body


def _combine_kernel(o0, o1, o2, l0, l1, l2, z_ref, u_ref, o1_sc, o2_sc, l1_sc, l2_sc, *, tm, dils):
    n_chunk = o0.shape[1] // HEAD_DIM
    for src, dst, d in ((o1, o1_sc, dils[0]), (l1, l1_sc, dils[0]),
                        (o2, o2_sc, dils[1]), (l2, l2_sc, dils[1])):
        n = tm // d
        for p in range(d):
            for c in range(n_chunk):
                dst[c, pl.ds(p, n, stride=d), :] = src[p, :, c * HEAD_DIM:(c + 1) * HEAD_DIM]
    for c in range(n_chunk):
        cs = slice(c * HEAD_DIM, (c + 1) * HEAD_DIM)
        a0, a1, a2 = l0[:, cs], l1_sc[c], l2_sc[c]
        m = jnp.maximum(jnp.maximum(a0, a1), a2)
        e0, e1, e2 = jnp.exp2(a0 - m), jnp.exp2(a1 - m), jnp.exp2(a2 - m)
        inv = 1.0 / (e0 + e1 + e2)
        ya = (e0 * inv) * o0[:, cs] + (e1 * inv) * o1_sc[c] + (e2 * inv) * o2_sc[c]
        u_ref[:, cs] = (ya * z_ref[:, cs]).astype(u_ref.dtype)


def combine_extra(outs, lses, z, zcol, tm, dils):
    t, w = outs[0].shape
    blk = pl.BlockSpec((tm, w), lambda j, i: (i, 0))

    def pblk(d):
        return pl.BlockSpec((d, tm // d, w), lambda j, i: (0, i, 0))

    def p3(a, d):
        return a.reshape(d, t // d, w)

    d1, d2 = dils
    in_specs = [blk, pblk(d1), pblk(d2), blk, pblk(d1), pblk(d2),
                pl.BlockSpec((tm, w), lambda j, i: (i, zcol // w))]
    args = [outs[0], p3(outs[1], d1), p3(outs[2], d2), lses[0], p3(lses[1], d1), p3(lses[2], d2), z]

    def body(in_refs, out_refs, scratch_refs, tile, proj_fn):
        _combine_kernel(*in_refs, out_refs[0], *scratch_refs, tm=tm, dils=dils)
        proj_fn()

    return (in_specs, args, [jax.ShapeDtypeStruct((t, w), BF16)], [blk],
            [pltpu.VMEM((w // HEAD_DIM, tm, HEAD_DIM), F32)] * 4, body)


def _diff_kernel(slope_ref, q_ref, k_ref, v_ref, lam_ref, g_ref, z_ref, o_ref,
                 vt_sc, acc_sc, bias_sc, s_sc, *, tq, tk, seq, lam_init):
    slope = slope_ref[pl.program_id(1)]
    per_q = tq // tk
    to_log2 = SCALE * LOG2E

    for cidx in range(seq // tk):
        rows = slice(cidx * tk, (cidx + 1) * tk)
        vt_sc[:, rows] = v_ref[rows, :].astype(F32).T.astype(BF16)
    kk = lax.broadcasted_iota(jnp.int32, (tk, tq), 0)
    r = lax.broadcasted_iota(jnp.int32, (tk, tq), 1)
    bias = (-slope / SCALE) * (r - kk).astype(F32)
    bias_sc[0] = bias
    for dblk in range(per_q):
        bias_sc[1 + dblk] = jnp.where(r >= kk + dblk * tk, bias, NEG)

    lp = lam_ref[...]
    lam_full = (jnp.exp(jnp.sum(lp[0:1] * lp[1:2], axis=-1, keepdims=True))
                - jnp.exp(jnp.sum(lp[2:3] * lp[3:4], axis=-1, keepdims=True)) + lam_init)
    out_gain = g_ref[...] * (1.0 - lam_init)

    def query_tile(qi, carry):
        base = qi * per_q
        q0 = pl.multiple_of(qi * tq, tq)
        qs = [q_ref[pl.ds(q0, tq), c * HEAD_DIM:(c + 1) * HEAD_DIM] for c in range(2)]

        def shift_of(kj):
            return (slope * LOG2E) * ((base - kj) * tk).astype(F32)

        def max_chunk(kj0, n, ms):
            ms = list(ms)
            for u in range(n):
                kj = kj0 + u
                start = pl.multiple_of(kj * tk, tk)
                b = bias_sc[jnp.maximum(kj - base + 1, 0)]
                for comp in range(2):
                    k = k_ref[pl.ds(start, tk), comp * HEAD_DIM:(comp + 1) * HEAD_DIM]
                    a = (lax.dot_general(k, qs[comp], CONTRACT_LAST, preferred_element_type=F32) + b) * to_log2
                    s_sc[comp, pl.ds(start, tk), :] = a
                    ms[comp] = jnp.maximum(ms[comp], jnp.max(a, axis=0, keepdims=True) - shift_of(kj))
            return tuple(ms)

        def over_blocks(chunk_fn, carry):
            done = 0
            for n in DIFF_UNROLL:
                if n % per_q:
                    continue
                n_it = (base + per_q - done) // n
                carry = lax.fori_loop(
                    0, n_it, lambda t, c, n=n, done=done: chunk_fn(done + t * n, n, c), carry)
                done = done + n_it * n
            return carry

        neg = jnp.full((1, tq), NEG, F32)
        ms = over_blocks(max_chunk, (neg, neg))

        def exp_chunk(kj0, n, ls):
            ls = list(ls)
            start0 = pl.multiple_of(kj0 * tk, tk)
            vt = vt_sc[:, pl.ds(start0, n * tk)]
            for comp in range(2):
                es = []
                for u in range(n):
                    start = pl.multiple_of((kj0 + u) * tk, tk)
                    e = jnp.exp2(s_sc[comp, pl.ds(start, tk), :] - (ms[comp] + shift_of(kj0 + u)))
                    ls[comp] = ls[comp] + jnp.sum(e, axis=0, keepdims=True)
                    es.append(e.astype(BF16))
                e_cat = es[0] if n == 1 else jnp.concatenate(es, axis=0)
                acc_sc[comp] += jnp.dot(vt, e_cat, preferred_element_type=F32)
            return tuple(ls)

        acc_sc[...] = jnp.zeros(acc_sc.shape, F32)
        zero = jnp.zeros((1, tq), F32)
        l1, l2 = over_blocks(exp_chunk, (zero, zero))

        y_t = acc_sc[0] * (1.0 / l1) - lam_full * (acc_sc[1] * (1.0 / l2))
        ms2 = jnp.mean(y_t * y_t, axis=0, keepdims=True)
        y = (y_t * lax.rsqrt(ms2 + EPS)).T * out_gain
        o_ref[pl.ds(q0, tq), :] = (y * z_ref[pl.ds(q0, tq), :]).astype(o_ref.dtype)
        return carry

    lax.fori_loop(0, seq // tq, query_tile, 0)


def diff_attention(cq, ck, cv, slopes, lam_l, subln_g, z, *, seq, zcol, tq, tk, lam_init):
    t = cq.shape[0]
    n_batch = t // seq
    hw = 2 * HEAD_DIM
    kern = functools.partial(_diff_kernel, tq=tq, tk=tk, seq=seq, lam_init=lam_init)
    whole = pl.BlockSpec((seq, hw), lambda b, h: (b, h))
    return pl.pallas_call(
        kern,
        out_shape=jax.ShapeDtypeStruct((t, C_OUT), BF16),
        grid=(n_batch, C_HEADS),
        in_specs=[
            pl.BlockSpec(memory_space=pltpu.SMEM),
            whole, whole, whole,
            pl.BlockSpec((4, HEAD_DIM), lambda b, h: (0, 0)),
            pl.BlockSpec((1, hw), lambda b, h: (0, 0)),
            pl.BlockSpec((seq, hw), lambda b, h: (b, zcol // hw + h)),
        ],
        out_specs=whole,
        scratch_shapes=[pltpu.VMEM((hw, seq), BF16), pltpu.VMEM((2, hw, tq), F32),
                        pltpu.VMEM((1 + tq // tk, tk, tq), F32), pltpu.VMEM((2, seq, tq), F32)],
        compiler_params=_params(2),
        name="diff_attention",
    )(slopes, cq, ck, cv, lam_l, subln_g.reshape(1, hw), z)


def _mem_kernel(q_ref, k_ref, v_ref, z_ref, o_ref):
    for h in range(M_HEADS):
        hs = slice(h * HEAD_DIM, (h + 1) * HEAD_DIM)
        s_t = lax.dot_general(k_ref[:, hs], q_ref[:, hs], CONTRACT_LAST,
                              preferred_element_type=F32) * (SCALE * LOG2E)
        m = jnp.max(s_t, axis=0, keepdims=True)
        e = jnp.exp2(s_t - m)
        p_t = (e * (1.0 / jnp.sum(e, axis=0, keepdims=True))).astype(BF16)
        o = lax.dot_general(p_t, v_ref[:, hs], CONTRACT_FIRST, preferred_element_type=F32)
        o_ref[:, hs] = (o * z_ref[:, hs]).astype(o_ref.dtype)


def mem_extra(cp, mkv, z, *, seq, qcol, zcol, tq):
    t = cp.shape[0]
    nq = seq // tq
    in_specs = [
        pl.BlockSpec((tq, M_OUT), lambda j, i: (i, qcol // M_OUT)),
        pl.BlockSpec((N_MEM, M_OUT), lambda j, i: (i // nq, 0)),
        pl.BlockSpec((N_MEM, M_OUT), lambda j, i: (i // nq, 1)),
        pl.BlockSpec((tq, M_OUT), lambda j, i: (i, zcol // M_OUT)),
    ]
    out_shape = [jax.ShapeDtypeStruct((t, M_OUT), BF16)]
    out_specs = [pl.BlockSpec((tq, M_OUT), lambda j, i: (i, 0))]

    def body(in_refs, out_refs, scratch_refs, tile, proj_fn):
        proj_fn()
        _mem_kernel(*in_refs, out_refs[0])

    return in_specs, [cp, mkv, mkv, z], out_shape, out_specs, [], body


_BRANCH_ROWS = ((0, A_OUT), (A_OUT, A_OUT + B_OUT), (A_OUT + B_OUT, A_OUT + B_OUT + C_OUT),
                (A_OUT + B_OUT + C_OUT, A_OUT + B_OUT + C_OUT + M_OUT))
_WB_CHUNK = 512


def _merge_kernel(*refs):
    h_ref, us = refs[0], refs[1:5]
    wgs, bgs = refs[5:9], refs[9:13]
    n_chunk = (A_OUT + B_OUT + C_OUT + M_OUT) // _WB_CHUNK
    wbs = refs[13:13 + n_chunk]
    o_ref, wg_sc, wb_sc = refs[13 + n_chunk:]

    @pl.when(pl.program_id(1) == 0)
    def _():
        for b in range(N_BRANCH):
            wg_sc[b] = wgs[b][...].astype(BF16)
        for c in range(n_chunk):
            wb_sc[c * _WB_CHUNK:(c + 1) * _WB_CHUNK, :] = wbs[c][...].astype(BF16)

    h = h_ref[...]
    acc = None
    for b in range(N_BRANCH):
        lo, hi = _BRANCH_ROWS[b]
        gate = _sigmoid(jnp.dot(h, wg_sc[b], preferred_element_type=F32) + bgs[b][...])
        term = gate * jnp.dot(us[b][...], wb_sc[lo:hi, :], preferred_element_type=F32)
        acc = term if acc is None else acc + term
    o_ref[...] = acc.astype(o_ref.dtype)


def gated_merge(h, us, w_in, b_gate3, w_branch, layer, tm, tn):
    t, d = h.shape
    nj = d // tn
    gate0 = O_GATE // tn
    n_chunk = w_branch.shape[1] // _WB_CHUNK
    in_specs = [pl.BlockSpec((tm, d), lambda j, i: (i, 0))]
    in_specs += [pl.BlockSpec((tm, u.shape[1]), lambda j, i: (i, 0)) for u in us]
    in_specs += [pl.BlockSpec((None, d, tn), functools.partial(
        lambda j, i, b: (layer, 0, gate0 + b * nj + j), b=b)) for b in range(N_BRANCH)]
    in_specs += [pl.BlockSpec((None, 1, tn), functools.partial(
        lambda j, i, b: (layer, 0, b * nj + j), b=b)) for b in range(N_BRANCH)]
    in_specs += [pl.BlockSpec((None, _WB_CHUNK, tn), functools.partial(
        lambda j, i, c: (layer, c, j), c=c)) for c in range(n_chunk)]
    return pl.pallas_call(
        _merge_kernel,
        out_shape=jax.ShapeDtypeStruct((t, d), BF16),
        grid=(nj, t // tm),
        in_specs=in_specs,
        out_specs=pl.BlockSpec((tm, tn), lambda j, i: (i, j)),
        scratch_shapes=[pltpu.VMEM((N_BRANCH, d, tn), BF16),
                        pltpu.VMEM((w_branch.shape[1], tn), BF16)],
        compiler_params=_params(2),
        name="gated_merge",
    )(h, *us, *([w_in] * N_BRANCH), *([b_gate3] * N_BRANCH), *([w_branch] * n_chunk))


def _out_kernel(x_ref, m_ref, w_ref, o_ref, w_sc):
    @pl.when(pl.program_id(1) == 0)
    def _():
        w_sc[...] = w_ref[...].astype(BF16)

    o_ref[...] = x_ref[...] + jnp.dot(m_ref[...], w_sc[...], preferred_element_type=F32)


def out_proj(x, merged, w_out, layer, tm, tn):
    t, d = x.shape
    return pl.pallas_call(
        _out_kernel,
        out_shape=jax.ShapeDtypeStruct((t, d), F32),
        grid=(d // tn, t // tm),
        in_specs=[pl.BlockSpec((tm, tn), lambda j, i: (i, j)),
                  pl.BlockSpec((tm, d), lambda j, i: (i, 0)),
                  pl.BlockSpec((None, d, tn), lambda j, i: (layer, 0, j))],
        out_specs=pl.BlockSpec((tm, tn), lambda j, i: (i, j)),
        scratch_shapes=[pltpu.VMEM((d, tn), BF16)],
        compiler_params=_params(2),
        name="out_proj",
    )(x, merged, w_out)


def kernel(x, mem, norm_g, w_in, b_gate, qk_gain, sinks, lam, subln_g, mem_norm_g,
           w_mem_kv, w_branch, w_out):
    bsz, s_len, d = x.shape
    depth = w_in.shape[0]
    t = bsz * s_len
    xf = x.reshape(t, d)
    memf = mem.reshape(bsz * N_MEM, d)
    b_gate3 = b_gate.reshape(depth, 1, N_BRANCH * d)

    slopes_a = _alibi_slopes(A_HEADS_PER_GROUP)
    slopes_b = _alibi_slopes(B_Q_HEADS)
    slopes_c = jnp.asarray(_alibi_slopes(C_HEADS), F32)
    ones = lambda n: jnp.ones((n,), F32)
    dils = tuple(dil for _, dil in A_PATTERNS)

    pc = PROJ_PIECE
    z_zb, z_zc, z_za, z_zm = 0, B_OUT, B_OUT + C_OUT, B_OUT + C_OUT + A_OUT
    g_bm = B_OUT
    g_cq = g_bm + 2 * B_KV_HEADS * HEAD_DIM + M_OUT
    g_ck, g_cv = g_cq + C_OUT, g_cq + 2 * C_OUT

    for l in range(depth):
        gq = qk_gain[l]
        gain_a = jnp.concatenate([jnp.tile(gq[QK_A_Q], 4), jnp.tile(gq[QK_A_K], 4), ones(A_OUT)]).reshape(1, -1)
        gain_bc = jnp.concatenate([
            jnp.tile(gq[QK_B_Q], B_Q_HEADS), jnp.tile(gq[QK_B_K], B_KV_HEADS), ones(B_KV_HEADS * HEAD_DIM),
            jnp.tile(gq[QK_M_Q], M_HEADS), jnp.tile(gq[QK_C_Q], 2 * C_HEADS),
            jnp.tile(gq[QK_C_K], 2 * C_HEADS), ones(C_OUT)]).reshape(1, -1)
        gain_z = ones(O_GATE - O_ZA).reshape(1, -1)
        gain_m = jnp.concatenate([jnp.tile(gq[QK_M_K], M_HEADS), ones(M_OUT)]).reshape(1, -1)

        bm, h, *hps = norm_proj(xf, norm_g[l], dils[1:], w_in, l, [O_BK // pc, O_MQ // pc],
                                gain_bc[:, g_bm:g_cq], "NNPPNNNN", HALF_TILE, BF16, "norm_proj_bm")
        h_perm = [h] + [hp.reshape(t, d) for hp in hps]

        def a_cols(g):
            return [O_AQ // pc + g, O_AK // pc + g, O_AV // pc + g]

        def a_attn(g):
            win, dil = A_PATTERNS[g]
            return dict(rows_per_seq=s_len // dil, qcol=0, kcol=A_OUT, vcol=2 * A_OUT,
                        n_q=A_HEADS_PER_GROUP, n_kv=A_HEADS_PER_GROUP, slopes=slopes_a,
                        max_dist=win // dil, dist_scale=dil, tq=ROW_TILE, out_dtype=F32, unit=1, write_lse=True)

        a_mode = "NNNNNNNNPPPP"
        n8 = "NNNNNNNN"
        sz = proj(h, w_in, l, [(O_ZB // pc, O_ZC // pc, O_ZA // pc), (O_ZB // pc + 1, O_ZC // pc + 1, O_ZM // pc)],
                  gain_z, ("SSSSSSSS",) * 3, ROW_TILE, F32, "proj_z")
        ap0 = proj(h_perm[0], w_in, l, [(c,) for c in a_cols(0)], gain_a, (a_mode,), ROW_TILE, BF16, "proj_a0")
        ap1, o0, l0 = proj_fused(h_perm[1], w_in, l, a_cols(1), gain_a, a_mode, ROW_TILE, BF16,
                                 banded_extra(ap0, ap0, **a_attn(0)), "proj_a1_attn_a0")
        ap2, o1, l1 = proj_fused(h_perm[2], w_in, l, a_cols(2), gain_a, a_mode, ROW_TILE, BF16,
                                 banded_extra(ap1, ap1, **a_attn(1)), "proj_a2_attn_a1")
        bqa, o2, l2 = proj_fused(h, w_in, l, [O_BQ // pc, O_BQ // pc + 1], gain_bc[:, :g_bm], n8, ROW_TILE, BF16,
                                 banded_extra(ap2, ap2, **a_attn(2)), "proj_bq_attn_a2")
        mkv = norm_proj(memf, mem_norm_g[l], (), w_mem_kv, l, [0, 1], gain_m, "NNNNPPPP", bsz * N_MEM, BF16,
                        "norm_proj_mem")[0]
        cq, u_b = proj_fused(
            h, w_in, l, [O_CQ // pc, O_CQ // pc + 1], gain_bc[:, g_cq:g_ck], n8, ROW_TILE, BF16,
            banded_extra(bqa, bm, rows_per_seq=s_len, qcol=0, kcol=0, vcol=B_KV_HEADS * HEAD_DIM,
                         n_q=B_Q_HEADS, n_kv=B_KV_HEADS, slopes=slopes_b, max_dist=B_WINDOW - 1, dist_scale=1,
                         tq=ROW_TILE, out_dtype=BF16, batch=1, sinks=sinks[l], z=sz, zcol=z_zb), "proj_cq_attn_b")
        ck, u_m = proj_fused(
            h, w_in, l, [O_CK // pc, O_CK // pc + 1], gain_bc[:, g_ck:g_cv], n8, ROW_TILE, BF16,
            mem_extra(bm, mkv, sz, seq=s_len, qcol=2 * B_KV_HEADS * HEAD_DIM, zcol=z_zm, tq=ROW_TILE),
            "proj_ck_attn_m")
        cv, u_a = proj_fused(h, w_in, l, [O_CV // pc, O_CV // pc + 1], gain_bc[:, g_cv:], "PPPPPPPP", HALF_TILE,
                             BF16, combine_extra([o0, o1, o2], [l0, l1, l2], sz, z_za, HALF_TILE, dils[1:]),
                             "proj_cv_combine_a")

        lam_init = 0.8 - 0.6 * math.exp(-0.3 * l)
        u_c = diff_attention(cq, ck, cv, slopes_c, lam[l], subln_g[l], sz, seq=s_len,
                             zcol=z_zc, tq=DIFF_TQ, tk=DIFF_TK, lam_init=lam_init)

        merged = gated_merge(h, [u_a, u_b, u_c, u_m], w_in, b_gate3, w_branch, l, ROW_TILE, MERGE_TN)
        xf = out_proj(xf, merged, w_out, l, ROW_TILE, OUT_TN)

    return xf.reshape(bsz, s_len, d)
```

```python
import functools
import math

import jax
import jax.numpy as jnp
from jax import lax
from jax.experimental import pallas as pl
from jax.experimental.pallas import tpu as pltpu

D_MODEL = 2048
HEAD_DIM = 128
BLOCK = 128
EPS = 1e-6
A_PATTERNS = ((128, 1), (512, 4), (2048, 16))
A_GROUPS = 3
A_HEADS_PER_GROUP = 4
A_HEADS = A_GROUPS * A_HEADS_PER_GROUP
A_OUT = A_HEADS_PER_GROUP * HEAD_DIM
B_Q_HEADS = 8
B_KV_HEADS = 2
B_WINDOW = 128
B_OUT = B_Q_HEADS * HEAD_DIM
C_HEADS = 4
C_OUT = C_HEADS * 2 * HEAD_DIM
N_MEM = 256
M_HEADS = 4
M_OUT = M_HEADS * HEAD_DIM
N_BRANCH = 4
QK_A_Q, QK_A_K, QK_B_Q, QK_B_K, QK_C_Q, QK_C_K, QK_M_Q, QK_M_K = range(8)

_IN_SIZES = (A_HEADS * HEAD_DIM, A_HEADS * HEAD_DIM, A_HEADS * HEAD_DIM,
             B_Q_HEADS * HEAD_DIM, B_KV_HEADS * HEAD_DIM, B_KV_HEADS * HEAD_DIM,
             C_HEADS * 2 * HEAD_DIM, C_HEADS * 2 * HEAD_DIM, C_OUT,
             M_HEADS * HEAD_DIM, A_OUT, B_OUT, C_OUT, M_OUT, N_BRANCH * D_MODEL)
_OFF = [0]
for _s in _IN_SIZES:
    _OFF.append(_OFF[-1] + _s)
(O_AQ, O_AK, O_AV, O_BQ, O_BK, O_BV, O_CQ, O_CK, O_CV, O_MQ,
 O_ZA, O_ZB, O_ZC, O_ZM, O_GATE, O_END) = _OFF

NEG = -1e30
SCALE = HEAD_DIM ** -0.5
VMEM_LIMIT = 58 * 1024 * 1024
PROJ_PIECE = 512
ROW_TILE = 1024
HALF_TILE = 512
MERGE_TN = 256
OUT_TN = 1024
DIFF_TQ, DIFF_TK = 512, 512
DIFF_UNROLL = (4, 3, 2, 1)
LOG2E = math.log2(math.e)
BF16 = jnp.bfloat16
F32 = jnp.float32
CONTRACT_LAST = (((1,), (1,)), ((), ()))
CONTRACT_FIRST = (((0,), (0,)), ((), ()))


def _params(n_axes):
    return pltpu.CompilerParams(dimension_semantics=("arbitrary",) * n_axes,
                                vmem_limit_bytes=VMEM_LIMIT)


def _sigmoid(x):
    return 0.5 * jnp.tanh(0.5 * x) + 0.5


def _alibi_slopes(n):
    return [2.0 ** (-8.0 * i / n) for i in range(1, n + 1)]


def _norm_proj_kernel(*refs, dils, tm, sub, n_piece, tile_mode):
    x_ref, g_ref = refs[0], refs[1]
    w_refs, gain_ref = refs[2:2 + n_piece], refs[2 + n_piece]
    o_ref, h_ref = refs[3 + n_piece], refs[4 + n_piece]
    perm_refs = refs[5 + n_piece:5 + n_piece + len(dils)]
    w_sc = refs[5 + n_piece + len(dils)]
    x = x_ref[...]
    ms = jnp.mean(x * x, axis=-1, keepdims=True)
    h_ref[...] = (x * lax.rsqrt(ms + EPS) * g_ref[...]).astype(BF16)
    _proj_body(h_ref, w_refs, gain_ref, o_ref, w_sc, (tile_mode,))
    row = lax.broadcasted_iota(jnp.int32, (sub, sub), 0)
    col = lax.broadcasted_iota(jnp.int32, (sub, sub), 1)
    for d, p_ref in zip(dils, perm_refs):
        n = sub // d
        perm = jnp.where(col == (row % n) * d + row // n, 1.0, 0.0).astype(BF16)
        for s0 in range(tm // sub):
            hp = jnp.dot(perm, h_ref[s0 * sub:(s0 + 1) * sub, :], preferred_element_type=F32).astype(BF16)
            for p in range(d):
                p_ref[p, s0 * n:(s0 + 1) * n, :] = hp[p * n:(p + 1) * n, :]


def norm_proj(x, g, dils, w3, layer, piece_cols, gain, tile_mode, tm, out_dtype, name):
    t, d = x.shape
    n_piece = len(piece_cols)
    tn = n_piece * PROJ_PIECE
    sub = 256
    assert len(tile_mode) == tn // HEAD_DIM and gain.shape == (1, tn) and tm % sub == 0
    out_shape = [jax.ShapeDtypeStruct((t, tn), out_dtype), jax.ShapeDtypeStruct((t, d), BF16)]
    out_specs = [pl.BlockSpec((tm, tn), lambda j, i: (i, 0)), pl.BlockSpec((tm, d), lambda j, i: (i, 0))]
    for dil in dils:
        out_shape.append(jax.ShapeDtypeStruct((dil, t // dil, d), BF16))
        out_specs.append(pl.BlockSpec((dil, tm // dil, d), lambda j, i: (0, i, 0)))
    w_specs = [pl.BlockSpec((None, d, PROJ_PIECE), functools.partial(lambda j, i, c: (layer, 0, c), c=c))
               for c in piece_cols]
    return pl.pallas_call(
        functools.partial(_norm_proj_kernel, dils=tuple(dils), tm=tm, sub=sub, n_piece=n_piece,
                          tile_mode=tile_mode),
        out_shape=out_shape,
        grid=(1, t // tm),
        in_specs=[pl.BlockSpec((tm, d), lambda j, i: (i, 0)), pl.BlockSpec((1, d), lambda j, i: (0, 0))]
                 + w_specs + [pl.BlockSpec((1, tn), lambda j, i: (0, 0))],
        out_specs=out_specs,
        scratch_shapes=[pltpu.VMEM((d, tn), BF16)],
        compiler_params=_params(2),
        name=name,
    )(x, g.reshape(1, d), *([w3] * n_piece), gain)


def _proj_kernel(*refs, tile_modes, n_piece):
    h_ref, w_refs = refs[0], refs[1:1 + n_piece]
    gain_ref, o_ref, w_sc = refs[1 + n_piece:]
    _proj_body(h_ref, w_refs, gain_ref, o_ref, w_sc, tile_modes)


def _proj_body(h_ref, w_refs, gain_ref, o_ref, w_sc, tile_modes):
    _proj_cast(w_refs, w_sc)
    _proj_compute(h_ref, gain_ref, o_ref, w_sc, tile_modes)


def _proj_cast(w_refs, w_sc):
    @pl.when(pl.program_id(1) == 0)
    def _():
        for p in range(len(w_refs)):
            w_sc[:, p * PROJ_PIECE:(p + 1) * PROJ_PIECE] = w_refs[p][...].astype(BF16)


def _proj_compute(h_ref, gain_ref, o_ref, w_sc, tile_modes):
    j = pl.program_id(0)
    acc = jnp.dot(h_ref[...], w_sc[...], preferred_element_type=F32)
    for c in range(len(tile_modes[0])):
        sl = slice(c * HEAD_DIM, (c + 1) * HEAD_DIM)
        modes = [m[c] for m in tile_modes]
        y = acc[:, sl]
        if 'S' in modes:
            assert set(modes) == {'S'}
            y = y * _sigmoid(y)
        elif 'N' in modes:
            ms = jnp.mean(y * y, axis=-1, keepdims=True)
            normed = y * lax.rsqrt(ms + EPS) * gain_ref[:, sl]
            if set(modes) == {'N'}:
                y = normed
            else:
                is_norm = functools.reduce(jnp.logical_or, [j == jj for jj, m in enumerate(modes) if m == 'N'])
                y = jnp.where(is_norm, normed, y)
        o_ref[:, sl] = y.astype(o_ref.dtype)


def _select(j, values):
    out = values[-1]
    for jj in range(len(values) - 2, -1, -1):
        out = jnp.where(j == jj, values[jj], out)
    return out


def proj(h, w3, layer, piece_cols, gain, tile_modes, tm, out_dtype, name):
    t, k = h.shape
    n_piece = len(piece_cols)
    tn = n_piece * PROJ_PIECE
    n_tiles = len(tile_modes)
    assert t % tm == 0 and all(len(m) == tn // HEAD_DIM for m in tile_modes)
    assert all(len(cols) == n_tiles for cols in piece_cols) and gain.shape == (1, tn * n_tiles)
    w_specs = [pl.BlockSpec((None, k, PROJ_PIECE),
                            functools.partial(lambda j, i, cols: (layer, 0, _select(j, cols)), cols=tuple(cols)))
               for cols in piece_cols]
    return pl.pallas_call(
        functools.partial(_proj_kernel, tile_modes=tuple(tile_modes), n_piece=n_piece),
        out_shape=jax.ShapeDtypeStruct((t, tn * n_tiles), out_dtype),
        grid=(n_tiles, t // tm),
        in_specs=[pl.BlockSpec((tm, k), lambda j, i: (i, 0))] + w_specs
                 + [pl.BlockSpec((1, tn), lambda j, i: (0, j))],
        out_specs=pl.BlockSpec((tm, tn), lambda j, i: (i, j)),
        scratch_shapes=[pltpu.VMEM((k, tn), BF16)],
        compiler_params=_params(2),
        name=name,
    )(h, *([w3] * n_piece), gain)


def _banded_body(q_ref, kc_ref, kp_ref, vc_ref, vp_ref, sink_ref, z_ref, o_ref, lse_ref, *, tile,
                 n_q, n_kv, slopes, max_dist, dist_scale, tq, rows_per_seq, unit, batch, between=None):
    has_sink, mul_z, write_lse = sink_ref is not None, z_ref is not None, lse_ref is not None
    shared_kv = n_kv < n_q and unit > 1
    group = n_q // n_kv
    assert n_q % unit == 0 and (not shared_kv or group == unit)
    kk = lax.broadcasted_iota(jnp.int32, (2 * BLOCK, BLOCK), 0)
    r = lax.broadcasted_iota(jnp.int32, (2 * BLOCK, BLOCK), 1)
    dist = r + BLOCK - kk
    distf = (dist * dist_scale).astype(F32)
    valid = (dist >= 0) & (dist <= max_dist)
    valid_start = valid & (kk >= BLOCK)
    if rows_per_seq > tq:
        has_prev = jnp.minimum(tile % (rows_per_seq // tq), 1) * BLOCK
        valid0 = valid & (kk + has_prev >= BLOCK)
    else:
        valid0 = valid_start

    def window(cur_ref, prev_ref, kv, j):
        ks = slice(kv * HEAD_DIM, (kv + 1) * HEAD_DIM)
        prev = prev_ref[:, ks] if j == 0 else cur_ref[(j - 1) * BLOCK:j * BLOCK, ks]
        return jnp.concatenate([prev, cur_ref[j * BLOCK:(j + 1) * BLOCK, ks]], axis=0)

    units = [(list(range(u * unit, (u + 1) * unit)), j)
             for u in range(n_q // unit) for j in range(tq // BLOCK)]

    for b0 in range(0, len(units), batch or len(units)):
        batch_units = units[b0:b0 + (batch or len(units))]
        scores = []
        for heads, j in batch_units:
            rows = slice(j * BLOCK, (j + 1) * BLOCK)
            qs = [q_ref[rows, h * HEAD_DIM:(h + 1) * HEAD_DIM] for h in heads]
            if shared_kv:
                s_t = lax.dot_general(window(kc_ref, kp_ref, heads[0] // group, j), jnp.concatenate(qs, axis=0),
                                      CONTRACT_LAST, preferred_element_type=F32)
            else:
                s_t = jnp.concatenate(
                    [lax.dot_general(window(kc_ref, kp_ref, h // group, j), q, CONTRACT_LAST,
                                     preferred_element_type=F32) for h, q in zip(heads, qs)], axis=1)
            scores.append(s_t)
        if between is not None and b0 == 0:
            between()

        probs, lses = [], []
        for (heads, j), s_t in zip(batch_units, scores):
            v_mask = valid0 if j == 0 else (valid_start if (j * BLOCK) % rows_per_seq == 0 else valid)
            bias = jnp.concatenate([jnp.where(v_mask, (-slopes[h] * LOG2E) * distf, NEG) for h in heads], axis=1)
            s_t = s_t * (SCALE * LOG2E) + bias
            m = jnp.max(s_t, axis=0, keepdims=True)
            if has_sink:
                sink = jnp.concatenate([jnp.full((1, BLOCK), sink_ref[h] * LOG2E, F32) for h in heads], axis=1)
                m = jnp.maximum(m, sink)
            e = jnp.exp2(s_t - m)
            denom = jnp.sum(e, axis=0, keepdims=True)
            if has_sink:
                denom = denom + jnp.exp2(sink - m)
            probs.append((e * (1.0 / denom)).astype(BF16))
            lses.append(m + jnp.log2(denom) if write_lse else None)

        for (heads, j), p_t, lse in zip(batch_units, probs, lses):
            rows = slice(j * BLOCK, (j + 1) * BLOCK)
            if shared_kv:
                o_all = lax.dot_general(p_t, window(vc_ref, vp_ref, heads[0] // group, j), CONTRACT_FIRST,
                                        preferred_element_type=F32)
            for g, h in enumerate(heads):
                hs = slice(h * HEAD_DIM, (h + 1) * HEAD_DIM)
                gs = slice(g * BLOCK, (g + 1) * BLOCK)
                if shared_kv:
                    o = o_all[gs, :]
                else:
                    o = lax.dot_general(p_t[:, gs], window(vc_ref, vp_ref, h // group, j), CONTRACT_FIRST,
                                        preferred_element_type=F32)
                if mul_z:
                    o = o * z_ref[rows, hs]
                o_ref[rows, hs] = o.astype(o_ref.dtype)
                if write_lse:
                    lse_ref[rows, hs] = jnp.broadcast_to(lse[:, gs], (BLOCK, BLOCK)).T


def _banded_specs(q_arr, kv_arr, *, qcol, kcol, vcol, n_q, n_kv, tq, out_dtype, sinks, z, zcol, write_lse,
                  tile_of):
    rows_total = q_arr.shape[0]
    qw, kw = n_q * HEAD_DIM, n_kv * HEAD_DIM
    assert qcol % qw == 0 and kcol % kw == 0 and vcol % kw == 0
    sub = tq // BLOCK

    def cur(col, w):
        return lambda *g: (tile_of(*g), col // w)

    def prev(col, w):
        return lambda *g: (jnp.maximum(tile_of(*g) * sub - 1, 0), col // w)

    in_specs = [
        pl.BlockSpec((tq, qw), cur(qcol, qw)),
        pl.BlockSpec((tq, kw), cur(kcol, kw)),
        pl.BlockSpec((BLOCK, kw), prev(kcol, kw)),
        pl.BlockSpec((tq, kw), cur(vcol, kw)),
        pl.BlockSpec((BLOCK, kw), prev(vcol, kw)),
    ]
    args = [q_arr, kv_arr, kv_arr, kv_arr, kv_arr]
    if sinks is not None:
        in_specs.append(pl.BlockSpec(memory_space=pltpu.SMEM))
        args.append(sinks)
    if z is not None:
        assert zcol % qw == 0
        in_specs.append(pl.BlockSpec((tq, qw), cur(zcol, qw)))
        args.append(z)
    out_block = pl.BlockSpec((tq, qw), cur(0, qw))
    out_shape = [jax.ShapeDtypeStruct((rows_total, qw), out_dtype)]
    out_specs = [out_block]
    if write_lse:
        out_shape.append(jax.ShapeDtypeStruct((rows_total, qw), F32))
        out_specs.append(out_block)
    return in_specs, args, out_shape, out_specs


def _banded_cfg(*, rows_per_seq, n_q, n_kv, slopes, max_dist, dist_scale, tq, unit, batch):
    assert tq % rows_per_seq == 0 or rows_per_seq % tq == 0
    return dict(n_q=n_q, n_kv=n_kv, slopes=tuple(slopes), max_dist=max_dist, dist_scale=dist_scale,
                tq=tq, rows_per_seq=rows_per_seq, unit=unit, batch=batch)


def _proj_fused_kernel(*refs, tile_mode, n_piece, n_in, n_out, body):
    h_ref, w_refs, gain_ref = refs[0], refs[1:1 + n_piece], refs[1 + n_piece]
    extra_in = refs[2 + n_piece:2 + n_piece + n_in]
    p_out = refs[2 + n_piece + n_in]
    extra_out = refs[3 + n_piece + n_in:3 + n_piece + n_in + n_out]
    w_sc = refs[3 + n_piece + n_in + n_out]
    extra_scratch = refs[4 + n_piece + n_in + n_out:]
    _proj_cast(w_refs, w_sc)
    body(extra_in, extra_out, extra_scratch, pl.program_id(1),
         lambda: _proj_compute(h_ref, gain_ref, p_out, w_sc, (tile_mode,)))


def proj_fused(h, w3, layer, piece_cols, gain, tile_mode, tm, out_dtype, extra, name):
    t, k = h.shape
    n_piece = len(piece_cols)
    tn = n_piece * PROJ_PIECE
    assert len(tile_mode) == tn // HEAD_DIM and gain.shape == (1, tn)
    e_specs, e_args, e_shape, e_out_specs, e_scratch, body = extra
    w_specs = [pl.BlockSpec((None, k, PROJ_PIECE), functools.partial(lambda j, i, c: (layer, 0, c), c=c))
               for c in piece_cols]
    return pl.pallas_call(
        functools.partial(_proj_fused_kernel, tile_mode=tile_mode, n_piece=n_piece,
                          n_in=len(e_specs), n_out=len(e_shape), body=body),
        out_shape=[jax.ShapeDtypeStruct((t, tn), out_dtype)] + list(e_shape),
        grid=(1, t // tm),
        in_specs=[pl.BlockSpec((tm, k), lambda j, i: (i, 0))] + w_specs
                 + [pl.BlockSpec((1, tn), lambda j, i: (0, 0))] + list(e_specs),
        out_specs=[pl.BlockSpec((tm, tn), lambda j, i: (i, 0))] + list(e_out_specs),
        scratch_shapes=[pltpu.VMEM((k, tn), BF16)] + list(e_scratch),
        compiler_params=_params(2),
        name=name,
    )(h, *([w3] * n_piece), gain, *e_args)


def banded_extra(q_arr, kv_arr, *, rows_per_seq, qcol, kcol, vcol, n_q, n_kv, slopes, max_dist, dist_scale,
                 tq, out_dtype, unit=4, batch=None, sinks=None, z=None, zcol=0, write_lse=False):
    in_specs, args, out_shape, out_specs = _banded_specs(
        q_arr, kv_arr, qcol=qcol, kcol=kcol, vcol=vcol, n_q=n_q, n_kv=n_kv, tq=tq, out_dtype=out_dtype,
        sinks=sinks, z=z, zcol=zcol, write_lse=write_lse, tile_of=lambda j, i: i)
    cfg = _banded_cfg(rows_per_seq=rows_per_seq, n_q=n_q, n_kv=n_kv, slopes=slopes, max_dist=max_dist,
                      dist_scale=dist_scale, tq=tq, unit=unit, batch=batch)
    has_sink, mul_z = sinks is not None, z is not None

    def body(in_refs, out_refs, scratch_refs, tile, proj_fn):
        in_refs = list(in_refs)
        sink_ref = in_refs[5] if has_sink else None
        z_ref = in_refs[5 + has_sink] if mul_z else None
        lse_ref = out_refs[1] if write_lse else None
        _banded_body(*in_refs[:5], sink_ref, z_ref, out_refs[0], lse_ref, tile=tile, between=proj_fn, **cfg)

    return in_specs, args, out_shape, out_specs, [], body


def _combine_kernel(o0, o1, o2, l0, l1, l2, z_ref, u_ref, o1_sc, o2_sc, l1_sc, l2_sc, *, tm, dils):
    n_chunk = o0.shape[1] // HEAD_DIM
    for src, dst, d in ((o1, o1_sc, dils[0]), (l1, l1_sc, dils[0]),
                        (o2, o2_sc, dils[1]), (l2, l2_sc, dils[1])):
        n = tm // d
        for p in range(d):
            for c in range(n_chunk):
                dst[c, pl.ds(p, n, stride=d), :] = src[p, :, c * HEAD_DIM:(c + 1) * HEAD_DIM]
    for c in range(n_chunk):
        cs = slice(c * HEAD_DIM, (c + 1) * HEAD_DIM)
        a0, a1, a2 = l0[:, cs], l1_sc[c], l2_sc[c]
        m = jnp.maximum(jnp.maximum(a0, a1), a2)
        e0, e1, e2 = jnp.exp2(a0 - m), jnp.exp2(a1 - m), jnp.exp2(a2 - m)
        inv = 1.0 / (e0 + e1 + e2)
        ya = (e0 * inv) * o0[:, cs] + (e1 * inv) * o1_sc[c] + (e2 * inv) * o2_sc[c]
        u_ref[:, cs] = (ya * z_ref[:, cs]).astype(u_ref.dtype)


def combine_extra(outs, lses, z, zcol, tm, dils):
    t, w = outs[0].shape
    blk = pl.BlockSpec((tm, w), lambda j, i: (i, 0))

    def pblk(d):
        return pl.BlockSpec((d, tm // d, w), lambda j, i: (0, i, 0))

    def p3(a, d):
        return a.reshape(d, t // d, w)

    d1, d2 = dils
    in_specs = [blk, pblk(d1), pblk(d2), blk, pblk(d1), pblk(d2),
                pl.BlockSpec((tm, w), lambda j, i: (i, zcol // w))]
    args = [outs[0], p3(outs[1], d1), p3(outs[2], d2), lses[0], p3(lses[1], d1), p3(lses[2], d2), z]

    def body(in_refs, out_refs, scratch_refs, tile, proj_fn):
        _combine_kernel(*in_refs, out_refs[0], *scratch_refs, tm=tm, dils=dils)
        proj_fn()

    return (in_specs, args, [jax.ShapeDtypeStruct((t, w), BF16)], [blk],
            [pltpu.VMEM((w // HEAD_DIM, tm, HEAD_DIM), F32)] * 4, body)


def _diff_kernel(slope_ref, q_ref, k_ref, v_ref, lam_ref, g_ref, z_ref, o_ref,
                 vt_sc, acc_sc, bias_sc, s_sc, *, tq, tk, seq, lam_init):
    slope = slope_ref[pl.program_id(1)]
    per_q = tq // tk
    to_log2 = SCALE * LOG2E

    for cidx in range(seq // tk):
        rows = slice(cidx * tk, (cidx + 1) * tk)
        vt_sc[:, rows] = v_ref[rows, :].astype(F32).T.astype(BF16)
    kk = lax.broadcasted_iota(jnp.int32, (tk, tq), 0)
    r = lax.broadcasted_iota(jnp.int32, (tk, tq), 1)
    bias = (-slope / SCALE) * (r - kk).astype(F32)
    bias_sc[0] = bias
    for dblk in range(per_q):
        bias_sc[1 + dblk] = jnp.where(r >= kk + dblk * tk, bias, NEG)

    lp = lam_ref[...]
    lam_full = (jnp.exp(jnp.sum(lp[0:1] * lp[1:2], axis=-1, keepdims=True))
                - jnp.exp(jnp.sum(lp[2:3] * lp[3:4], axis=-1, keepdims=True)) + lam_init)
    out_gain = g_ref[...] * (1.0 - lam_init)

    def query_tile(qi, carry):
        base = qi * per_q
        q0 = pl.multiple_of(qi * tq, tq)
        qs = [q_ref[pl.ds(q0, tq), c * HEAD_DIM:(c + 1) * HEAD_DIM] for c in range(2)]

        def shift_of(kj):
            return (slope * LOG2E) * ((base - kj) * tk).astype(F32)

        def max_chunk(kj0, n, ms):
            ms = list(ms)
            for u in range(n):
                kj = kj0 + u
                start = pl.multiple_of(kj * tk, tk)
                b = bias_sc[jnp.maximum(kj - base + 1, 0)]
                for comp in range(2):
                    k = k_ref[pl.ds(start, tk), comp * HEAD_DIM:(comp + 1) * HEAD_DIM]
                    a = (lax.dot_general(k, qs[comp], CONTRACT_LAST, preferred_element_type=F32) + b) * to_log2
                    s_sc[comp, pl.ds(start, tk), :] = a
                    ms[comp] = jnp.maximum(ms[comp], jnp.max(a, axis=0, keepdims=True) - shift_of(kj))
            return tuple(ms)

        def over_blocks(chunk_fn, carry):
            done = 0
            for n in DIFF_UNROLL:
                if n % per_q:
                    continue
                n_it = (base + per_q - done) // n
                carry = lax.fori_loop(
                    0, n_it, lambda t, c, n=n, done=done: chunk_fn(done + t * n, n, c), carry)
                done = done + n_it * n
            return carry

        neg = jnp.full((1, tq), NEG, F32)
        ms = over_blocks(max_chunk, (neg, neg))

        def exp_chunk(kj0, n, ls):
            ls = list(ls)
            start0 = pl.multiple_of(kj0 * tk, tk)
            vt = vt_sc[:, pl.ds(start0, n * tk)]
            for comp in range(2):
                es = []
                for u in range(n):
                    start = pl.multiple_of((kj0 + u) * tk, tk)
                    e = jnp.exp2(s_sc[comp, pl.ds(start, tk), :] - (ms[comp] + shift_of(kj0 + u)))
                    ls[comp] = ls[comp] + jnp.sum(e, axis=0, keepdims=True)
                    es.append(e.astype(BF16))
                e_cat = es[0] if n == 1 else jnp.concatenate(es, axis=0)
                acc_sc[comp] += jnp.dot(vt, e_cat, preferred_element_type=F32)
            return tuple(ls)

        acc_sc[...] = jnp.zeros(acc_sc.shape, F32)
        zero = jnp.zeros((1, tq), F32)
        l1, l2 = over_blocks(exp_chunk, (zero, zero))

        y_t = acc_sc[0] * (1.0 / l1) - lam_full * (acc_sc[1] * (1.0 / l2))
        ms2 = jnp.mean(y_t * y_t, axis=0, keepdims=True)
        y = (y_t * lax.rsqrt(ms2 + EPS)).T * out_gain
        o_ref[pl.ds(q0, tq), :] = (y * z_ref[pl.ds(q0, tq), :]).astype(o_ref.dtype)
        return carry

    lax.fori_loop(0, seq // tq, query_tile, 0)


def diff_attention(cq, ck, cv, slopes, lam_l, subln_g, z, *, seq, zcol, tq, tk, lam_init):
    t = cq.shape[0]
    n_batch = t // seq
    hw = 2 * HEAD_DIM
    kern = functools.partial(_diff_kernel, tq=tq, tk=tk, seq=seq, lam_init=lam_init)
    whole = pl.BlockSpec((seq, hw), lambda b, h: (b, h))
    return pl.pallas_call(
        kern,
        out_shape=jax.ShapeDtypeStruct((t, C_OUT), BF16),
        grid=(n_batch, C_HEADS),
        in_specs=[
            pl.BlockSpec(memory_space=pltpu.SMEM),
            whole, whole, whole,
            pl.BlockSpec((4, HEAD_DIM), lambda b, h: (0, 0)),
            pl.BlockSpec((1, hw), lambda b, h: (0, 0)),
            pl.BlockSpec((seq, hw), lambda b, h: (b, zcol // hw + h)),
        ],
        out_specs=whole,
        scratch_shapes=[pltpu.VMEM((hw, seq), BF16), pltpu.VMEM((2, hw, tq), F32),
                        pltpu.VMEM((1 + tq // tk, tk, tq), F32), pltpu.VMEM((2, seq, tq), F32)],
        compiler_params=_params(2),
        name="diff_attention",
    )(slopes, cq, ck, cv, lam_l, subln_g.reshape(1, hw), z)


def _mem_kernel(q_ref, k_ref, v_ref, z_ref, o_ref):
    for h in range(M_HEADS):
        hs = slice(h * HEAD_DIM, (h + 1) * HEAD_DIM)
        s_t = lax.dot_general(k_ref[:, hs], q_ref[:, hs], CONTRACT_LAST,
                              preferred_element_type=F32) * (SCALE * LOG2E)
        m = jnp.max(s_t, axis=0, keepdims=True)
        e = jnp.exp2(s_t - m)
        p_t = (e * (1.0 / jnp.sum(e, axis=0, keepdims=True))).astype(BF16)
        o = lax.dot_general(p_t, v_ref[:, hs], CONTRACT_FIRST, preferred_element_type=F32)
        o_ref[:, hs] = (o * z_ref[:, hs]).astype(o_ref.dtype)


def mem_extra(cp, mkv, z, *, seq, qcol, zcol, tq):
    t = cp.shape[0]
    nq = seq // tq
    in_specs = [
        pl.BlockSpec((tq, M_OUT), lambda j, i: (i, qcol // M_OUT)),
        pl.BlockSpec((N_MEM, M_OUT), lambda j, i: (i // nq, 0)),
        pl.BlockSpec((N_MEM, M_OUT), lambda j, i: (i // nq, 1)),
        pl.BlockSpec((tq, M_OUT), lambda j, i: (i, zcol // M_OUT)),
    ]
    out_shape = [jax.ShapeDtypeStruct((t, M_OUT), BF16)]
    out_specs = [pl.BlockSpec((tq, M_OUT), lambda j, i: (i, 0))]

    def body(in_refs, out_refs, scratch_refs, tile, proj_fn):
        proj_fn()
        _mem_kernel(*in_refs, out_refs[0])

    return in_specs, [cp, mkv, mkv, z], out_shape, out_specs, [], body


_BRANCH_ROWS = ((0, A_OUT), (A_OUT, A_OUT + B_OUT), (A_OUT + B_OUT, A_OUT + B_OUT + C_OUT),
                (A_OUT + B_OUT + C_OUT, A_OUT + B_OUT + C_OUT + M_OUT))
_WB_CHUNK = 512


def _merge_kernel(*refs):
    h_ref, us = refs[0], refs[1:5]
    wgs, bgs = refs[5:9], refs[9:13]
    n_chunk = (A_OUT + B_OUT + C_OUT + M_OUT) // _WB_CHUNK
    wbs = refs[13:13 + n_chunk]
    o_ref, wg_sc, wb_sc = refs[13 + n_chunk:]

    @pl.when(pl.program_id(1) == 0)
    def _():
        for b in range(N_BRANCH):
            wg_sc[b] = wgs[b][...].astype(BF16)
        for c in range(n_chunk):
            wb_sc[c * _WB_CHUNK:(c + 1) * _WB_CHUNK, :] = wbs[c][...].astype(BF16)

    h = h_ref[...]
    acc = None
    for b in range(N_BRANCH):
        lo, hi = _BRANCH_ROWS[b]
        gate = _sigmoid(jnp.dot(h, wg_sc[b], preferred_element_type=F32) + bgs[b][...])
        term = gate * jnp.dot(us[b][...], wb_sc[lo:hi, :], preferred_element_type=F32)
        acc = term if acc is None else acc + term
    o_ref[...] = acc.astype(o_ref.dtype)


def gated_merge(h, us, w_in, b_gate3, w_branch, layer, tm, tn):
    t, d = h.shape
    nj = d // tn
    gate0 = O_GATE // tn
    n_chunk = w_branch.shape[1] // _WB_CHUNK
    in_specs = [pl.BlockSpec((tm, d), lambda j, i: (i, 0))]
    in_specs += [pl.BlockSpec((tm, u.shape[1]), lambda j, i: (i, 0)) for u in us]
    in_specs += [pl.BlockSpec((None, d, tn), functools.partial(
        lambda j, i, b: (layer, 0, gate0 + b * nj + j), b=b)) for b in range(N_BRANCH)]
    in_specs += [pl.BlockSpec((None, 1, tn), functools.partial(
        lambda j, i, b: (layer, 0, b * nj + j), b=b)) for b in range(N_BRANCH)]
    in_specs += [pl.BlockSpec((None, _WB_CHUNK, tn), functools.partial(
        lambda j, i, c: (layer, c, j), c=c)) for c in range(n_chunk)]
    return pl.pallas_call(
        _merge_kernel,
        out_shape=jax.ShapeDtypeStruct((t, d), BF16),
        grid=(nj, t // tm),
        in_specs=in_specs,
        out_specs=pl.BlockSpec((tm, tn), lambda j, i: (i, j)),
        scratch_shapes=[pltpu.VMEM((N_BRANCH, d, tn), BF16),
                        pltpu.VMEM((w_branch.shape[1], tn), BF16)],
        compiler_params=_params(2),
        name="gated_merge",
    )(h, *us, *([w_in] * N_BRANCH), *([b_gate3] * N_BRANCH), *([w_branch] * n_chunk))


def _out_kernel(x_ref, m_ref, w_ref, o_ref, w_sc):
    @pl.when(pl.program_id(1) == 0)
    def _():
        w_sc[...] = w_ref[...].astype(BF16)

    o_ref[...] = x_ref[...] + jnp.dot(m_ref[...], w_sc[...], preferred_element_type=F32)


def out_proj(x, merged, w_out, layer, tm, tn):
    t, d = x.shape
    return pl.pallas_call(
        _out_kernel,
        out_shape=jax.ShapeDtypeStruct((t, d), F32),
        grid=(d // tn, t // tm),
        in_specs=[pl.BlockSpec((tm, tn), lambda j, i: (i, j)),
                  pl.BlockSpec((tm, d), lambda j, i: (i, 0)),
                  pl.BlockSpec((None, d, tn), lambda j, i: (layer, 0, j))],
        out_specs=pl.BlockSpec((tm, tn), lambda j, i: (i, j)),
        scratch_shapes=[pltpu.VMEM((d, tn), BF16)],
        compiler_params=_params(2),
        name="out_proj",
    )(x, merged, w_out)


def kernel(x, mem, norm_g, w_in, b_gate, qk_gain, sinks, lam, subln_g, mem_norm_g,
           w_mem_kv, w_branch, w_out):
    bsz, s_len, d = x.shape
    depth = w_in.shape[0]
    t = bsz * s_len
    xf = x.reshape(t, d)
    memf = mem.reshape(bsz * N_MEM, d)
    b_gate3 = b_gate.reshape(depth, 1, N_BRANCH * d)

    slopes_a = _alibi_slopes(A_HEADS_PER_GROUP)
    slopes_b = _alibi_slopes(B_Q_HEADS)
    slopes_c = jnp.asarray(_alibi_slopes(C_HEADS), F32)
    ones = lambda n: jnp.ones((n,), F32)
    dils = tuple(dil for _, dil in A_PATTERNS)

    pc = PROJ_PIECE
    z_zb, z_zc, z_za, z_zm = 0, B_OUT, B_OUT + C_OUT, B_OUT + C_OUT + A_OUT
    g_bm = B_OUT
    g_cq = g_bm + 2 * B_KV_HEADS * HEAD_DIM + M_OUT
    g_ck, g_cv = g_cq + C_OUT, g_cq + 2 * C_OUT

    for l in range(depth):
        gq = qk_gain[l]
        gain_a = jnp.concatenate([jnp.tile(gq[QK_A_Q], 4), jnp.tile(gq[QK_A_K], 4), ones(A_OUT)]).reshape(1, -1)
        gain_bc = jnp.concatenate([
            jnp.tile(gq[QK_B_Q], B_Q_HEADS), jnp.tile(gq[QK_B_K], B_KV_HEADS), ones(B_KV_HEADS * HEAD_DIM),
            jnp.tile(gq[QK_M_Q], M_HEADS), jnp.tile(gq[QK_C_Q], 2 * C_HEADS),
            jnp.tile(gq[QK_C_K], 2 * C_HEADS), ones(C_OUT)]).reshape(1, -1)
        gain_z = ones(O_GATE - O_ZA).reshape(1, -1)
        gain_m = jnp.concatenate([jnp.tile(gq[QK_M_K], M_HEADS), ones(M_OUT)]).reshape(1, -1)

        ap0, h, *hps = norm_proj(xf, norm_g[l], dils[1:], w_in, l, [O_AQ // pc, O_AK // pc, O_AV // pc],
                                 gain_a, "NNNNNNNNPPPP", HALF_TILE, BF16, "norm_proj_a0")
        h_perm = [h] + [hp.reshape(t, d) for hp in hps]
        bm = proj(h, w_in, l, [(O_BK // pc,), (O_MQ // pc,)], gain_bc[:, g_bm:g_cq], ("NNPPNNNN",),
                  ROW_TILE, BF16, "proj_bm")

        def a_cols(g):
            return [O_AQ // pc + g, O_AK // pc + g, O_AV // pc + g]

        def a_attn(g):
            win, dil = A_PATTERNS[g]
            return dict(rows_per_seq=s_len // dil, qcol=0, kcol=A_OUT, vcol=2 * A_OUT,
                        n_q=A_HEADS_PER_GROUP, n_kv=A_HEADS_PER_GROUP, slopes=slopes_a,
                        max_dist=win // dil, dist_scale=dil, tq=ROW_TILE, out_dtype=F32, unit=1, write_lse=True)

        a_mode = "NNNNNNNNPPPP"
        n8 = "NNNNNNNN"
        sz = proj(h, w_in, l, [(O_ZB // pc, O_ZC // pc, O_ZA // pc), (O_ZB // pc + 1, O_ZC // pc + 1, O_ZM // pc)],
                  gain_z, ("SSSSSSSS",) * 3, ROW_TILE, F32, "proj_z")
        ap1, o0, l0 = proj_fused(h_perm[1], w_in, l, a_cols(1), gain_a, a_mode, ROW_TILE, BF16,
                                 banded_extra(ap0, ap0, **a_attn(0)), "proj_a1_attn_a0")
        ap2, o1, l1 = proj_fused(h_perm[2], w_in, l, a_cols(2), gain_a, a_mode, ROW_TILE, BF16,
                                 banded_extra(ap1, ap1, **a_attn(1)), "proj_a2_attn_a1")
        bqa, o2, l2 = proj_fused(h, w_in, l, [O_BQ // pc, O_BQ // pc + 1], gain_bc[:, :g_bm], n8, ROW_TILE, BF16,
                                 banded_extra(ap2, ap2, **a_attn(2)), "proj_bq_attn_a2")
        mkv = norm_proj(memf, mem_norm_g[l], (), w_mem_kv, l, [0, 1], gain_m, "NNNNPPPP", bsz * N_MEM, BF16,
                        "norm_proj_mem")[0]
        cq, u_b = proj_fused(
            h, w_in, l, [O_CQ // pc, O_CQ // pc + 1], gain_bc[:, g_cq:g_ck], n8, ROW_TILE, BF16,
            banded_extra(bqa, bm, rows_per_seq=s_len, qcol=0, kcol=0, vcol=B_KV_HEADS * HEAD_DIM,
                         n_q=B_Q_HEADS, n_kv=B_KV_HEADS, slopes=slopes_b, max_dist=B_WINDOW - 1, dist_scale=1,
                         tq=ROW_TILE, out_dtype=BF16, batch=1, sinks=sinks[l], z=sz, zcol=z_zb), "proj_cq_attn_b")
        ck, u_m = proj_fused(
            h, w_in, l, [O_CK // pc, O_CK // pc + 1], gain_bc[:, g_ck:g_cv], n8, ROW_TILE, BF16,
            mem_extra(bm, mkv, sz, seq=s_len, qcol=2 * B_KV_HEADS * HEAD_DIM, zcol=z_zm, tq=ROW_TILE),
            "proj_ck_attn_m")
        cv, u_a = proj_fused(h, w_in, l, [O_CV // pc, O_CV // pc + 1], gain_bc[:, g_cv:], "PPPPPPPP", HALF_TILE,
                             BF16, combine_extra([o0, o1, o2], [l0, l1, l2], sz, z_za, HALF_TILE, dils[1:]),
                             "proj_cv_combine_a")

        lam_init = 0.8 - 0.6 * math.exp(-0.3 * l)
        u_c = diff_attention(cq, ck, cv, slopes_c, lam[l], subln_g[l], sz, seq=s_len,
                             zcol=z_zc, tq=DIFF_TQ, tk=DIFF_TK, lam_init=lam_init)

        merged = gated_merge(h, [u_a, u_b, u_c, u_m], w_in, b_gate3, w_branch, l, ROW_TILE, MERGE_TN)
        xf = out_proj(xf, merged, w_out, l, ROW_TILE, OUT_TN)

    return xf.reshape(bsz, s_len, d)
```

```python
import functools
import math

import jax
import jax.numpy as jnp
from jax import lax
from jax.experimental import pallas as pl
from jax.experimental.pallas import tpu as pltpu

D_MODEL = 2048
HEAD_DIM = 128
BLOCK = 128
EPS = 1e-6
A_PATTERNS = ((128, 1), (512, 4), (2048, 16))
A_GROUPS = 3
A_HEADS_PER_GROUP = 4
A_HEADS = A_GROUPS * A_HEADS_PER_GROUP
A_OUT = A_HEADS_PER_GROUP * HEAD_DIM
B_Q_HEADS = 8
B_KV_HEADS = 2
B_WINDOW = 128
B_OUT = B_Q_HEADS * HEAD_DIM
C_HEADS = 4
C_OUT = C_HEADS * 2 * HEAD_DIM
N_MEM = 256
M_HEADS = 4
M_OUT = M_HEADS * HEAD_DIM
N_BRANCH = 4
QK_A_Q, QK_A_K, QK_B_Q, QK_B_K, QK_C_Q, QK_C_K, QK_M_Q, QK_M_K = range(8)

_IN_SIZES = (A_HEADS * HEAD_DIM, A_HEADS * HEAD_DIM, A_HEADS * HEAD_DIM,
             B_Q_HEADS * HEAD_DIM, B_KV_HEADS * HEAD_DIM, B_KV_HEADS * HEAD_DIM,
             C_HEADS * 2 * HEAD_DIM, C_HEADS * 2 * HEAD_DIM, C_OUT,
             M_HEADS * HEAD_DIM, A_OUT, B_OUT, C_OUT, M_OUT, N_BRANCH * D_MODEL)
_OFF = [0]
for _s in _IN_SIZES:
    _OFF.append(_OFF[-1] + _s)
(O_AQ, O_AK, O_AV, O_BQ, O_BK, O_BV, O_CQ, O_CK, O_CV, O_MQ,
 O_ZA, O_ZB, O_ZC, O_ZM, O_GATE, O_END) = _OFF

NEG = -1e30
SCALE = HEAD_DIM ** -0.5
VMEM_LIMIT = 58 * 1024 * 1024
PROJ_PIECE = 512
ROW_TILE = 1024
HALF_TILE = 512
MERGE_TN = 256
OUT_TN = 1024
DIFF_TQ, DIFF_TK = 512, 512
DIFF_UNROLL = (4, 3, 2, 1)
LOG2E = math.log2(math.e)
BF16 = jnp.bfloat16
F32 = jnp.float32
CONTRACT_LAST = (((1,), (1,)), ((), ()))
CONTRACT_FIRST = (((0,), (0,)), ((), ()))


def _params(n_axes):
    return pltpu.CompilerParams(dimension_semantics=("arbitrary",) * n_axes,
                                vmem_limit_bytes=VMEM_LIMIT)


def _sigmoid(x):
    return 0.5 * jnp.tanh(0.5 * x) + 0.5


def _alibi_slopes(n):
    return [2.0 ** (-8.0 * i / n) for i in range(1, n + 1)]


def _norm_proj_kernel(*refs, dils, tm, sub, n_piece, tile_mode):
    x_ref, g_ref = refs[0], refs[1]
    w_refs, gain_ref = refs[2:2 + n_piece], refs[2 + n_piece]
    o_ref, h_ref = refs[3 + n_piece], refs[4 + n_piece]
    perm_refs = refs[5 + n_piece:5 + n_piece + len(dils)]
    w_sc = refs[5 + n_piece + len(dils)]
    x = x_ref[...]
    ms = jnp.mean(x * x, axis=-1, keepdims=True)
    h_ref[...] = (x * lax.rsqrt(ms + EPS) * g_ref[...]).astype(BF16)
    _proj_body(h_ref, w_refs, gain_ref, o_ref, w_sc, (tile_mode,))
    row = lax.broadcasted_iota(jnp.int32, (sub, sub), 0)
    col = lax.broadcasted_iota(jnp.int32, (sub, sub), 1)
    for d, p_ref in zip(dils, perm_refs):
        n = sub // d
        perm = jnp.where(col == (row % n) * d + row // n, 1.0, 0.0).astype(BF16)
        for s0 in range(tm // sub):
            hp = jnp.dot(perm, h_ref[s0 * sub:(s0 + 1) * sub, :], preferred_element_type=F32).astype(BF16)
            for p in range(d):
                p_ref[p, s0 * n:(s0 + 1) * n, :] = hp[p * n:(p + 1) * n, :]


def norm_proj(x, g, dils, w3, layer, piece_cols, gain, tile_mode, tm, out_dtype, name):
    t, d = x.shape
    n_piece = len(piece_cols)
    tn = n_piece * PROJ_PIECE
    sub = 256
    assert len(tile_mode) == tn // HEAD_DIM and gain.shape == (1, tn) and tm % sub == 0
    out_shape = [jax.ShapeDtypeStruct((t, tn), out_dtype), jax.ShapeDtypeStruct((t, d), BF16)]
    out_specs = [pl.BlockSpec((tm, tn), lambda j, i: (i, 0)), pl.BlockSpec((tm, d), lambda j, i: (i, 0))]
    for dil in dils:
        out_shape.append(jax.ShapeDtypeStruct((dil, t // dil, d), BF16))
        out_specs.append(pl.BlockSpec((dil, tm // dil, d), lambda j, i: (0, i, 0)))
    w_specs = [pl.BlockSpec((None, d, PROJ_PIECE), functools.partial(lambda j, i, c: (layer, 0, c), c=c))
               for c in piece_cols]
    return pl.pallas_call(
        functools.partial(_norm_proj_kernel, dils=tuple(dils), tm=tm, sub=sub, n_piece=n_piece,
                          tile_mode=tile_mode),
        out_shape=out_shape,
        grid=(1, t // tm),
        in_specs=[pl.BlockSpec((tm, d), lambda j, i: (i, 0)), pl.BlockSpec((1, d), lambda j, i: (0, 0))]
                 + w_specs + [pl.BlockSpec((1, tn), lambda j, i: (0, 0))],
        out_specs=out_specs,
        scratch_shapes=[pltpu.VMEM((d, tn), BF16)],
        compiler_params=_params(2),
        name=name,
    )(x, g.reshape(1, d), *([w3] * n_piece), gain)


def _proj_kernel(*refs, tile_modes, n_piece):
    h_ref, w_refs = refs[0], refs[1:1 + n_piece]
    gain_ref, o_ref, w_sc = refs[1 + n_piece:]
    _proj_body(h_ref, w_refs, gain_ref, o_ref, w_sc, tile_modes)


def _proj_body(h_ref, w_refs, gain_ref, o_ref, w_sc, tile_modes):
    _proj_cast(w_refs, w_sc)
    _proj_compute(h_ref, gain_ref, o_ref, w_sc, tile_modes)


def _proj_cast(w_refs, w_sc):
    @pl.when(pl.program_id(1) == 0)
    def _():
        for p in range(len(w_refs)):
            w_sc[:, p * PROJ_PIECE:(p + 1) * PROJ_PIECE] = w_refs[p][...].astype(BF16)


def _proj_compute(h_ref, gain_ref, o_ref, w_sc, tile_modes):
    j = pl.program_id(0)
    acc = jnp.dot(h_ref[...], w_sc[...], preferred_element_type=F32)
    for c in range(len(tile_modes[0])):
        sl = slice(c * HEAD_DIM, (c + 1) * HEAD_DIM)
        modes = [m[c] for m in tile_modes]
        y = acc[:, sl]
        if 'S' in modes:
            assert set(modes) == {'S'}
            y = y * _sigmoid(y)
        elif 'N' in modes:
            ms = jnp.mean(y * y, axis=-1, keepdims=True)
            normed = y * lax.rsqrt(ms + EPS) * gain_ref[:, sl]
            if set(modes) == {'N'}:
                y = normed
            else:
                is_norm = functools.reduce(jnp.logical_or, [j == jj for jj, m in enumerate(modes) if m == 'N'])
                y = jnp.where(is_norm, normed, y)
        o_ref[:, sl] = y.astype(o_ref.dtype)


def _select(j, values):
    out = values[-1]
    for jj in range(len(values) - 2, -1, -1):
        out = jnp.where(j == jj, values[jj], out)
    return out


def proj(h, w3, layer, piece_cols, gain, tile_modes, tm, out_dtype, name):
    t, k = h.shape
    n_piece = len(piece_cols)
    tn = n_piece * PROJ_PIECE
    n_tiles = len(tile_modes)
    assert t % tm == 0 and all(len(m) == tn // HEAD_DIM for m in tile_modes)
    assert all(len(cols) == n_tiles for cols in piece_cols) and gain.shape == (1, tn * n_tiles)
    w_specs = [pl.BlockSpec((None, k, PROJ_PIECE),
                            functools.partial(lambda j, i, cols: (layer, 0, _select(j, cols)), cols=tuple(cols)))
               for cols in piece_cols]
    return pl.pallas_call(
        functools.partial(_proj_kernel, tile_modes=tuple(tile_modes), n_piece=n_piece),
        out_shape=jax.ShapeDtypeStruct((t, tn * n_tiles), out_dtype),
        grid=(n_tiles, t // tm),
        in_specs=[pl.BlockSpec((tm, k), lambda j, i: (i, 0))] + w_specs
                 + [pl.BlockSpec((1, tn), lambda j, i: (0, j))],
        out_specs=pl.BlockSpec((tm, tn), lambda j, i: (i, j)),
        scratch_shapes=[pltpu.VMEM((k, tn), BF16)],
        compiler_params=_params(2),
        name=name,
    )(h, *([w3] * n_piece), gain)


def _banded_body(q_ref, kc_ref, kp_ref, vc_ref, vp_ref, sink_ref, z_ref, o_ref, lse_ref, *, tile,
                 n_q, n_kv, slopes, max_dist, dist_scale, tq, rows_per_seq, unit, batch, between=None):
    has_sink, mul_z, write_lse = sink_ref is not None, z_ref is not None, lse_ref is not None
    shared_kv = n_kv < n_q and unit > 1
    group = n_q // n_kv
    assert n_q % unit == 0 and (not shared_kv or group == unit)
    kk = lax.broadcasted_iota(jnp.int32, (2 * BLOCK, BLOCK), 0)
    r = lax.broadcasted_iota(jnp.int32, (2 * BLOCK, BLOCK), 1)
    dist = r + BLOCK - kk
    distf = (dist * dist_scale).astype(F32)
    valid = (dist >= 0) & (dist <= max_dist)
    valid_start = valid & (kk >= BLOCK)
    if rows_per_seq > tq:
        has_prev = jnp.minimum(tile % (rows_per_seq // tq), 1) * BLOCK
        valid0 = valid & (kk + has_prev >= BLOCK)
    else:
        valid0 = valid_start

    def window(cur_ref, prev_ref, kv, j):
        ks = slice(kv * HEAD_DIM, (kv + 1) * HEAD_DIM)
        prev = prev_ref[:, ks] if j == 0 else cur_ref[(j - 1) * BLOCK:j * BLOCK, ks]
        return jnp.concatenate([prev, cur_ref[j * BLOCK:(j + 1) * BLOCK, ks]], axis=0)

    units = [(list(range(u * unit, (u + 1) * unit)), j)
             for u in range(n_q // unit) for j in range(tq // BLOCK)]

    for b0 in range(0, len(units), batch or len(units)):
        batch_units = units[b0:b0 + (batch or len(units))]
        scores = []
        for heads, j in batch_units:
            rows = slice(j * BLOCK, (j + 1) * BLOCK)
            qs = [q_ref[rows, h * HEAD_DIM:(h + 1) * HEAD_DIM] for h in heads]
            if shared_kv:
                s_t = lax.dot_general(window(kc_ref, kp_ref, heads[0] // group, j), jnp.concatenate(qs, axis=0),
                                      CONTRACT_LAST, preferred_element_type=F32)
            else:
                s_t = jnp.concatenate(
                    [lax.dot_general(window(kc_ref, kp_ref, h // group, j), q, CONTRACT_LAST,
                                     preferred_element_type=F32) for h, q in zip(heads, qs)], axis=1)
            scores.append(s_t)
        if between is not None and b0 == 0:
            between()

        probs, lses = [], []
        for (heads, j), s_t in zip(batch_units, scores):
            v_mask = valid0 if j == 0 else (valid_start if (j * BLOCK) % rows_per_seq == 0 else valid)
            bias = jnp.concatenate([jnp.where(v_mask, (-slopes[h] * LOG2E) * distf, NEG) for h in heads], axis=1)
            s_t = s_t * (SCALE * LOG2E) + bias
            m = jnp.max(s_t, axis=0, keepdims=True)
            if has_sink:
                sink = jnp.concatenate([jnp.full((1, BLOCK), sink_ref[h] * LOG2E, F32) for h in heads], axis=1)
                m = jnp.maximum(m, sink)
            e = jnp.exp2(s_t - m)
            denom = jnp.sum(e, axis=0, keepdims=True)
            if has_sink:
                denom = denom + jnp.exp2(sink - m)
            probs.append((e * (1.0 / denom)).astype(BF16))
            lses.append(m + jnp.log2(denom) if write_lse else None)

        for (heads, j), p_t, lse in zip(batch_units, probs, lses):
            rows = slice(j * BLOCK, (j + 1) * BLOCK)
            if shared_kv:
                o_all = lax.dot_general(p_t, window(vc_ref, vp_ref, heads[0] // group, j), CONTRACT_FIRST,
                                        preferred_element_type=F32)
            for g, h in enumerate(heads):
                hs = slice(h * HEAD_DIM, (h + 1) * HEAD_DIM)
                gs = slice(g * BLOCK, (g + 1) * BLOCK)
                if shared_kv:
                    o = o_all[gs, :]
                else:
                    o = lax.dot_general(p_t[:, gs], window(vc_ref, vp_ref, h // group, j), CONTRACT_FIRST,
                                        preferred_element_type=F32)
                if mul_z:
                    o = o * z_ref[rows, hs]
                o_ref[rows, hs] = o.astype(o_ref.dtype)
                if write_lse:
                    lse_ref[rows, hs] = jnp.broadcast_to(lse[:, gs], (BLOCK, BLOCK)).T


def _banded_specs(q_arr, kv_arr, *, qcol, kcol, vcol, n_q, n_kv, tq, out_dtype, sinks, z, zcol, write_lse,
                  tile_of):
    rows_total = q_arr.shape[0]
    qw, kw = n_q * HEAD_DIM, n_kv * HEAD_DIM
    assert qcol % qw == 0 and kcol % kw == 0 and vcol % kw == 0
    sub = tq // BLOCK

    def cur(col, w):
        return lambda *g: (tile_of(*g), col // w)

    def prev(col, w):
        return lambda *g: (jnp.maximum(tile_of(*g) * sub - 1, 0), col // w)

    in_specs = [
        pl.BlockSpec((tq, qw), cur(qcol, qw)),
        pl.BlockSpec((tq, kw), cur(kcol, kw)),
        pl.BlockSpec((BLOCK, kw), prev(kcol, kw)),
        pl.BlockSpec((tq, kw), cur(vcol, kw)),
        pl.BlockSpec((BLOCK, kw), prev(vcol, kw)),
    ]
    args = [q_arr, kv_arr, kv_arr, kv_arr, kv_arr]
    if sinks is not None:
        in_specs.append(pl.BlockSpec(memory_space=pltpu.SMEM))
        args.append(sinks)
    if z is not None:
        assert zcol % qw == 0
        in_specs.append(pl.BlockSpec((tq, qw), cur(zcol, qw)))
        args.append(z)
    out_block = pl.BlockSpec((tq, qw), cur(0, qw))
    out_shape = [jax.ShapeDtypeStruct((rows_total, qw), out_dtype)]
    out_specs = [out_block]
    if write_lse:
        out_shape.append(jax.ShapeDtypeStruct((rows_total, qw), F32))
        out_specs.append(out_block)
    return in_specs, args, out_shape, out_specs


def _banded_cfg(*, rows_per_seq, n_q, n_kv, slopes, max_dist, dist_scale, tq, unit, batch):
    assert tq % rows_per_seq == 0 or rows_per_seq % tq == 0
    return dict(n_q=n_q, n_kv=n_kv, slopes=tuple(slopes), max_dist=max_dist, dist_scale=dist_scale,
                tq=tq, rows_per_seq=rows_per_seq, unit=unit, batch=batch)


def _proj_fused_kernel(*refs, tile_mode, n_piece, n_in, n_out, body, cast_first):
    h_ref, w_refs, gain_ref = refs[0], refs[1:1 + n_piece], refs[1 + n_piece]
    extra_in = refs[2 + n_piece:2 + n_piece + n_in]
    p_out = refs[2 + n_piece + n_in]
    extra_out = refs[3 + n_piece + n_in:3 + n_piece + n_in + n_out]
    w_sc = refs[3 + n_piece + n_in + n_out]
    extra_scratch = refs[4 + n_piece + n_in + n_out:]
    if cast_first:
        _proj_cast(w_refs, w_sc)
        proj_fn = lambda: _proj_compute(h_ref, gain_ref, p_out, w_sc, (tile_mode,))
    else:
        proj_fn = lambda: _proj_body(h_ref, w_refs, gain_ref, p_out, w_sc, (tile_mode,))
    body(extra_in, extra_out, extra_scratch, pl.program_id(1), proj_fn)


def proj_fused(h, w3, layer, piece_cols, gain, tile_mode, tm, out_dtype, extra, name, cast_first=True):
    t, k = h.shape
    n_piece = len(piece_cols)
    tn = n_piece * PROJ_PIECE
    assert len(tile_mode) == tn // HEAD_DIM and gain.shape == (1, tn)
    e_specs, e_args, e_shape, e_out_specs, e_scratch, body = extra
    w_specs = [pl.BlockSpec((None, k, PROJ_PIECE), functools.partial(lambda j, i, c: (layer, 0, c), c=c))
               for c in piece_cols]
    return pl.pallas_call(
        functools.partial(_proj_fused_kernel, tile_mode=tile_mode, n_piece=n_piece,
                          n_in=len(e_specs), n_out=len(e_shape), body=body, cast_first=cast_first),
        out_shape=[jax.ShapeDtypeStruct((t, tn), out_dtype)] + list(e_shape),
        grid=(1, t // tm),
        in_specs=[pl.BlockSpec((tm, k), lambda j, i: (i, 0))] + w_specs
                 + [pl.BlockSpec((1, tn), lambda j, i: (0, 0))] + list(e_specs),
        out_specs=[pl.BlockSpec((tm, tn), lambda j, i: (i, 0))] + list(e_out_specs),
        scratch_shapes=[pltpu.VMEM((k, tn), BF16)] + list(e_scratch),
        compiler_params=_params(2),
        name=name,
    )(h, *([w3] * n_piece), gain, *e_args)


def banded_extra(q_arr, kv_arr, *, rows_per_seq, qcol, kcol, vcol, n_q, n_kv, slopes, max_dist, dist_scale,
                 tq, out_dtype, unit=4, batch=None, sinks=None, z=None, zcol=0, write_lse=False):
    in_specs, args, out_shape, out_specs = _banded_specs(
        q_arr, kv_arr, qcol=qcol, kcol=kcol, vcol=vcol, n_q=n_q, n_kv=n_kv, tq=tq, out_dtype=out_dtype,
        sinks=sinks, z=z, zcol=zcol, write_lse=write_lse, tile_of=lambda j, i: i)
    cfg = _banded_cfg(rows_per_seq=rows_per_seq, n_q=n_q, n_kv=n_kv, slopes=slopes, max_dist=max_dist,
                      dist_scale=dist_scale, tq=tq, unit=unit, batch=batch)
    has_sink, mul_z = sinks is not None, z is not None

    def body(in_refs, out_refs, scratch_refs, tile, proj_fn):
        in_refs = list(in_refs)
        sink_ref = in_refs[5] if has_sink else None
        z_ref = in_refs[5 + has_sink] if mul_z else None
        lse_ref = out_refs[1] if write_lse else None
        _banded_body(*in_refs[:5], sink_ref, z_ref, out_refs[0], lse_ref, tile=tile, between=proj_fn, **cfg)

    return in_specs, args, out_shape, out_specs, [], body


def _combine_kernel(o0, o1, o2, l0, l1, l2, z_ref, u_ref, o1_sc, o2_sc, l1_sc, l2_sc, *, tm, dils):
    n_chunk = o0.shape[1] // HEAD_DIM
    for src, dst, d in ((o1, o1_sc, dils[0]), (l1, l1_sc, dils[0]),
                        (o2, o2_sc, dils[1]), (l2, l2_sc, dils[1])):
        n = tm // d
        for p in range(d):
            for c in range(n_chunk):
                dst[c, pl.ds(p, n, stride=d), :] = src[p, :, c * HEAD_DIM:(c + 1) * HEAD_DIM]
    for c in range(n_chunk):
        cs = slice(c * HEAD_DIM, (c + 1) * HEAD_DIM)
        a0, a1, a2 = l0[:, cs], l1_sc[c], l2_sc[c]
        m = jnp.maximum(jnp.maximum(a0, a1), a2)
        e0, e1, e2 = jnp.exp2(a0 - m), jnp.exp2(a1 - m), jnp.exp2(a2 - m)
        inv = 1.0 / (e0 + e1 + e2)
        ya = (e0 * inv) * o0[:, cs] + (e1 * inv) * o1_sc[c] + (e2 * inv) * o2_sc[c]
        u_ref[:, cs] = (ya * z_ref[:, cs]).astype(u_ref.dtype)


def combine_extra(outs, lses, z, zcol, tm, dils):
    t, w = outs[0].shape
    blk = pl.BlockSpec((tm, w), lambda j, i: (i, 0))

    def pblk(d):
        return pl.BlockSpec((d, tm // d, w), lambda j, i: (0, i, 0))

    def p3(a, d):
        return a.reshape(d, t // d, w)

    d1, d2 = dils
    in_specs = [blk, pblk(d1), pblk(d2), blk, pblk(d1), pblk(d2),
                pl.BlockSpec((tm, w), lambda j, i: (i, zcol // w))]
    args = [outs[0], p3(outs[1], d1), p3(outs[2], d2), lses[0], p3(lses[1], d1), p3(lses[2], d2), z]

    def body(in_refs, out_refs, scratch_refs, tile, proj_fn):
        _combine_kernel(*in_refs, out_refs[0], *scratch_refs, tm=tm, dils=dils)
        proj_fn()

    return (in_specs, args, [jax.ShapeDtypeStruct((t, w), BF16)], [blk],
            [pltpu.VMEM((w // HEAD_DIM, tm, HEAD_DIM), F32)] * 4, body)


def _diff_kernel(slope_ref, q_ref, k_ref, v_ref, lam_ref, g_ref, z_ref, o_ref,
                 vt_sc, acc_sc, bias_sc, s_sc, *, tq, tk, seq, lam_init):
    slope = slope_ref[pl.program_id(1)]
    per_q = tq // tk
    to_log2 = SCALE * LOG2E

    for cidx in range(seq // tk):
        rows = slice(cidx * tk, (cidx + 1) * tk)
        vt_sc[:, rows] = v_ref[rows, :].astype(F32).T.astype(BF16)
    kk = lax.broadcasted_iota(jnp.int32, (tk, tq), 0)
    r = lax.broadcasted_iota(jnp.int32, (tk, tq), 1)
    bias = (-slope / SCALE) * (r - kk).astype(F32)
    bias_sc[0] = bias
    for dblk in range(per_q):
        bias_sc[1 + dblk] = jnp.where(r >= kk + dblk * tk, bias, NEG)

    lp = lam_ref[...]
    lam_full = (jnp.exp(jnp.sum(lp[0:1] * lp[1:2], axis=-1, keepdims=True))
                - jnp.exp(jnp.sum(lp[2:3] * lp[3:4], axis=-1, keepdims=True)) + lam_init)
    out_gain = g_ref[...] * (1.0 - lam_init)

    def query_tile(qi, carry):
        base = qi * per_q
        q0 = pl.multiple_of(qi * tq, tq)
        qs = [q_ref[pl.ds(q0, tq), c * HEAD_DIM:(c + 1) * HEAD_DIM] for c in range(2)]

        def shift_of(kj):
            return (slope * LOG2E) * ((base - kj) * tk).astype(F32)

        def max_chunk(kj0, n, ms):
            ms = list(ms)
            for u in range(n):
                kj = kj0 + u
                start = pl.multiple_of(kj * tk, tk)
                b = bias_sc[jnp.maximum(kj - base + 1, 0)]
                for comp in range(2):
                    k = k_ref[pl.ds(start, tk), comp * HEAD_DIM:(comp + 1) * HEAD_DIM]
                    a = (lax.dot_general(k, qs[comp], CONTRACT_LAST, preferred_element_type=F32) + b) * to_log2
                    s_sc[comp, pl.ds(start, tk), :] = a
                    ms[comp] = jnp.maximum(ms[comp], jnp.max(a, axis=0, keepdims=True) - shift_of(kj))
            return tuple(ms)

        def over_blocks(chunk_fn, carry):
            done = 0
            for n in DIFF_UNROLL:
                if n % per_q:
                    continue
                n_it = (base + per_q - done) // n
                carry = lax.fori_loop(
                    0, n_it, lambda t, c, n=n, done=done: chunk_fn(done + t * n, n, c), carry)
                done = done + n_it * n
            return carry

        neg = jnp.full((1, tq), NEG, F32)
        ms = over_blocks(max_chunk, (neg, neg))

        def exp_chunk(kj0, n, ls):
            ls = list(ls)
            start0 = pl.multiple_of(kj0 * tk, tk)
            vt = vt_sc[:, pl.ds(start0, n * tk)]
            for comp in range(2):
                es = []
                for u in range(n):
                    start = pl.multiple_of((kj0 + u) * tk, tk)
                    e = jnp.exp2(s_sc[comp, pl.ds(start, tk), :] - (ms[comp] + shift_of(kj0 + u)))
                    ls[comp] = ls[comp] + jnp.sum(e, axis=0, keepdims=True)
                    es.append(e.astype(BF16))
                e_cat = es[0] if n == 1 else jnp.concatenate(es, axis=0)
                acc_sc[comp] += jnp.dot(vt, e_cat, preferred_element_type=F32)
            return tuple(ls)

        acc_sc[...] = jnp.zeros(acc_sc.shape, F32)
        zero = jnp.zeros((1, tq), F32)
        l1, l2 = over_blocks(exp_chunk, (zero, zero))

        y_t = acc_sc[0] * (1.0 / l1) - lam_full * (acc_sc[1] * (1.0 / l2))
        ms2 = jnp.mean(y_t * y_t, axis=0, keepdims=True)
        y = (y_t * lax.rsqrt(ms2 + EPS)).T * out_gain
        o_ref[pl.ds(q0, tq), :] = (y * z_ref[pl.ds(q0, tq), :]).astype(o_ref.dtype)
        return carry

    lax.fori_loop(0, seq // tq, query_tile, 0)


def diff_attention(cq, ck, cv, slopes, lam_l, subln_g, z, *, seq, zcol, tq, tk, lam_init):
    t = cq.shape[0]
    n_batch = t // seq
    hw = 2 * HEAD_DIM
    kern = functools.partial(_diff_kernel, tq=tq, tk=tk, seq=seq, lam_init=lam_init)
    whole = pl.BlockSpec((seq, hw), lambda b, h: (b, h))
    return pl.pallas_call(
        kern,
        out_shape=jax.ShapeDtypeStruct((t, C_OUT), BF16),
        grid=(n_batch, C_HEADS),
        in_specs=[
            pl.BlockSpec(memory_space=pltpu.SMEM),
            whole, whole, whole,
            pl.BlockSpec((4, HEAD_DIM), lambda b, h: (0, 0)),
            pl.BlockSpec((1, hw), lambda b, h: (0, 0)),
            pl.BlockSpec((seq, hw), lambda b, h: (b, zcol // hw + h)),
        ],
        out_specs=whole,
        scratch_shapes=[pltpu.VMEM((hw, seq), BF16), pltpu.VMEM((2, hw, tq), F32),
                        pltpu.VMEM((1 + tq // tk, tk, tq), F32), pltpu.VMEM((2, seq, tq), F32)],
        compiler_params=_params(2),
        name="diff_attention",
    )(slopes, cq, ck, cv, lam_l, subln_g.reshape(1, hw), z)


def _mem_kernel(q_ref, k_ref, v_ref, z_ref, o_ref):
    for h in range(M_HEADS):
        hs = slice(h * HEAD_DIM, (h + 1) * HEAD_DIM)
        s_t = lax.dot_general(k_ref[:, hs], q_ref[:, hs], CONTRACT_LAST,
                              preferred_element_type=F32) * (SCALE * LOG2E)
        m = jnp.max(s_t, axis=0, keepdims=True)
        e = jnp.exp2(s_t - m)
        p_t = (e * (1.0 / jnp.sum(e, axis=0, keepdims=True))).astype(BF16)
        o = lax.dot_general(p_t, v_ref[:, hs], CONTRACT_FIRST, preferred_element_type=F32)
        o_ref[:, hs] = (o * z_ref[:, hs]).astype(o_ref.dtype)


def mem_extra(cp, mkv, z, *, seq, qcol, zcol, tq):
    t = cp.shape[0]
    nq = seq // tq
    in_specs = [
        pl.BlockSpec((tq, M_OUT), lambda j, i: (i, qcol // M_OUT)),
        pl.BlockSpec((N_MEM, M_OUT), lambda j, i: (i // nq, 0)),
        pl.BlockSpec((N_MEM, M_OUT), lambda j, i: (i // nq, 1)),
        pl.BlockSpec((tq, M_OUT), lambda j, i: (i, zcol // M_OUT)),
    ]
    out_shape = [jax.ShapeDtypeStruct((t, M_OUT), BF16)]
    out_specs = [pl.BlockSpec((tq, M_OUT), lambda j, i: (i, 0))]

    def body(in_refs, out_refs, scratch_refs, tile, proj_fn):
        proj_fn()
        _mem_kernel(*in_refs, out_refs[0])

    return in_specs, [cp, mkv, mkv, z], out_shape, out_specs, [], body


_BRANCH_ROWS = ((0, A_OUT), (A_OUT, A_OUT + B_OUT), (A_OUT + B_OUT, A_OUT + B_OUT + C_OUT),
                (A_OUT + B_OUT + C_OUT, A_OUT + B_OUT + C_OUT + M_OUT))
_WB_CHUNK = 512


def _merge_kernel(*refs):
    h_ref, us = refs[0], refs[1:5]
    wgs, bgs = refs[5:9], refs[9:13]
    n_chunk = (A_OUT + B_OUT + C_OUT + M_OUT) // _WB_CHUNK
    wbs = refs[13:13 + n_chunk]
    o_ref, wg_sc, wb_sc = refs[13 + n_chunk:]

    @pl.when(pl.program_id(1) == 0)
    def _():
        for b in range(N_BRANCH):
            wg_sc[b] = wgs[b][...].astype(BF16)
        for c in range(n_chunk):
            wb_sc[c * _WB_CHUNK:(c + 1) * _WB_CHUNK, :] = wbs[c][...].astype(BF16)

    h = h_ref[...]
    acc = None
    for b in range(N_BRANCH):
        lo, hi = _BRANCH_ROWS[b]
        gate = _sigmoid(jnp.dot(h, wg_sc[b], preferred_element_type=F32) + bgs[b][...])
        term = gate * jnp.dot(us[b][...], wb_sc[lo:hi, :], preferred_element_type=F32)
        acc = term if acc is None else acc + term
    o_ref[...] = acc.astype(o_ref.dtype)


def gated_merge(h, us, w_in, b_gate3, w_branch, layer, tm, tn):
    t, d = h.shape
    nj = d // tn
    gate0 = O_GATE // tn
    n_chunk = w_branch.shape[1] // _WB_CHUNK
    in_specs = [pl.BlockSpec((tm, d), lambda j, i: (i, 0))]
    in_specs += [pl.BlockSpec((tm, u.shape[1]), lambda j, i: (i, 0)) for u in us]
    in_specs += [pl.BlockSpec((None, d, tn), functools.partial(
        lambda j, i, b: (layer, 0, gate0 + b * nj + j), b=b)) for b in range(N_BRANCH)]
    in_specs += [pl.BlockSpec((None, 1, tn), functools.partial(
        lambda j, i, b: (layer, 0, b * nj + j), b=b)) for b in range(N_BRANCH)]
    in_specs += [pl.BlockSpec((None, _WB_CHUNK, tn), functools.partial(
        lambda j, i, c: (layer, c, j), c=c)) for c in range(n_chunk)]
    return pl.pallas_call(
        _merge_kernel,
        out_shape=jax.ShapeDtypeStruct((t, d), BF16),
        grid=(nj, t // tm),
        in_specs=in_specs,
        out_specs=pl.BlockSpec((tm, tn), lambda j, i: (i, j)),
        scratch_shapes=[pltpu.VMEM((N_BRANCH, d, tn), BF16),
                        pltpu.VMEM((w_branch.shape[1], tn), BF16)],
        compiler_params=_params(2),
        name="gated_merge",
    )(h, *us, *([w_in] * N_BRANCH), *([b_gate3] * N_BRANCH), *([w_branch] * n_chunk))


def _out_kernel(x_ref, m_ref, w_ref, o_ref, w_sc):
    @pl.when(pl.program_id(1) == 0)
    def _():
        w_sc[...] = w_ref[...].astype(BF16)

    o_ref[...] = x_ref[...] + jnp.dot(m_ref[...], w_sc[...], preferred_element_type=F32)


def out_proj(x, merged, w_out, layer, tm, tn):
    t, d = x.shape
    return pl.pallas_call(
        _out_kernel,
        out_shape=jax.ShapeDtypeStruct((t, d), F32),
        grid=(d // tn, t // tm),
        in_specs=[pl.BlockSpec((tm, tn), lambda j, i: (i, j)),
                  pl.BlockSpec((tm, d), lambda j, i: (i, 0)),
                  pl.BlockSpec((None, d, tn), lambda j, i: (layer, 0, j))],
        out_specs=pl.BlockSpec((tm, tn), lambda j, i: (i, j)),
        scratch_shapes=[pltpu.VMEM((d, tn), BF16)],
        compiler_params=_params(2),
        name="out_proj",
    )(x, merged, w_out)


def kernel(x, mem, norm_g, w_in, b_gate, qk_gain, sinks, lam, subln_g, mem_norm_g,
           w_mem_kv, w_branch, w_out):
    bsz, s_len, d = x.shape
    depth = w_in.shape[0]
    t = bsz * s_len
    xf = x.reshape(t, d)
    memf = mem.reshape(bsz * N_MEM, d)
    b_gate3 = b_gate.reshape(depth, 1, N_BRANCH * d)

    slopes_a = _alibi_slopes(A_HEADS_PER_GROUP)
    slopes_b = _alibi_slopes(B_Q_HEADS)
    slopes_c = jnp.asarray(_alibi_slopes(C_HEADS), F32)
    ones = lambda n: jnp.ones((n,), F32)
    dils = tuple(dil for _, dil in A_PATTERNS)

    pc = PROJ_PIECE
    z_zb, z_zc, z_za, z_zm = 0, B_OUT, B_OUT + C_OUT, B_OUT + C_OUT + A_OUT
    g_bm = B_OUT
    g_cq = g_bm + 2 * B_KV_HEADS * HEAD_DIM + M_OUT
    g_ck, g_cv = g_cq + C_OUT, g_cq + 2 * C_OUT

    for l in range(depth):
        gq = qk_gain[l]
        gain_a = jnp.concatenate([jnp.tile(gq[QK_A_Q], 4), jnp.tile(gq[QK_A_K], 4), ones(A_OUT)]).reshape(1, -1)
        gain_bc = jnp.concatenate([
            jnp.tile(gq[QK_B_Q], B_Q_HEADS), jnp.tile(gq[QK_B_K], B_KV_HEADS), ones(B_KV_HEADS * HEAD_DIM),
            jnp.tile(gq[QK_M_Q], M_HEADS), jnp.tile(gq[QK_C_Q], 2 * C_HEADS),
            jnp.tile(gq[QK_C_K], 2 * C_HEADS), ones(C_OUT)]).reshape(1, -1)
        gain_z = ones(O_GATE - O_ZA).reshape(1, -1)
        gain_m = jnp.concatenate([jnp.tile(gq[QK_M_K], M_HEADS), ones(M_OUT)]).reshape(1, -1)

        ap0, h, *hps = norm_proj(xf, norm_g[l], dils[1:], w_in, l, [O_AQ // pc, O_AK // pc, O_AV // pc],
                                 gain_a, "NNNNNNNNPPPP", HALF_TILE, BF16, "norm_proj_a0")
        h_perm = [h] + [hp.reshape(t, d) for hp in hps]
        bm = proj(h, w_in, l, [(O_BK // pc,), (O_MQ // pc,)], gain_bc[:, g_bm:g_cq], ("NNPPNNNN",),
                  ROW_TILE, BF16, "proj_bm")

        def a_cols(g):
            return [O_AQ // pc + g, O_AK // pc + g, O_AV // pc + g]

        def a_attn(g):
            win, dil = A_PATTERNS[g]
            return dict(rows_per_seq=s_len // dil, qcol=0, kcol=A_OUT, vcol=2 * A_OUT,
                        n_q=A_HEADS_PER_GROUP, n_kv=A_HEADS_PER_GROUP, slopes=slopes_a,
                        max_dist=win // dil, dist_scale=dil, tq=ROW_TILE, out_dtype=F32, unit=1, write_lse=True)

        a_mode = "NNNNNNNNPPPP"
        n8 = "NNNNNNNN"
        sz = proj(h, w_in, l, [(O_ZB // pc, O_ZC // pc, O_ZA // pc), (O_ZB // pc + 1, O_ZC // pc + 1, O_ZM // pc)],
                  gain_z, ("SSSSSSSS",) * 3, ROW_TILE, F32, "proj_z")
        ap1, o0, l0 = proj_fused(h_perm[1], w_in, l, a_cols(1), gain_a, a_mode, ROW_TILE, BF16,
                                 banded_extra(ap0, ap0, **a_attn(0)), "proj_a1_attn_a0")
        ap2, o1, l1 = proj_fused(h_perm[2], w_in, l, a_cols(2), gain_a, a_mode, ROW_TILE, BF16,
                                 banded_extra(ap1, ap1, **a_attn(1)), "proj_a2_attn_a1")
        bqa, o2, l2 = proj_fused(h, w_in, l, [O_BQ // pc, O_BQ // pc + 1], gain_bc[:, :g_bm], n8, ROW_TILE, BF16,
                                 banded_extra(ap2, ap2, **a_attn(2)), "proj_bq_attn_a2")
        mkv = norm_proj(memf, mem_norm_g[l], (), w_mem_kv, l, [0, 1], gain_m, "NNNNPPPP", bsz * N_MEM, BF16,
                        "norm_proj_mem")[0]
        cq, u_b = proj_fused(
            h, w_in, l, [O_CQ // pc, O_CQ // pc + 1], gain_bc[:, g_cq:g_ck], n8, ROW_TILE, BF16,
            banded_extra(bqa, bm, rows_per_seq=s_len, qcol=0, kcol=0, vcol=B_KV_HEADS * HEAD_DIM,
                         n_q=B_Q_HEADS, n_kv=B_KV_HEADS, slopes=slopes_b, max_dist=B_WINDOW - 1, dist_scale=1,
                         tq=ROW_TILE, out_dtype=BF16, batch=1, sinks=sinks[l], z=sz, zcol=z_zb), "proj_cq_attn_b")
        ck, u_m = proj_fused(
            h, w_in, l, [O_CK // pc, O_CK // pc + 1], gain_bc[:, g_ck:g_cv], n8, ROW_TILE, BF16,
            mem_extra(bm, mkv, sz, seq=s_len, qcol=2 * B_KV_HEADS * HEAD_DIM, zcol=z_zm, tq=ROW_TILE),
            "proj_ck_attn_m")
        cv, u_a = proj_fused(h, w_in, l, [O_CV // pc, O_CV // pc + 1], gain_bc[:, g_cv:], "PPPPPPPP", HALF_TILE,
                             BF16, combine_extra([o0, o1, o2], [l0, l1, l2], sz, z_za, HALF_TILE, dils[1:]),
                             "proj_cv_combine_a", cast_first=False)

        lam_init = 0.8 - 0.6 * math.exp(-0.3 * l)
        u_c = diff_attention(cq, ck, cv, slopes_c, lam[l], subln_g[l], sz, seq=s_len,
                             zcol=z_zc, tq=DIFF_TQ, tk=DIFF_TK, lam_init=lam_init)

        merged = gated_merge(h, [u_a, u_b, u_c, u_m], w_in, b_gate3, w_branch, l, ROW_TILE, MERGE_TN)
        xf = out_proj(xf, merged, w_out, l, ROW_TILE, OUT_TN)

    return xf.reshape(bsz, s_len, d)
```
